```python
import jax, jax.numpy as jnp
from jax import lax
import numpy as np

D_MODEL = 1024
BATCH = 2
SEQ = 16384
DEPTH = 2

DENSE_HEAD_DIM = 128
N_FOX_HEADS = 4
N_SB_HEADS = 4
FOX_W = N_FOX_HEADS * DENSE_HEAD_DIM
SB_W = N_SB_HEADS * DENSE_HEAD_DIM
EVEN_WIDTH = FOX_W + SB_W
EVEN_SIZES = (FOX_W, FOX_W, FOX_W, N_FOX_HEADS, SB_W, SB_W, SB_W, EVEN_WIDTH)
EVEN_IN = sum(EVEN_SIZES)
DIL_HEAD_DIM = 64
DILATED_PAIRS = ((128, 1), (512, 4), (2048, 16))
N_DIL_GROUPS = len(DILATED_PAIRS)
N_DIL_HEADS = 8
DIL_W = N_DIL_GROUPS * N_DIL_HEADS * DIL_HEAD_DIM
ODD_WIDTH = N_DIL_HEADS * DIL_HEAD_DIM
ODD_SIZES = (DIL_W, DIL_W, DIL_W, ODD_WIDTH)
ODD_IN = sum(ODD_SIZES)
Q_BLOCK = 128
RMS_EPS = 1e-6
N_EVEN = (DEPTH + 1) // 2
N_ODD = DEPTH // 2

kernel_name = "hybrid_fox_stickbreak_dilated_gated"


def rmsnorm(x, g):
    xf = x.astype(jnp.float32)
    y = xf * lax.rsqrt(jnp.mean(xf * xf, axis=-1, keepdims=True) + RMS_EPS)
    return (y * g.astype(jnp.float32)).astype(x.dtype)


def split_points(sizes):
    return np.cumsum(np.array(sizes))[:-1].tolist()


def split_heads(t, n, hd):
    b, s, _ = t.shape
    return t.reshape(b, s, n, hd).transpose(0, 2, 1, 3)


def merge_heads(t):
    b, n, s, hd = t.shape
    return t.transpose(0, 2, 1, 3).reshape(b, s, n * hd)


def alibi_slopes(n):
    return jnp.asarray(2.0 ** (-8.0 * np.arange(1, n + 1) / n), dtype=jnp.float32)


def forgetting_attention(q, k, v, log_f):
    s = q.shape[2]
    q = q * jnp.asarray(DENSE_HEAD_DIM ** -0.5, q.dtype)
    cum = jnp.cumsum(log_f, axis=-1)
    outs = []
    for i in range(s // Q_BLOCK):
        start, end = i * Q_BLOCK, (i + 1) * Q_BLOCK
        qpos = start + jnp.arange(Q_BLOCK)
        causal = jnp.arange(end)[None, :] <= qpos[:, None]
        logits = jnp.einsum('bhqd,bhkd->bhqk', q[:, :, start:end], k[:, :, :end]).astype(jnp.float32)
        logits = logits + (cum[:, :, start:end, None] - cum[:, :, None, :end])
        p = jax.nn.softmax(jnp.where(causal, logits, -jnp.inf), axis=-1)
        outs.append(jnp.einsum('bhqk,bhkd->bhqd', p.astype(v.dtype), v[:, :, :end]))
    return jnp.concatenate(outs, axis=2)


def stick_breaking_attention(q, k, v):
    b, h, s, _ = q.shape
    q = q * jnp.asarray(DENSE_HEAD_DIM ** -0.5, q.dtype)
    c = jnp.arange(Q_BLOCK)
    upper_incl = (c[:, None] >= c[None, :]).astype(jnp.float32)
    outs = []
    for i in range(s // Q_BLOCK):
        start, end = i * Q_BLOCK, (i + 1) * Q_BLOCK
        nb = i + 1
        qpos = start + jnp.arange(Q_BLOCK)
        strict = jnp.arange(end)[None, :] < qpos[:, None]
        z = jnp.einsum('bhqd,bhkd->bhqk', q[:, :, start:end], k[:, :, :end]).astype(jnp.float32)
        log_beta = jax.nn.log_sigmoid(z)
        log_one_minus = jnp.where(strict, log_beta - z, 0.0)
        lob = log_one_minus.reshape(b, h, Q_BLOCK, nb, Q_BLOCK)
        incl = jnp.einsum('bhqnc,cd->bhqnd', lob, upper_incl)
        n_idx = jnp.arange(nb)
        later_blocks = (n_idx[:, None] > n_idx[None, :]).astype(jnp.float32)
        off = jnp.einsum('bhqn,nm->bhqm', jnp.sum(lob, axis=-1), later_blocks)
        later = (incl - lob + off[..., None]).reshape(b, h, Q_BLOCK, end)
        w = jnp.where(strict, jnp.exp(log_beta + later), 0.0)
        outs.append(jnp.einsum('bhqk,bhkd->bhqd', w.astype(v.dtype), v[:, :, :end]))
    return jnp.concatenate(outs, axis=2)


def dilated_group(q, k, v, window, dil, slopes):
    b, h, s, hd = q.shape
    length = s // dil
    blk = min(Q_BLOCK, length)
    nblk = length // blk
    span = window // dil

    def residues(t):
        return t.reshape(b, h, length, dil, hd).transpose(0, 1, 3, 2, 4).reshape(b, h, dil, nblk, blk, hd)

    def with_prev(t):
        prev = jnp.pad(t[:, :, :, :-1], ((0, 0), (0, 0), (0, 0), (1, 0), (0, 0), (0, 0)))
        return jnp.concatenate([prev, t], axis=4)

    qb = residues(q) * jnp.asarray(DIL_HEAD_DIM ** -0.5, q.dtype)
    kw = with_prev(residues(k))
    vw = with_prev(residues(v))
    a = jnp.arange(blk)[:, None]
    c = jnp.arange(2 * blk)[None, :]
    dist_sub = a - c + blk
    key_idx = jnp.arange(nblk)[:, None, None] * blk + c[None] - blk
    valid = (dist_sub >= 0) & (dist_sub <= span) & (key_idx >= 0)
    logits = jnp.einsum('bhrnqd,bhrnkd->bhrnqk', qb, kw).astype(jnp.float32)
    logits = logits - slopes[:, None, None, None, None] * (dist_sub * dil).astype(jnp.float32)
    logits = jnp.where(valid, logits, -jnp.inf)
    m = jnp.max(logits, axis=-1, keepdims=True)
    p = jnp.exp(logits - m)
    den = jnp.sum(p, axis=-1)
    o = jnp.einsum('bhrnqk,bhrnkd->bhrnqd', p.astype(vw.dtype), vw).astype(jnp.float32) / den[..., None]

    def back(t):
        extra = t.shape[5:]
        t = t.reshape(b, h, dil, length, *extra)
        t = jnp.moveaxis(t, 2, 3)
        return t.reshape(b, h, s, *extra)

    return back(o), back(m[..., 0]), back(den)


def dilated_window_attention(q, k, v):
    slopes = alibi_slopes(N_DIL_GROUPS * N_DIL_HEADS).reshape(N_DIL_GROUPS, N_DIL_HEADS)
    maxes, dens, outs = [], [], []
    for g, (window, dil) in enumerate(DILATED_PAIRS):
        o, m, den = dilated_group(q[g], k[g], v[g], window, dil, slopes[g])
        maxes.append(m); dens.append(den); outs.append(o)
    m_all = jnp.stack(maxes)
    den_all = jnp.stack(dens)
    o_all = jnp.stack(outs)
    wts = den_all * jnp.exp(m_all - jnp.max(m_all, axis=0))
    wts = wts / jnp.sum(wts, axis=0)
    return jnp.sum(wts[..., None] * o_all, axis=0).astype(v.dtype)


def even_layer(x, g_norm, w_in, b_f, g_q, g_k, w_out):
    h = rmsnorm(x, g_norm)
    proj = h @ w_in
    fq, fk, fv, f_logit, sq, sk, sv, gate = jnp.split(proj, split_points(EVEN_SIZES), axis=-1)
    log_f = jax.nn.log_sigmoid((f_logit + b_f).astype(jnp.float32)).transpose(0, 2, 1)
    fq = rmsnorm(split_heads(fq, N_FOX_HEADS, DENSE_HEAD_DIM), g_q)
    fk = rmsnorm(split_heads(fk, N_FOX_HEADS, DENSE_HEAD_DIM), g_k)
    fox = forgetting_attention(fq, fk, split_heads(fv, N_FOX_HEADS, DENSE_HEAD_DIM), log_f)
    sb = stick_breaking_attention(split_heads(sq, N_SB_HEADS, DENSE_HEAD_DIM),
                                  split_heads(sk, N_SB_HEADS, DENSE_HEAD_DIM),
                                  split_heads(sv, N_SB_HEADS, DENSE_HEAD_DIM))
    mixed = jnp.concatenate([merge_heads(fox), merge_heads(sb)], axis=-1) * jax.nn.silu(gate)
    return x + mixed @ w_out


def odd_layer(x, g_norm, w_in, g_q, g_k, w_out):
    h = rmsnorm(x, g_norm)
    proj = h @ w_in
    q, k, v, gate = jnp.split(proj, split_points(ODD_SIZES), axis=-1)
    b, s, _ = x.shape

    def groups(t):
        return t.reshape(b, s, N_DIL_GROUPS, N_DIL_HEADS, DIL_HEAD_DIM).transpose(2, 0, 3, 1, 4)

    q = rmsnorm(groups(q), g_q)
    k = rmsnorm(groups(k), g_k)
    att = dilated_window_attention(q, k, groups(v))
    mixed = merge_heads(att) * jax.nn.silu(gate)
    return x + mixed @ w_out


def setup_inputs(seed: int = 0) -> dict:
    key = jax.random.key(seed)
    ks = jax.random.split(key, 13)
    f32 = jnp.float32
    x = jax.random.normal(ks[0], (BATCH, SEQ, D_MODEL), f32)
    even_norm = 1.0 + 0.02 * jax.random.normal(ks[1], (N_EVEN, D_MODEL), f32)
    even_w_in = jax.random.normal(ks[2], (N_EVEN, D_MODEL, EVEN_IN), f32) * D_MODEL ** -0.5
    even_b_f = (jnp.linspace(1.0, 4.0, N_FOX_HEADS, dtype=f32)[None, :]
                + 0.1 * jax.random.normal(ks[3], (N_EVEN, N_FOX_HEADS), f32))
    even_q_gain = 1.0 + 0.02 * jax.random.normal(ks[4], (N_EVEN, DENSE_HEAD_DIM), f32)
    even_k_gain = 1.0 + 0.02 * jax.random.normal(ks[5], (N_EVEN, DENSE_HEAD_DIM), f32)
    even_w_out = jax.random.normal(ks[6], (N_EVEN, EVEN_WIDTH, D_MODEL), f32) * EVEN_WIDTH ** -0.5
    odd_norm = 1.0 + 0.02 * jax.random.normal(ks[7], (N_ODD, D_MODEL), f32)
    odd_w_in = jax.random.normal(ks[8], (N_ODD, D_MODEL, ODD_IN), f32) * D_MODEL ** -0.5
    odd_q_gain = 1.0 + 0.02 * jax.random.normal(ks[9], (N_ODD, DIL_HEAD_DIM), f32)
    odd_k_gain = 1.0 + 0.02 * jax.random.normal(ks[10], (N_ODD, DIL_HEAD_DIM), f32)
    odd_w_out = jax.random.normal(ks[11], (N_ODD, ODD_WIDTH, D_MODEL), f32) * ODD_WIDTH ** -0.5
    return {"x": x, "even_norm": even_norm, "even_w_in": even_w_in, "even_b_f": even_b_f,
            "even_q_gain": even_q_gain, "even_k_gain": even_k_gain, "even_w_out": even_w_out,
            "odd_norm": odd_norm, "odd_w_in": odd_w_in, "odd_q_gain": odd_q_gain,
            "odd_k_gain": odd_k_gain, "odd_w_out": odd_w_out}


def reference(x, even_norm, even_w_in, even_b_f, even_q_gain, even_k_gain, even_w_out,
              odd_norm, odd_w_in, odd_q_gain, odd_k_gain, odd_w_out):
    h = x
    for layer in range(DEPTH):
        i = layer // 2
        if layer % 2 == 0:
            h = even_layer(h, even_norm[i], even_w_in[i], even_b_f[i], even_q_gain[i],
                           even_k_gain[i], even_w_out[i])
        else:
            h = odd_layer(h, odd_norm[i], odd_w_in[i], odd_q_gain[i], odd_k_gain[i], odd_w_out[i])
    return h
```

```python
import functools
import math

import jax
import jax.numpy as jnp
from jax import lax
from jax.experimental import pallas as pl
from jax.experimental.pallas import tpu as pltpu

F32 = jnp.float32
BF16 = jnp.bfloat16

D_MODEL = 1024
DENSE_HEAD_DIM = 128
N_DENSE_HEADS = 4
DENSE_W = N_DENSE_HEADS * DENSE_HEAD_DIM
DIL_HEAD_DIM = 64
N_DIL_HEADS = 8
DIL_GROUP_W = N_DIL_HEADS * DIL_HEAD_DIM
DILATIONS = (1, 4, 16)
DIL_SPAN = 128
DIL_BLK = 128
RMS_EPS = 1e-6

LANES = 128
PROJ_ROWS = 512
ATT_TILE = 256
DIL_POS = DILATIONS[-1] * DIL_BLK
VMEM_LIMIT = 56 * 1024 * 1024

EXP_ZERO = 104.0

_NT = (((1,), (1,)), ((), ()))


def _cparams(sem):
    return pltpu.CompilerParams(dimension_semantics=sem, vmem_limit_bytes=VMEM_LIMIT)


def _log_sigmoid_pair(z):
    soft = jnp.log(1.0 + jnp.exp(-jnp.abs(z)))
    log_beta = jnp.minimum(z, 0.0) - soft
    return log_beta, log_beta - z


def _silu(x):
    return x * (1.0 / (1.0 + jnp.exp(-x)))


def _row_rms(x, g):
    ms = jnp.mean(x * x, axis=-1, keepdims=True)
    return x * lax.rsqrt(ms + RMS_EPS) * g


def _even_proj_kernel(x_ref, gn_ref, w_ref, wf_ref, bf_ref, gq_ref, gk_ref,
                      fq_ref, fk_ref, fv_ref, sq_ref, sk_ref, sv_ref, sg_ref, cum_ref, carry_sc):
    i = pl.program_id(1)
    rows = x_ref.shape[0]
    h = _row_rms(x_ref[...], gn_ref[...]).astype(BF16)

    def proj(c0, width):
        return jnp.dot(h, w_ref[:, c0:c0 + width], preferred_element_type=F32)

    def head_norm(acc, g_ref, out_ref):
        for hh in range(N_DENSE_HEADS):
            sl = slice(hh * DENSE_HEAD_DIM, (hh + 1) * DENSE_HEAD_DIM)
            out_ref[:, sl] = _row_rms(acc[:, sl], g_ref[...]).astype(BF16)

    head_norm(proj(0, DENSE_W), gq_ref, fq_ref)
    head_norm(proj(DENSE_W, DENSE_W), gk_ref, fk_ref)
    fv_ref[...] = proj(2 * DENSE_W, DENSE_W).astype(BF16)
    sq_ref[...] = (proj(3 * DENSE_W, DENSE_W) * (DENSE_HEAD_DIM ** -0.5)).astype(BF16)
    sk_ref[...] = proj(4 * DENSE_W, DENSE_W).astype(BF16)
    sv_ref[...] = proj(5 * DENSE_W, DENSE_W).astype(BF16)
    for c in range(2):
        sl = slice(c * DENSE_W, (c + 1) * DENSE_W)
        sg_ref[:, sl] = _silu(proj(6 * DENSE_W + c * DENSE_W, DENSE_W)).astype(BF16)

    fl = lax.dot_general(wf_ref[...], h, _NT, preferred_element_type=F32)[:8] + bf_ref[:, 0:1]
    lf, _ = _log_sigmoid_pair(fl)
    lane = lax.broadcasted_iota(jnp.int32, lf.shape, 1)
    shift = 1
    while shift < rows:
        lf = lf + jnp.where(lane >= shift, pltpu.roll(lf, shift, axis=1), 0.0)
        shift *= 2

    @pl.when(i == 0)
    def _():
        carry_sc[...] = jnp.zeros_like(carry_sc)

    cum = lf + carry_sc[:, 0:1]
    cum_ref[...] = cum
    carry_sc[...] = jnp.broadcast_to(cum[:, rows - 1:rows], carry_sc.shape)


def _even_proj(x, g_norm, w, wf, b_f, g_q, g_k):
    B, S, D = x.shape
    tm = PROJ_ROWS
    row = lambda width: pl.BlockSpec((None, tm, width), lambda b, i: (b, i, 0))
    const = lambda shape: pl.BlockSpec(shape, lambda b, i: (0,) * len(shape))
    act = lambda width: jax.ShapeDtypeStruct((B, S, width), BF16)
    return pl.pallas_call(
        _even_proj_kernel,
        grid=(B, S // tm),
        in_specs=[row(D), const((1, D)), const(w.shape), const(wf.shape), const(b_f.shape),
                  const(g_q.shape), const(g_k.shape)],
        out_specs=[row(DENSE_W)] * 6 + [row(2 * DENSE_W),
                                        pl.BlockSpec((None, 8, tm), lambda b, i: (b, 0, i))],
        out_shape=[act(DENSE_W)] * 6 + [act(2 * DENSE_W), jax.ShapeDtypeStruct((B, 8, S), F32)],
        scratch_shapes=[pltpu.VMEM((8, LANES), F32)],
        compiler_params=_cparams(("arbitrary", "arbitrary")),
        name="even_proj",
    )(x, g_norm, w, wf, b_f, g_q, g_k)


def _fox_kernel(cend_ref, thr_ref, q_ref, k_ref, v_ref, c_ref, sg_ref, o_ref, m_sc, l_sc, acc_sc):
    b, h, i = pl.program_id(0), pl.program_id(1), pl.program_id(2)
    t = q_ref.shape[0]
    row_id = b * N_DENSE_HEADS + h
    q = q_ref[...]
    c_base = jnp.where(i > 0, cend_ref[row_id, jnp.maximum(i - 1, 0)], 0.0)

    m_sc[...] = jnp.full_like(m_sc, -jnp.inf)
    l_sc[...] = jnp.zeros_like(l_sc)
    acc_sc[...] = jnp.zeros_like(acc_sc)

    def tile(j, diagonal):
        start = pl.multiple_of(j * t, t)
        k = k_ref[pl.ds(start, t), :]
        v = v_ref[pl.ds(start, t), :]
        s = lax.dot_general(q, k, _NT, preferred_element_type=F32)
        s = s + (c_base - c_ref[pl.ds(h, 1), pl.ds(start, t)])
        if diagonal:
            r_io = lax.broadcasted_iota(jnp.int32, s.shape, 0)
            c_io = lax.broadcasted_iota(jnp.int32, s.shape, 1)
            s = jnp.where(c_io <= r_io, s, -jnp.inf)
        m_prev = m_sc[...]
        m_new = jnp.maximum(m_prev, jnp.max(s, axis=-1, keepdims=True))
        alpha = jnp.exp(m_prev - m_new)
        p = jnp.exp(s - m_new)
        l_sc[...] = alpha * l_sc[...] + jnp.sum(p, axis=-1, keepdims=True)
        acc_sc[...] = alpha * acc_sc[...] + jnp.dot(p.astype(BF16), v, preferred_element_type=F32)
        m_sc[...] = m_new

    tile(i, True)

    def wanted(j):
        return jnp.logical_and(j >= 0, c_base - cend_ref[row_id, jnp.maximum(j, 0)] >= -thr_ref[0])

    def body(c):
        j, _ = c
        tile(j, False)
        return j - 1, wanted(j - 1)

    lax.while_loop(lambda c: c[1], body, (i - 1, wanted(i - 1)))
    o_ref[...] = (acc_sc[...] * (1.0 / l_sc[...]) * sg_ref[...].astype(F32)).astype(BF16)


def _fox_attn(cend, thr, fq, fk, fv, cum, sgate):
    B, S, _ = fq.shape
    t = ATT_TILE
    tile = pl.BlockSpec((None, t, DENSE_HEAD_DIM), lambda b, h, i: (b, i, h))
    full = pl.BlockSpec((None, S, DENSE_HEAD_DIM), lambda b, h, i: (b, 0, h))
    smem = pl.BlockSpec(memory_space=pltpu.SMEM)
    return pl.pallas_call(
        _fox_kernel,
        grid=(B, N_DENSE_HEADS, S // t),
        in_specs=[smem, smem, tile, full, full,
                  pl.BlockSpec((None, 8, S), lambda b, h, i: (b, 0, 0)), tile],
        out_specs=tile,
        out_shape=jax.ShapeDtypeStruct((B, S, DENSE_W), BF16),
        scratch_shapes=[pltpu.VMEM((t, 1), F32), pltpu.VMEM((t, 1), F32),
                        pltpu.VMEM((t, DENSE_HEAD_DIM), F32)],
        compiler_params=_cparams(("arbitrary", "arbitrary", "arbitrary")),
        name="fox_attn",
    )(cend, thr, fq, fk, fv, cum, sgate)


def _sb_kernel(q_ref, k_ref, v_ref, u_ref, sg_ref, o_ref, off_sc, acc_sc):
    i = pl.program_id(2)
    t = q_ref.shape[0]
    q = q_ref[...]
    off_sc[...] = jnp.zeros_like(off_sc)
    acc_sc[...] = jnp.zeros_like(acc_sc)

    def tile(j, diagonal):
        start = pl.multiple_of(j * t, t)
        k = k_ref[pl.ds(start, t), :]
        v = v_ref[pl.ds(start, t), :]
        z = lax.dot_general(q, k, _NT, preferred_element_type=F32)
        log_beta, log_om = _log_sigmoid_pair(z)
        if diagonal:
            r_io = lax.broadcasted_iota(jnp.int32, z.shape, 0)
            c_io = lax.broadcasted_iota(jnp.int32, z.shape, 1)
            strict = c_io < r_io
            log_om = jnp.where(strict, log_om, 0.0)
        later = jnp.dot(log_om.astype(BF16), u_ref[...], preferred_element_type=F32) + off_sc[...]
        w = jnp.exp(log_beta + later)
        if diagonal:
            w = jnp.where(strict, w, 0.0)
        acc_sc[...] += jnp.dot(w.astype(BF16), v, preferred_element_type=F32)
        off_sc[...] += jnp.sum(log_om, axis=-1, keepdims=True)

    tile(i, True)

    def body(c):
        j, _ = c
        tile(j, False)
        alive = jnp.max(off_sc[...]) >= -(EXP_ZERO + 8.0)
        return j - 1, jnp.logical_and(j >= 1, alive)

    lax.while_loop(lambda c: c[1], body, (i - 1, i >= 1))
    o_ref[...] = (acc_sc[...] * sg_ref[...].astype(F32)).astype(BF16)


def _sb_attn(sq, sk, sv, upper, sgate):
    B, S, _ = sq.shape
    t = ATT_TILE
    tile = pl.BlockSpec((None, t, DENSE_HEAD_DIM), lambda b, h, i: (b, i, h))
    gate_tile = pl.BlockSpec((None, t, DENSE_HEAD_DIM), lambda b, h, i: (b, i, N_DENSE_HEADS + h))
    full = pl.BlockSpec((None, S, DENSE_HEAD_DIM), lambda b, h, i: (b, 0, h))
    return pl.pallas_call(
        _sb_kernel,
        grid=(B, N_DENSE_HEADS, S // t),
        in_specs=[tile, full, full, pl.BlockSpec((t, t), lambda b, h, i: (0, 0)), gate_tile],
        out_specs=tile,
        out_shape=jax.ShapeDtypeStruct((B, S, DENSE_W), BF16),
        scratch_shapes=[pltpu.VMEM((t, 1), F32), pltpu.VMEM((t, DENSE_HEAD_DIM), F32)],
        compiler_params=_cparams(("arbitrary", "arbitrary", "arbitrary")),
        name="sb_attn",
    )(sq, sk, sv, upper, sgate)


def _even_out_kernel(x_ref, mf_ref, ms_ref, wf_ref, ws_ref, o_ref):
    o_ref[...] = (x_ref[...]
                  + jnp.dot(mf_ref[...], wf_ref[...], preferred_element_type=F32)
                  + jnp.dot(ms_ref[...], ws_ref[...], preferred_element_type=F32))


def _even_out(x, mf, ms, wo_f, wo_s):
    B, S, D = x.shape
    tm = PROJ_ROWS
    row = lambda width: pl.BlockSpec((None, tm, width), lambda b, i: (b, i, 0))
    const = lambda shape: pl.BlockSpec(shape, lambda b, i: (0,) * len(shape))
    return pl.pallas_call(
        _even_out_kernel,
        grid=(B, S // tm),
        in_specs=[row(D), row(DENSE_W), row(DENSE_W), const(wo_f.shape), const(wo_s.shape)],
        out_specs=row(D),
        out_shape=jax.ShapeDtypeStruct((B, S, D), F32),
        compiler_params=_cparams(("arbitrary", "arbitrary")),
        name="even_out",
    )(x, mf, ms, wo_f, wo_s)


def _odd_out_kernel(x_ref, m_ref, w_ref, o_ref):
    o_ref[...] = x_ref[...] + jnp.dot(m_ref[...], w_ref[...], preferred_element_type=F32)


def _odd_out(x, mixed, wo):
    B, S, D = x.shape
    tm = PROJ_ROWS
    row = lambda width: pl.BlockSpec((None, tm, width), lambda b, i: (b, i, 0))
    return pl.pallas_call(
        _odd_out_kernel,
        grid=(B, S // tm),
        in_specs=[row(D), row(DIL_GROUP_W), pl.BlockSpec(wo.shape, lambda b, i: (0, 0))],
        out_specs=row(D),
        out_shape=jax.ShapeDtypeStruct((B, S, D), F32),
        compiler_params=_cparams(("arbitrary", "arbitrary")),
        name="odd_out",
    )(x, mixed, wo)


def _odd_proj_kernel(x_ref, gn_ref, w_ref, bd_ref, gq_ref, gk_ref, *rest):
    out_refs, sg_ref, tmp_sc = rest[:9], rest[9], rest[10]
    rows = x_ref.shape[0]
    h = _row_rms(x_ref[...], gn_ref[...]).astype(BF16)

    def head_norm(acc, g_ref):
        ms = jnp.dot((acc * acc).astype(BF16), bd_ref[...], preferred_element_type=F32)
        return acc * lax.rsqrt(ms + RMS_EPS) * g_ref[...]

    for c in range(9):
        kind, g = divmod(c, 3)
        acc = jnp.dot(h, w_ref[:, c * DIL_GROUP_W:(c + 1) * DIL_GROUP_W], preferred_element_type=F32)
        if kind == 0:
            acc = head_norm(acc, gq_ref)
        elif kind == 1:
            acc = head_norm(acc, gk_ref)
        dil = DILATIONS[g]
        if dil == 1:
            out_refs[c][0] = acc.astype(BF16)
        else:
            for cb in range(DIL_GROUP_W // LANES):
                tmp_sc[cb] = acc[:, cb * LANES:(cb + 1) * LANES]
            for r in range(dil):
                for cb in range(DIL_GROUP_W // LANES):
                    out_refs[c][r, :, cb * LANES:(cb + 1) * LANES] = (
                        tmp_sc[cb, pl.ds(r, rows // dil, stride=dil), :].astype(BF16))
    gate = jnp.dot(h, w_ref[:, 9 * DIL_GROUP_W:10 * DIL_GROUP_W], preferred_element_type=F32)
    sg_ref[...] = _silu(gate).astype(BF16)


def _odd_proj(x, g_norm, w, bd, g_q, g_k):
    B, S, D = x.shape
    tm = PROJ_ROWS
    const = lambda shape: pl.BlockSpec(shape, lambda b, i: (0,) * len(shape))
    out_specs, out_shape = [], []
    for _ in range(3):
        for dil in DILATIONS:
            out_specs.append(pl.BlockSpec((None, dil, tm // dil, DIL_GROUP_W), lambda b, i: (b, 0, i, 0)))
            out_shape.append(jax.ShapeDtypeStruct((B, dil, S // dil, DIL_GROUP_W), BF16))
    out_specs.append(pl.BlockSpec((None, tm, DIL_GROUP_W), lambda b, i: (b, i, 0)))
    out_shape.append(jax.ShapeDtypeStruct((B, S, DIL_GROUP_W), BF16))
    return pl.pallas_call(
        _odd_proj_kernel,
        grid=(B, S // tm),
        in_specs=[pl.BlockSpec((None, tm, D), lambda b, i: (b, i, 0)), const((1, D)), const(w.shape),
                  const(bd.shape), const(g_q.shape), const(g_k.shape)],
        out_specs=out_specs,
        out_shape=out_shape,
        scratch_shapes=[pltpu.VMEM((DIL_GROUP_W // LANES, tm, LANES), F32)],
        compiler_params=_cparams(("arbitrary", "arbitrary")),
        name="odd_proj",
    )(x, g_norm, w, bd, g_q, g_k)


def _dil_kernel(slope_ref, *refs):
    ins, (sg_ref, o_ref), scr = refs[:15], refs[15:17], refs[17:]
    kbufs, vbufs, (o_sc, m_sc, l_sc) = scr[0:3], scr[3:6], scr[6:9]
    hp, st = pl.program_id(1), pl.program_id(2)
    blk = DIL_BLK

    a_io = lax.broadcasted_iota(jnp.int32, (blk, 2 * blk), 0)
    c_io = lax.broadcasted_iota(jnp.int32, (blk, 2 * blk), 1)
    dist = a_io - c_io + blk
    band = jnp.logical_and(dist >= 0, dist <= DIL_SPAN)
    neg = jnp.where(band, 0.0, -jnp.inf)
    neg_first = jnp.where(jnp.logical_and(band, c_io >= blk), 0.0, -jnp.inf)
    dist_f = dist.astype(F32)
    lane_lo = lax.broadcasted_iota(jnp.int32, (blk, LANES), 1) < DIL_HEAD_DIM

    for g, dil in enumerate(DILATIONS):
        q_ref, kc_ref, kh_ref, vc_ref, vh_ref = ins[5 * g:5 * g + 5]
        kbuf, vbuf = kbufs[g], vbufs[g]
        length = DIL_POS // dil
        nb = length // blk
        kbuf[:, 0:blk, :] = kh_ref[...]
        kbuf[:, blk:, :] = kc_ref[...]
        vbuf[:, 0:blk, :] = vh_ref[...]
        vbuf[:, blk:, :] = vc_ref[...]
        bias = []
        for hd in range(2):
            slope = slope_ref[g * N_DIL_HEADS + 2 * hp + hd]
            alibi = dist_f * (slope * float(dil))
            bias.append((neg - alibi, neg_first - alibi))

        def block(idx, carry, g=g, dil=dil, nb=nb, q_ref=q_ref, kbuf=kbuf, vbuf=vbuf, bias=bias):
            r = idx // nb
            n = idx - r * nb
            row0 = pl.multiple_of(n * blk, blk)
            q2 = q_ref[r, pl.ds(row0, blk), :]
            k2 = kbuf[r, pl.ds(row0, 2 * blk), :]
            v2 = vbuf[r, pl.ds(row0, 2 * blk), :]
            first = jnp.logical_and(st == 0, n == 0)
            outs = []
            for hd in range(2):
                keep = lane_lo if hd == 0 else jnp.logical_not(lane_lo)
                qm = jnp.where(keep, q2, jnp.zeros_like(q2))
                s = lax.dot_general(qm, k2, _NT, preferred_element_type=F32)
                s = s + jnp.where(first, bias[hd][1], bias[hd][0])
                m = jnp.max(s, axis=-1, keepdims=True)
                p = jnp.exp(s - m)
                l = jnp.sum(p, axis=-1, keepdims=True)
                o = jnp.dot(p.astype(BF16), v2, preferred_element_type=F32)
                outs.append((o, m, l))
            o = jnp.where(lane_lo, outs[0][0], outs[1][0])
            m = jnp.where(lane_lo, outs[0][1], outs[1][1])
            l = jnp.where(lane_lo, outs[0][2], outs[1][2])
            start = n * (blk * dil) + r
            if dil == 1:
                dst = pl.ds(pl.multiple_of(start, blk), blk)
            else:
                dst = pl.ds(start, blk, stride=dil)
            o_sc[g, dst, :] = o
            m_sc[g, dst, :] = m
            l_sc[g, dst, :] = l
            return carry

        lax.fori_loop(0, dil * nb, block, 0)

    chunk = 256

    def merge(ci, carry):
        rows = pl.ds(pl.multiple_of(ci * chunk, chunk), chunk)
        ms = [m_sc[g, rows, :] for g in range(3)]
        m_all = jnp.maximum(jnp.maximum(ms[0], ms[1]), ms[2])
        num = jnp.zeros((chunk, LANES), F32)
        den = jnp.zeros((chunk, LANES), F32)
        for g in range(3):
            e = jnp.exp(ms[g] - m_all)
            num = num + e * o_sc[g, rows, :]
            den = den + e * l_sc[g, rows, :]
        o_ref[rows, :] = (num / den * sg_ref[rows, :].astype(F32)).astype(BF16)
        return carry

    lax.fori_loop(0, DIL_POS // chunk, merge, 0)


def _dil_attn(slopes, qkv, sgate):
    B, S, _ = sgate.shape
    n_pairs = N_DIL_HEADS * DIL_HEAD_DIM // LANES
    blk = DIL_BLK
    in_specs = [pl.BlockSpec(memory_space=pltpu.SMEM)]
    args = [slopes]
    scratch_k, scratch_v = [], []
    for g, dil in enumerate(DILATIONS):
        length = DIL_POS // dil
        nb = length // blk
        cur = pl.BlockSpec((None, dil, length, LANES), lambda b, hp, st: (b, 0, st, hp))
        halo = pl.BlockSpec((None, dil, blk, LANES),
                            lambda b, hp, st, nb=nb: (b, 0, jnp.maximum(st * nb - 1, 0), hp))
        q, k, v = qkv[g], qkv[3 + g], qkv[6 + g]
        in_specs += [cur, cur, halo, cur, halo]
        args += [q, k, k, v, v]
        scratch_k.append(pltpu.VMEM((dil, length + blk, LANES), BF16))
        scratch_v.append(pltpu.VMEM((dil, length + blk, LANES), BF16))
    pos = pl.BlockSpec((None, DIL_POS, LANES), lambda b, hp, st: (b, st, hp))
    in_specs.append(pos)
    args.append(sgate)
    return pl.pallas_call(
        _dil_kernel,
        grid=(B, n_pairs, S // DIL_POS),
        in_specs=in_specs,
        out_specs=pos,
        out_shape=jax.ShapeDtypeStruct((B, S, DIL_GROUP_W), BF16),
        scratch_shapes=scratch_k + scratch_v + [pltpu.VMEM((3, DIL_POS, LANES), F32)] * 3,
        compiler_params=_cparams(("arbitrary", "arbitrary", "arbitrary")),
        name="dil_attn",
    )(*args)


def _even_layer(x, g_norm, w_in, b_f, g_q, g_k, w_out):
    B, S, D = x.shape
    n_f = N_DENSE_HEADS
    cut0, cut1 = 3 * DENSE_W, 3 * DENSE_W + n_f
    w = jnp.concatenate([w_in[:, :cut0], w_in[:, cut1:]], axis=1).astype(BF16)
    wf = jnp.zeros((16, D), F32).at[:n_f].set(w_in[:, cut0:cut1].T).astype(BF16)
    bf = jnp.zeros((8, LANES), F32).at[:n_f].set(jnp.broadcast_to(b_f[:, None], (n_f, LANES)))
    gq = (g_q * DENSE_HEAD_DIM ** -0.5).reshape(1, DENSE_HEAD_DIM)
    gk = g_k.reshape(1, DENSE_HEAD_DIM)
    fq, fk, fv, sq, sk, sv, sgate, cum = _even_proj(x, g_norm.reshape(1, D), w, wf, bf, gq, gk)

    t = ATT_TILE
    cend = cum[:, :n_f, t - 1::t].reshape(B * n_f, S // t)
    qk_bound = math.sqrt(DENSE_HEAD_DIM) * jnp.max(jnp.abs(g_q)) * jnp.max(jnp.abs(g_k)) * 1.02
    thr = (EXP_ZERO + 1.0 + 2.0 * qk_bound).reshape(1).astype(F32)
    mixed_f = _fox_attn(cend, thr, fq, fk, fv, cum, sgate)

    idx = jnp.arange(t)
    upper = (idx[:, None] > idx[None, :]).astype(BF16)
    mixed_s = _sb_attn(sq, sk, sv, upper, sgate)
    return _even_out(x, mixed_f, mixed_s, w_out[:DENSE_W].astype(BF16), w_out[DENSE_W:].astype(BF16))


def _odd_layer(x, g_norm, w_in, g_q, g_k, w_out):
    B, S, D = x.shape
    w = w_in.astype(BF16)
    hid = jnp.arange(DIL_GROUP_W) // DIL_HEAD_DIM
    bd = ((hid[:, None] == hid[None, :]).astype(F32) / DIL_HEAD_DIM).astype(BF16)
    gq = jnp.tile(g_q * DIL_HEAD_DIM ** -0.5, N_DIL_HEADS).reshape(1, DIL_GROUP_W)
    gk = jnp.tile(g_k, N_DIL_HEADS).reshape(1, DIL_GROUP_W)
    outs = _odd_proj(x, g_norm.reshape(1, D), w, bd, gq, gk)
    n_all = len(DILATIONS) * N_DIL_HEADS
    slopes = jnp.asarray([2.0 ** (-8.0 * (i + 1) / n_all) for i in range(n_all)], F32)
    mixed = _dil_attn(slopes, outs[:9], outs[9])
    return _odd_out(x, mixed, w_out.astype(BF16))


def kernel(x, even_norm, even_w_in, even_b_f, even_q_gain, even_k_gain, even_w_out,
           odd_norm, odd_w_in, odd_q_gain, odd_k_gain, odd_w_out):
    assert x.shape[1] % DIL_POS == 0 and x.shape[2] == D_MODEL
    h = _even_layer(x, even_norm[0], even_w_in[0], even_b_f[0], even_q_gain[0], even_k_gain[0],
                    even_w_out[0])
    return _odd_layer(h, odd_norm[0], odd_w_in[0], odd_q_gain[0], odd_k_gain[0], odd_w_out[0])
```

```python
import functools
import math

import jax
import jax.numpy as jnp
from jax import lax
from jax.experimental import pallas as pl
from jax.experimental.pallas import tpu as pltpu

F32 = jnp.float32
BF16 = jnp.bfloat16

D_MODEL = 1024
DENSE_HEAD_DIM = 128
N_DENSE_HEADS = 4
DENSE_W = N_DENSE_HEADS * DENSE_HEAD_DIM
DIL_HEAD_DIM = 64
N_DIL_HEADS = 8
DIL_GROUP_W = N_DIL_HEADS * DIL_HEAD_DIM
DILATIONS = (1, 4, 16)
DIL_SPAN = 128
DIL_BLK = 128
RMS_EPS = 1e-6

LANES = 128
PROJ_ROWS = 512
ATT_ROWS = 512
ATT_KEYS = 256
DIL_POS = DILATIONS[-1] * DIL_BLK
DIL_UNROLL = 4
VMEM_LIMIT = 56 * 1024 * 1024

EXP_ZERO = 104.0

_NT = (((1,), (1,)), ((), ()))


def _cparams(sem):
    return pltpu.CompilerParams(dimension_semantics=sem, vmem_limit_bytes=VMEM_LIMIT)


def _log_sigmoid_pair(z):
    soft = jnp.log(1.0 + jnp.exp(-jnp.abs(z)))
    log_beta = jnp.minimum(z, 0.0) - soft
    return log_beta, log_beta - z


def _silu(x):
    return x * (1.0 / (1.0 + jnp.exp(-x)))


def _row_rms(x, g):
    ms = jnp.mean(x * x, axis=-1, keepdims=True)
    return x * lax.rsqrt(ms + RMS_EPS) * g


def _even_proj_kernel(x_ref, gn_ref, w_ref, wf_ref, bf_ref, gq_ref, gk_ref,
                      fq_ref, fk_ref, fv_ref, sq_ref, sk_ref, sv_ref, sg_ref, cum_ref, carry_sc):
    i = pl.program_id(1)
    rows = x_ref.shape[0]
    h = _row_rms(x_ref[...], gn_ref[...]).astype(BF16)

    def proj(c0, width):
        return jnp.dot(h, w_ref[:, c0:c0 + width], preferred_element_type=F32)

    def head_norm(acc, g_ref, out_ref):
        for hh in range(N_DENSE_HEADS):
            sl = slice(hh * DENSE_HEAD_DIM, (hh + 1) * DENSE_HEAD_DIM)
            out_ref[:, sl] = _row_rms(acc[:, sl], g_ref[...]).astype(BF16)

    head_norm(proj(0, DENSE_W), gq_ref, fq_ref)
    head_norm(proj(DENSE_W, DENSE_W), gk_ref, fk_ref)
    fv_ref[...] = proj(2 * DENSE_W, DENSE_W).astype(BF16)
    sq_ref[...] = (proj(3 * DENSE_W, DENSE_W) * (DENSE_HEAD_DIM ** -0.5)).astype(BF16)
    sk_ref[...] = proj(4 * DENSE_W, DENSE_W).astype(BF16)
    sv_ref[...] = proj(5 * DENSE_W, DENSE_W).astype(BF16)
    for c in range(2):
        sl = slice(c * DENSE_W, (c + 1) * DENSE_W)
        sg_ref[:, sl] = _silu(proj(6 * DENSE_W + c * DENSE_W, DENSE_W)).astype(BF16)

    fl = lax.dot_general(wf_ref[...], h, _NT, preferred_element_type=F32)[:8] + bf_ref[:, 0:1]
    lf, _ = _log_sigmoid_pair(fl)
    lane = lax.broadcasted_iota(jnp.int32, lf.shape, 1)
    shift = 1
    while shift < rows:
        lf = lf + jnp.where(lane >= shift, pltpu.roll(lf, shift, axis=1), 0.0)
        shift *= 2

    @pl.when(i == 0)
    def _():
        carry_sc[...] = jnp.zeros_like(carry_sc)

    cum = lf + carry_sc[:, 0:1]
    cum_ref[...] = cum
    carry_sc[...] = jnp.broadcast_to(cum[:, rows - 1:rows], carry_sc.shape)


def _even_proj(x, g_norm, w, wf, b_f, g_q, g_k):
    B, S, D = x.shape
    tm = PROJ_ROWS
    row = lambda width: pl.BlockSpec((None, tm, width), lambda b, i: (b, i, 0))
    const = lambda shape: pl.BlockSpec(shape, lambda b, i: (0,) * len(shape))
    act = lambda width: jax.ShapeDtypeStruct((B, S, width), BF16)
    return pl.pallas_call(
        _even_proj_kernel,
        grid=(B, S // tm),
        in_specs=[row(D), const((1, D)), const(w.shape), const(wf.shape), const(b_f.shape),
                  const(g_q.shape), const(g_k.shape)],
        out_specs=[row(DENSE_W)] * 6 + [row(2 * DENSE_W),
                                        pl.BlockSpec((None, 8, tm), lambda b, i: (b, 0, i))],
        out_shape=[act(DENSE_W)] * 6 + [act(2 * DENSE_W), jax.ShapeDtypeStruct((B, 8, S), F32)],
        scratch_shapes=[pltpu.VMEM((8, LANES), F32)],
        compiler_params=_cparams(("arbitrary", "arbitrary")),
        name="even_proj",
    )(x, g_norm, w, wf, b_f, g_q, g_k)


def _fox_kernel(cend_ref, thr_ref, q_ref, k_ref, v_ref, c_ref, sg_ref, o_ref, m_sc, l_sc, acc_sc, s_sc):
    b, h, i = pl.program_id(0), pl.program_id(1), pl.program_id(2)
    tq, tk = q_ref.shape[0], ATT_KEYS
    nsub = tq // tk
    row_id = b * N_DENSE_HEADS + h
    c_base = jnp.where(i > 0, cend_ref[row_id, jnp.maximum(i * nsub - 1, 0)], 0.0)

    m_sc[...] = jnp.full_like(m_sc, -jnp.inf)
    l_sc[...] = jnp.zeros_like(l_sc)
    acc_sc[...] = jnp.zeros_like(acc_sc)

    def logits(kt, rows):
        start = pl.multiple_of(kt * tk, tk)
        s = lax.dot_general(q_ref[rows, :], k_ref[pl.ds(start, tk), :], _NT, preferred_element_type=F32)
        return s + (c_base - c_ref[pl.ds(h, 1), pl.ds(start, tk)])

    def update(s, kt, rows):
        start = pl.multiple_of(kt * tk, tk)
        m_prev = m_sc[rows, :]
        m_new = jnp.maximum(m_prev, jnp.max(s, axis=-1, keepdims=True))
        alpha = jnp.exp(m_prev - m_new)
        p = jnp.exp(s - jnp.concatenate([m_new] * (tk // LANES), axis=1))
        l_sc[rows, :] = alpha * l_sc[rows, :] + jnp.sum(p, axis=-1, keepdims=True)
        pv = jnp.dot(p.astype(BF16), v_ref[pl.ds(start, tk), :], preferred_element_type=F32)
        acc_sc[rows, :] = alpha * acc_sc[rows, :] + pv
        m_sc[rows, :] = m_new

    for dd in range(nsub):
        rows = slice(dd * tk, tq)
        s = logits(i * nsub + dd, rows)
        r_io = lax.broadcasted_iota(jnp.int32, s.shape, 0)
        c_io = lax.broadcasted_iota(jnp.int32, s.shape, 1)
        update(jnp.where(c_io <= r_io, s, -jnp.inf), i * nsub + dd, rows)

    def wanted(kt):
        return jnp.logical_and(kt >= 0, c_base - cend_ref[row_id, jnp.maximum(kt, 0)] >= -thr_ref[0])

    everything = slice(0, tq)
    kt0 = i * nsub - 1
    s_sc[...] = logits(jnp.maximum(kt0, 0), everything)

    def body(c):
        kt, _ = c
        s = s_sc[...]
        s_sc[...] = logits(jnp.maximum(kt - 1, 0), everything)
        update(s, kt, everything)
        return kt - 1, wanted(kt - 1)

    lax.while_loop(lambda c: c[1], body, (kt0, wanted(kt0)))
    o_ref[...] = (acc_sc[...] * (1.0 / l_sc[...]) * sg_ref[...].astype(F32)).astype(BF16)


def _fox_attn(cend, thr, fq, fk, fv, cum, sgate):
    B, S, _ = fq.shape
    tq = ATT_ROWS
    tile = pl.BlockSpec((None, tq, DENSE_HEAD_DIM), lambda b, h, i: (b, i, h))
    full = pl.BlockSpec((None, S, DENSE_HEAD_DIM), lambda b, h, i: (b, 0, h))
    smem = pl.BlockSpec(memory_space=pltpu.SMEM)
    return pl.pallas_call(
        _fox_kernel,
        grid=(B, N_DENSE_HEADS, S // tq),
        in_specs=[smem, smem, tile, full, full,
                  pl.BlockSpec((None, 8, S), lambda b, h, i: (b, 0, 0)), tile],
        out_specs=tile,
        out_shape=jax.ShapeDtypeStruct((B, S, DENSE_W), BF16),
        scratch_shapes=[pltpu.VMEM((tq, LANES), F32), pltpu.VMEM((tq, LANES), F32),
                        pltpu.VMEM((tq, DENSE_HEAD_DIM), F32), pltpu.VMEM((tq, ATT_KEYS), F32)],
        compiler_params=_cparams(("arbitrary", "arbitrary", "arbitrary")),
        name="fox_attn",
    )(cend, thr, fq, fk, fv, cum, sgate)


def _sb_kernel(q_ref, k_ref, v_ref, u_ref, sg_ref, o_ref, off_sc, acc_sc, z_sc):
    i = pl.program_id(2)
    tq, tk = q_ref.shape[0], ATT_KEYS
    nsub = tq // tk
    off_sc[...] = jnp.zeros_like(off_sc)
    acc_sc[...] = jnp.zeros_like(acc_sc)

    def logits(kt, rows):
        start = pl.multiple_of(kt * tk, tk)
        return lax.dot_general(q_ref[rows, :], k_ref[pl.ds(start, tk), :], _NT, preferred_element_type=F32)

    def update(z, kt, rows, diagonal):
        start = pl.multiple_of(kt * tk, tk)
        log_beta, log_om = _log_sigmoid_pair(z)
        if diagonal:
            r_io = lax.broadcasted_iota(jnp.int32, z.shape, 0)
            c_io = lax.broadcasted_iota(jnp.int32, z.shape, 1)
            strict = c_io < r_io
            log_om = jnp.where(strict, log_om, 0.0)
        off = off_sc[rows, :]
        later = (jnp.dot(log_om.astype(BF16), u_ref[...], preferred_element_type=F32)
                 + jnp.concatenate([off] * (tk // LANES), axis=1))
        w = jnp.exp(log_beta + later)
        if diagonal:
            w = jnp.where(strict, w, 0.0)
        acc_sc[rows, :] += jnp.dot(w.astype(BF16), v_ref[pl.ds(start, tk), :], preferred_element_type=F32)
        off_sc[rows, :] = off + jnp.sum(log_om, axis=-1, keepdims=True)

    for dd in reversed(range(nsub)):
        rows = slice(dd * tk, tq)
        update(logits(i * nsub + dd, rows), i * nsub + dd, rows, True)

    everything = slice(0, tq)
    kt0 = i * nsub - 1
    z_sc[...] = logits(jnp.maximum(kt0, 0), everything)

    def body(c):
        kt, _ = c
        z = z_sc[...]
        z_sc[...] = logits(jnp.maximum(kt - 1, 0), everything)
        update(z, kt, everything, False)
        alive = jnp.max(off_sc[...]) >= -(EXP_ZERO + 8.0)
        return kt - 1, jnp.logical_and(kt >= 1, alive)

    lax.while_loop(lambda c: c[1], body, (kt0, i >= 1))
    o_ref[...] = (acc_sc[...] * sg_ref[...].astype(F32)).astype(BF16)


def _sb_attn(sq, sk, sv, upper, sgate):
    B, S, _ = sq.shape
    tq, tk = ATT_ROWS, ATT_KEYS
    tile = pl.BlockSpec((None, tq, DENSE_HEAD_DIM), lambda b, h, i: (b, i, h))
    gate_tile = pl.BlockSpec((None, tq, DENSE_HEAD_DIM), lambda b, h, i: (b, i, N_DENSE_HEADS + h))
    full = pl.BlockSpec((None, S, DENSE_HEAD_DIM), lambda b, h, i: (b, 0, h))
    return pl.pallas_call(
        _sb_kernel,
        grid=(B, N_DENSE_HEADS, S // tq),
        in_specs=[tile, full, full, pl.BlockSpec((tk, tk), lambda b, h, i: (0, 0)), gate_tile],
        out_specs=tile,
        out_shape=jax.ShapeDtypeStruct((B, S, DENSE_W), BF16),
        scratch_shapes=[pltpu.VMEM((tq, LANES), F32), pltpu.VMEM((tq, DENSE_HEAD_DIM), F32),
                        pltpu.VMEM((tq, ATT_KEYS), F32)],
        compiler_params=_cparams(("arbitrary", "arbitrary", "arbitrary")),
        name="sb_attn",
    )(sq, sk, sv, upper, sgate)


def _even_out_kernel(x_ref, mf_ref, ms_ref, wf_ref, ws_ref, o_ref):
    o_ref[...] = (x_ref[...]
                  + jnp.dot(mf_ref[...], wf_ref[...], preferred_element_type=F32)
                  + jnp.dot(ms_ref[...], ws_ref[...], preferred_element_type=F32))


def _even_out(x, mf, ms, wo_f, wo_s):
    B, S, D = x.shape
    tm = PROJ_ROWS
    row = lambda width: pl.BlockSpec((None, tm, width), lambda b, i: (b, i, 0))
    const = lambda shape: pl.BlockSpec(shape, lambda b, i: (0,) * len(shape))
    return pl.pallas_call(
        _even_out_kernel,
        grid=(B, S // tm),
        in_specs=[row(D), row(DENSE_W), row(DENSE_W), const(wo_f.shape), const(wo_s.shape)],
        out_specs=row(D),
        out_shape=jax.ShapeDtypeStruct((B, S, D), F32),
        compiler_params=_cparams(("arbitrary", "arbitrary")),
        name="even_out",
    )(x, mf, ms, wo_f, wo_s)


def _odd_out_kernel(x_ref, m_ref, w_ref, o_ref):
    o_ref[...] = x_ref[...] + jnp.dot(m_ref[...], w_ref[...], preferred_element_type=F32)


def _odd_out(x, mixed, wo):
    B, S, D = x.shape
    tm = PROJ_ROWS
    row = lambda width: pl.BlockSpec((None, tm, width), lambda b, i: (b, i, 0))
    return pl.pallas_call(
        _odd_out_kernel,
        grid=(B, S // tm),
        in_specs=[row(D), row(DIL_GROUP_W), pl.BlockSpec(wo.shape, lambda b, i: (0, 0))],
        out_specs=row(D),
        out_shape=jax.ShapeDtypeStruct((B, S, D), F32),
        compiler_params=_cparams(("arbitrary", "arbitrary")),
        name="odd_out",
    )(x, mixed, wo)


def _odd_proj_kernel(x_ref, gn_ref, w_ref, bd_ref, gq_ref, gk_ref, *rest):
    out_refs, sg_ref, tmp_sc = rest[:9], rest[9], rest[10]
    rows = x_ref.shape[0]
    h = _row_rms(x_ref[...], gn_ref[...]).astype(BF16)

    def head_norm(acc, g_ref):
        ms = jnp.dot((acc * acc).astype(BF16), bd_ref[...], preferred_element_type=F32)
        return acc * lax.rsqrt(ms + RMS_EPS) * g_ref[...]

    for c in range(9):
        kind, g = divmod(c, 3)
        acc = jnp.dot(h, w_ref[:, c * DIL_GROUP_W:(c + 1) * DIL_GROUP_W], preferred_element_type=F32)
        if kind == 0:
            acc = head_norm(acc, gq_ref)
        elif kind == 1:
            acc = head_norm(acc, gk_ref)
        dil = DILATIONS[g]
        if dil == 1:
            out_refs[c][0] = acc.astype(BF16)
        else:
            for cb in range(DIL_GROUP_W // LANES):
                tmp_sc[cb] = acc[:, cb * LANES:(cb + 1) * LANES]
            for r in range(dil):
                for cb in range(DIL_GROUP_W // LANES):
                    out_refs[c][r, :, cb * LANES:(cb + 1) * LANES] = (
                        tmp_sc[cb, pl.ds(r, rows // dil, stride=dil), :].astype(BF16))
    gate = jnp.dot(h, w_ref[:, 9 * DIL_GROUP_W:10 * DIL_GROUP_W], preferred_element_type=F32)
    sg_ref[...] = _silu(gate).astype(BF16)


def _odd_proj(x, g_norm, w, bd, g_q, g_k):
    B, S, D = x.shape
    tm = PROJ_ROWS
    const = lambda shape: pl.BlockSpec(shape, lambda b, i: (0,) * len(shape))
    out_specs, out_shape = [], []
    for _ in range(3):
        for dil in DILATIONS:
            out_specs.append(pl.BlockSpec((None, dil, tm // dil, DIL_GROUP_W), lambda b, i: (b, 0, i, 0)))
            out_shape.append(jax.ShapeDtypeStruct((B, dil, S // dil, DIL_GROUP_W), BF16))
    out_specs.append(pl.BlockSpec((None, tm, DIL_GROUP_W), lambda b, i: (b, i, 0)))
    out_shape.append(jax.ShapeDtypeStruct((B, S, DIL_GROUP_W), BF16))
    return pl.pallas_call(
        _odd_proj_kernel,
        grid=(B, S // tm),
        in_specs=[pl.BlockSpec((None, tm, D), lambda b, i: (b, i, 0)), const((1, D)), const(w.shape),
                  const(bd.shape), const(g_q.shape), const(g_k.shape)],
        out_specs=out_specs,
        out_shape=out_shape,
        scratch_shapes=[pltpu.VMEM((DIL_GROUP_W // LANES, tm, LANES), F32)],
        compiler_params=_cparams(("arbitrary", "arbitrary")),
        name="odd_proj",
    )(x, g_norm, w, bd, g_q, g_k)


def _dil_kernel(slope_ref, *refs):
    ins, (sg_ref, o_ref), scr = refs[:15], refs[15:17], refs[17:]
    kbufs, vbufs, (o_sc, m_sc, l_sc) = scr[0:3], scr[3:6], scr[6:9]
    hp, st = pl.program_id(1), pl.program_id(2)
    blk = DIL_BLK

    a_io = lax.broadcasted_iota(jnp.int32, (blk, 2 * blk), 0)
    c_io = lax.broadcasted_iota(jnp.int32, (blk, 2 * blk), 1)
    dist = a_io - c_io + blk
    band = jnp.logical_and(dist >= 0, dist <= DIL_SPAN)
    neg = jnp.where(band, 0.0, -jnp.inf)
    neg_first = jnp.where(jnp.logical_and(band, c_io >= blk), 0.0, -jnp.inf)
    dist_f = dist.astype(F32)
    lane_lo = lax.broadcasted_iota(jnp.int32, (blk, LANES), 1) < DIL_HEAD_DIM

    for g, dil in enumerate(DILATIONS):
        q_ref, kc_ref, kh_ref, vc_ref, vh_ref = ins[5 * g:5 * g + 5]
        kbuf, vbuf = kbufs[g], vbufs[g]
        length = DIL_POS // dil
        nb = length // blk
        kbuf[:, 0:blk, :] = kh_ref[...]
        kbuf[:, blk:, :] = kc_ref[...]
        vbuf[:, 0:blk, :] = vh_ref[...]
        vbuf[:, blk:, :] = vc_ref[...]
        bias = []
        for hd in range(2):
            slope = slope_ref[g * N_DIL_HEADS + 2 * hp + hd]
            alibi = dist_f * (slope * float(dil))
            bias.append((neg - alibi, neg_first - alibi))

        def block(idx, carry, g=g, dil=dil, nb=nb, q_ref=q_ref, kbuf=kbuf, vbuf=vbuf, bias=bias):
            r = idx // nb
            n = idx - r * nb
            row0 = pl.multiple_of(n * blk, blk)
            q2 = q_ref[r, pl.ds(row0, blk), :]
            k2 = kbuf[r, pl.ds(row0, 2 * blk), :]
            v2 = vbuf[r, pl.ds(row0, 2 * blk), :]
            first = jnp.logical_and(st == 0, n == 0)
            outs = []
            for hd in range(2):
                keep = lane_lo if hd == 0 else jnp.logical_not(lane_lo)
                qm = jnp.where(keep, q2, jnp.zeros_like(q2))
                s = lax.dot_general(qm, k2, _NT, preferred_element_type=F32)
                s = s + jnp.where(first, bias[hd][1], bias[hd][0])
                m = jnp.max(s, axis=-1, keepdims=True)
                p = jnp.exp(s - m)
                l = jnp.sum(p, axis=-1, keepdims=True)
                o = jnp.dot(p.astype(BF16), v2, preferred_element_type=F32)
                outs.append((o, m, l))
            o = jnp.where(lane_lo, outs[0][0], outs[1][0])
            m = jnp.where(lane_lo, outs[0][1], outs[1][1])
            l = jnp.where(lane_lo, outs[0][2], outs[1][2])
            start = n * (blk * dil) + r
            if dil == 1:
                dst = pl.ds(pl.multiple_of(start, blk), blk)
            else:
                dst = pl.ds(start, blk, stride=dil)
            o_sc[g, dst, :] = o
            m_sc[g, dst, :] = m
            l_sc[g, dst, :] = l
            return carry

        lax.fori_loop(0, dil * nb, block, 0, unroll=DIL_UNROLL)

    chunk = 256

    def merge(ci, carry):
        rows = pl.ds(pl.multiple_of(ci * chunk, chunk), chunk)
        ms = [m_sc[g, rows, :] for g in range(3)]
        m_all = jnp.maximum(jnp.maximum(ms[0], ms[1]), ms[2])
        num = jnp.zeros((chunk, LANES), F32)
        den = jnp.zeros((chunk, LANES), F32)
        for g in range(3):
            e = jnp.exp(ms[g] - m_all)
            num = num + e * o_sc[g, rows, :]
            den = den + e * l_sc[g, rows, :]
        o_ref[rows, :] = (num / den * sg_ref[rows, :].astype(F32)).astype(BF16)
        return carry

    lax.fori_loop(0, DIL_POS // chunk, merge, 0)


def _dil_attn(slopes, qkv, sgate):
    B, S, _ = sgate.shape
    n_pairs = N_DIL_HEADS * DIL_HEAD_DIM // LANES
    blk = DIL_BLK
    in_specs = [pl.BlockSpec(memory_space=pltpu.SMEM)]
    args = [slopes]
    scratch_k, scratch_v = [], []
    for g, dil in enumerate(DILATIONS):
        length = DIL_POS // dil
        nb = length // blk
        cur = pl.BlockSpec((None, dil, length, LANES), lambda b, hp, st: (b, 0, st, hp))
        halo = pl.BlockSpec((None, dil, blk, LANES),
                            lambda b, hp, st, nb=nb: (b, 0, jnp.maximum(st * nb - 1, 0), hp))
        q, k, v = qkv[g], qkv[3 + g], qkv[6 + g]
        in_specs += [cur, cur, halo, cur, halo]
        args += [q, k, k, v, v]
        scratch_k.append(pltpu.VMEM((dil, length + blk, LANES), BF16))
        scratch_v.append(pltpu.VMEM((dil, length + blk, LANES), BF16))
    pos = pl.BlockSpec((None, DIL_POS, LANES), lambda b, hp, st: (b, st, hp))
    in_specs.append(pos)
    args.append(sgate)
    return pl.pallas_call(
        _dil_kernel,
        grid=(B, n_pairs, S // DIL_POS),
        in_specs=in_specs,
        out_specs=pos,
        out_shape=jax.ShapeDtypeStruct((B, S, DIL_GROUP_W), BF16),
        scratch_shapes=scratch_k + scratch_v + [pltpu.VMEM((3, DIL_POS, LANES), F32)] * 3,
        compiler_params=_cparams(("arbitrary", "arbitrary", "arbitrary")),
        name="dil_attn",
    )(*args)


def _even_layer(x, g_norm, w_in, b_f, g_q, g_k, w_out):
    B, S, D = x.shape
    n_f = N_DENSE_HEADS
    cut0, cut1 = 3 * DENSE_W, 3 * DENSE_W + n_f
    w = jnp.concatenate([w_in[:, :cut0], w_in[:, cut1:]], axis=1).astype(BF16)
    wf = jnp.zeros((16, D), F32).at[:n_f].set(w_in[:, cut0:cut1].T).astype(BF16)
    bf = jnp.zeros((8, LANES), F32).at[:n_f].set(jnp.broadcast_to(b_f[:, None], (n_f, LANES)))
    gq = (g_q * DENSE_HEAD_DIM ** -0.5).reshape(1, DENSE_HEAD_DIM)
    gk = g_k.reshape(1, DENSE_HEAD_DIM)
    fq, fk, fv, sq, sk, sv, sgate, cum = _even_proj(x, g_norm.reshape(1, D), w, wf, bf, gq, gk)

    t = ATT_KEYS
    cend = cum[:, :n_f, t - 1::t].reshape(B * n_f, S // t)
    qk_bound = math.sqrt(DENSE_HEAD_DIM) * jnp.max(jnp.abs(g_q)) * jnp.max(jnp.abs(g_k)) * 1.02
    thr = (EXP_ZERO + 1.0 + 2.0 * qk_bound).reshape(1).astype(F32)
    mixed_f = _fox_attn(cend, thr, fq, fk, fv, cum, sgate)

    idx = jnp.arange(t)
    upper = (idx[:, None] > idx[None, :]).astype(BF16)
    mixed_s = _sb_attn(sq, sk, sv, upper, sgate)
    return _even_out(x, mixed_f, mixed_s, w_out[:DENSE_W].astype(BF16), w_out[DENSE_W:].astype(BF16))


def _odd_layer(x, g_norm, w_in, g_q, g_k, w_out):
    B, S, D = x.shape
    w = w_in.astype(BF16)
    hid = jnp.arange(DIL_GROUP_W) // DIL_HEAD_DIM
    bd = ((hid[:, None] == hid[None, :]).astype(F32) / DIL_HEAD_DIM).astype(BF16)
    gq = jnp.tile(g_q * DIL_HEAD_DIM ** -0.5, N_DIL_HEADS).reshape(1, DIL_GROUP_W)
    gk = jnp.tile(g_k, N_DIL_HEADS).reshape(1, DIL_GROUP_W)
    outs = _odd_proj(x, g_norm.reshape(1, D), w, bd, gq, gk)
    n_all = len(DILATIONS) * N_DIL_HEADS
    slopes = jnp.asarray([2.0 ** (-8.0 * (i + 1) / n_all) for i in range(n_all)], F32)
    mixed = _dil_attn(slopes, outs[:9], outs[9])
    return _odd_out(x, mixed, w_out.astype(BF16))


def kernel(x, even_norm, even_w_in, even_b_f, even_q_gain, even_k_gain, even_w_out,
           odd_norm, odd_w_in, odd_q_gain, odd_k_gain, odd_w_out):
    assert x.shape[1] % DIL_POS == 0 and x.shape[2] == D_MODEL
    h = _even_layer(x, even_norm[0], even_w_in[0], even_b_f[0], even_q_gain[0], even_k_gain[0],
                    even_w_out[0])
    return _odd_layer(h, odd_norm[0], odd_w_in[0], odd_q_gain[0], odd_k_gain[0], odd_w_out[0])
```

```python
import functools
import math

import jax
import jax.numpy as jnp
from jax import lax
from jax.experimental import pallas as pl
from jax.experimental.pallas import tpu as pltpu

F32 = jnp.float32
BF16 = jnp.bfloat16

D_MODEL = 1024
DENSE_HEAD_DIM = 128
N_DENSE_HEADS = 4
DENSE_W = N_DENSE_HEADS * DENSE_HEAD_DIM
DIL_HEAD_DIM = 64
N_DIL_HEADS = 8
DIL_GROUP_W = N_DIL_HEADS * DIL_HEAD_DIM
DILATIONS = (1, 4, 16)
DIL_SPAN = 128
DIL_BLK = 128
RMS_EPS = 1e-6

LANES = 128
PROJ_ROWS = 512
ATT_ROWS = 512
ATT_KEYS = 256
DIL_POS = DILATIONS[-1] * DIL_BLK
DIL_UNROLL = 16
VMEM_LIMIT = 56 * 1024 * 1024

EXP_ZERO = 104.0

_NT = (((1,), (1,)), ((), ()))


def _cparams(sem):
    return pltpu.CompilerParams(dimension_semantics=sem, vmem_limit_bytes=VMEM_LIMIT)


def _log_sigmoid_pair(z):
    soft = jnp.log(1.0 + jnp.exp(-jnp.abs(z)))
    log_beta = jnp.minimum(z, 0.0) - soft
    return log_beta, log_beta - z


def _silu(x):
    return x * (1.0 / (1.0 + jnp.exp(-x)))


def _row_rms(x, g):
    ms = jnp.mean(x * x, axis=-1, keepdims=True)
    return x * lax.rsqrt(ms + RMS_EPS) * g


def _even_proj_kernel(x_ref, gn_ref, w_ref, wf_ref, bf_ref, gq_ref, gk_ref,
                      fq_ref, fk_ref, fv_ref, sq_ref, sk_ref, sv_ref, sg_ref, cum_ref, carry_sc):
    i = pl.program_id(1)
    rows = x_ref.shape[0]
    h = _row_rms(x_ref[...], gn_ref[...]).astype(BF16)

    def proj(c0, width):
        return jnp.dot(h, w_ref[:, c0:c0 + width], preferred_element_type=F32)

    def head_norm(acc, g_ref, out_ref):
        for hh in range(N_DENSE_HEADS):
            sl = slice(hh * DENSE_HEAD_DIM, (hh + 1) * DENSE_HEAD_DIM)
            out_ref[:, sl] = _row_rms(acc[:, sl], g_ref[...]).astype(BF16)

    head_norm(proj(0, DENSE_W), gq_ref, fq_ref)
    head_norm(proj(DENSE_W, DENSE_W), gk_ref, fk_ref)
    fv_ref[...] = proj(2 * DENSE_W, DENSE_W).astype(BF16)
    sq_ref[...] = (proj(3 * DENSE_W, DENSE_W) * (DENSE_HEAD_DIM ** -0.5)).astype(BF16)
    sk_ref[...] = proj(4 * DENSE_W, DENSE_W).astype(BF16)
    sv_ref[...] = proj(5 * DENSE_W, DENSE_W).astype(BF16)
    for c in range(2):
        sl = slice(c * DENSE_W, (c + 1) * DENSE_W)
        sg_ref[:, sl] = _silu(proj(6 * DENSE_W + c * DENSE_W, DENSE_W)).astype(BF16)

    fl = lax.dot_general(wf_ref[...], h, _NT, preferred_element_type=F32)[:8] + bf_ref[:, 0:1]
    lf, _ = _log_sigmoid_pair(fl)
    lane = lax.broadcasted_iota(jnp.int32, lf.shape, 1)
    shift = 1
    while shift < rows:
        lf = lf + jnp.where(lane >= shift, pltpu.roll(lf, shift, axis=1), 0.0)
        shift *= 2

    @pl.when(i == 0)
    def _():
        carry_sc[...] = jnp.zeros_like(carry_sc)

    cum = lf + carry_sc[:, 0:1]
    cum_ref[...] = cum
    carry_sc[...] = jnp.broadcast_to(cum[:, rows - 1:rows], carry_sc.shape)


def _even_proj(x, g_norm, w, wf, b_f, g_q, g_k):
    B, S, D = x.shape
    tm = PROJ_ROWS
    row = lambda width: pl.BlockSpec((None, tm, width), lambda b, i: (b, i, 0))
    const = lambda shape: pl.BlockSpec(shape, lambda b, i: (0,) * len(shape))
    act = lambda width: jax.ShapeDtypeStruct((B, S, width), BF16)
    return pl.pallas_call(
        _even_proj_kernel,
        grid=(B, S // tm),
        in_specs=[row(D), const((1, D)), const(w.shape), const(wf.shape), const(b_f.shape),
                  const(g_q.shape), const(g_k.shape)],
        out_specs=[row(DENSE_W)] * 6 + [row(2 * DENSE_W),
                                        pl.BlockSpec((None, 8, tm), lambda b, i: (b, 0, i))],
        out_shape=[act(DENSE_W)] * 6 + [act(2 * DENSE_W), jax.ShapeDtypeStruct((B, 8, S), F32)],
        scratch_shapes=[pltpu.VMEM((8, LANES), F32)],
        compiler_params=_cparams(("arbitrary", "arbitrary")),
        name="even_proj",
    )(x, g_norm, w, wf, b_f, g_q, g_k)


def _fox_kernel(cend_ref, thr_ref, q_ref, k_ref, v_ref, c_ref, sg_ref, o_ref, m_sc, acc_sc, s_sc):
    b, h, i = pl.program_id(0), pl.program_id(1), pl.program_id(2)
    tq, tk = q_ref.shape[0], ATT_KEYS
    nsub = tq // tk
    row_id = b * N_DENSE_HEADS + h
    c_base = jnp.where(i > 0, cend_ref[row_id, jnp.maximum(i * nsub - 1, 0)], 0.0)

    m_sc[...] = jnp.full_like(m_sc, -jnp.inf)
    acc_sc[...] = jnp.zeros_like(acc_sc)

    def logits(kt, rows):
        start = pl.multiple_of(kt * tk, tk)
        s = lax.dot_general(q_ref[rows, :], k_ref[pl.ds(start, tk), :], _NT, preferred_element_type=F32)
        return s + (c_base - c_ref[pl.ds(h, 1), pl.ds(start, tk)])

    def update(s, kt, rows):
        start = pl.multiple_of(kt * tk, tk)
        m_prev = m_sc[rows, :]
        m_new = jnp.maximum(m_prev, jnp.max(s, axis=-1, keepdims=True))
        alpha = jnp.exp(m_prev - m_new)
        p = jnp.exp(s - jnp.concatenate([m_new] * (tk // LANES), axis=1))
        v_ones = jnp.concatenate([v_ref[pl.ds(start, tk), :], jnp.ones((tk, LANES), BF16)], axis=1)
        pv = jnp.dot(p.astype(BF16), v_ones, preferred_element_type=F32)
        acc_sc[rows, :] = jnp.concatenate([alpha, alpha], axis=1) * acc_sc[rows, :] + pv
        m_sc[rows, :] = m_new

    for dd in range(nsub):
        rows = slice(dd * tk, tq)
        s = logits(i * nsub + dd, rows)
        r_io = lax.broadcasted_iota(jnp.int32, s.shape, 0)
        c_io = lax.broadcasted_iota(jnp.int32, s.shape, 1)
        update(jnp.where(c_io <= r_io, s, -jnp.inf), i * nsub + dd, rows)

    def wanted(kt):
        return jnp.logical_and(kt >= 0, c_base - cend_ref[row_id, jnp.maximum(kt, 0)] >= -thr_ref[0])

    everything = slice(0, tq)
    kt0 = i * nsub - 1
    s_sc[...] = logits(jnp.maximum(kt0, 0), everything)

    def body(c):
        kt, _ = c
        s = s_sc[...]
        s_sc[...] = logits(jnp.maximum(kt - 1, 0), everything)
        update(s, kt, everything)
        return kt - 1, wanted(kt - 1)

    lax.while_loop(lambda c: c[1], body, (kt0, wanted(kt0)))
    o_ref[...] = (acc_sc[:, :LANES] * (1.0 / acc_sc[:, LANES:]) * sg_ref[...].astype(F32)).astype(BF16)


def _fox_attn(cend, thr, fq, fk, fv, cum, sgate):
    B, S, _ = fq.shape
    tq = ATT_ROWS
    tile = pl.BlockSpec((None, tq, DENSE_HEAD_DIM), lambda b, h, i: (b, i, h))
    full = pl.BlockSpec((None, S, DENSE_HEAD_DIM), lambda b, h, i: (b, 0, h))
    smem = pl.BlockSpec(memory_space=pltpu.SMEM)
    return pl.pallas_call(
        _fox_kernel,
        grid=(B, N_DENSE_HEADS, S // tq),
        in_specs=[smem, smem, tile, full, full,
                  pl.BlockSpec((None, 8, S), lambda b, h, i: (b, 0, 0)), tile],
        out_specs=tile,
        out_shape=jax.ShapeDtypeStruct((B, S, DENSE_W), BF16),
        scratch_shapes=[pltpu.VMEM((tq, LANES), F32), pltpu.VMEM((tq, 2 * LANES), F32),
                        pltpu.VMEM((tq, ATT_KEYS), F32)],
        compiler_params=_cparams(("arbitrary", "arbitrary", "arbitrary")),
        name="fox_attn",
    )(cend, thr, fq, fk, fv, cum, sgate)


def _sb_kernel(q_ref, k_ref, v_ref, u_ref, sg_ref, o_ref, off_sc, acc_sc, z_sc):
    i = pl.program_id(2)
    tq, tk = q_ref.shape[0], ATT_KEYS
    nsub = tq // tk
    off_sc[...] = jnp.zeros_like(off_sc)
    acc_sc[...] = jnp.zeros_like(acc_sc)

    def logits(kt, rows):
        start = pl.multiple_of(kt * tk, tk)
        return lax.dot_general(q_ref[rows, :], k_ref[pl.ds(start, tk), :], _NT, preferred_element_type=F32)

    def update(z, kt, rows, diagonal):
        start = pl.multiple_of(kt * tk, tk)
        log_beta, log_om = _log_sigmoid_pair(z)
        if diagonal:
            r_io = lax.broadcasted_iota(jnp.int32, z.shape, 0)
            c_io = lax.broadcasted_iota(jnp.int32, z.shape, 1)
            strict = c_io < r_io
            log_om = jnp.where(strict, log_om, 0.0)
        off = off_sc[rows, :]
        later = (jnp.dot(log_om.astype(BF16), u_ref[...], preferred_element_type=F32)
                 + jnp.concatenate([off] * (tk // LANES), axis=1))
        w = jnp.exp(log_beta + later)
        if diagonal:
            w = jnp.where(strict, w, 0.0)
        acc_sc[rows, :] += jnp.dot(w.astype(BF16), v_ref[pl.ds(start, tk), :], preferred_element_type=F32)
        off_sc[rows, :] = off + jnp.sum(log_om, axis=-1, keepdims=True)

    for dd in reversed(range(nsub)):
        rows = slice(dd * tk, tq)
        update(logits(i * nsub + dd, rows), i * nsub + dd, rows, True)

    everything = slice(0, tq)
    kt0 = i * nsub - 1
    z_sc[...] = logits(jnp.maximum(kt0, 0), everything)

    def body(c):
        kt, _ = c
        z = z_sc[...]
        z_sc[...] = logits(jnp.maximum(kt - 1, 0), everything)
        update(z, kt, everything, False)
        alive = jnp.max(off_sc[...]) >= -(EXP_ZERO + 8.0)
        return kt - 1, jnp.logical_and(kt >= 1, alive)

    lax.while_loop(lambda c: c[1], body, (kt0, i >= 1))
    o_ref[...] = (acc_sc[...] * sg_ref[...].astype(F32)).astype(BF16)


def _sb_attn(sq, sk, sv, upper, sgate):
    B, S, _ = sq.shape
    tq, tk = ATT_ROWS, ATT_KEYS
    tile = pl.BlockSpec((None, tq, DENSE_HEAD_DIM), lambda b, h, i: (b, i, h))
    gate_tile = pl.BlockSpec((None, tq, DENSE_HEAD_DIM), lambda b, h, i: (b, i, N_DENSE_HEADS + h))
    full = pl.BlockSpec((None, S, DENSE_HEAD_DIM), lambda b, h, i: (b, 0, h))
    return pl.pallas_call(
        _sb_kernel,
        grid=(B, N_DENSE_HEADS, S // tq),
        in_specs=[tile, full, full, pl.BlockSpec((tk, tk), lambda b, h, i: (0, 0)), gate_tile],
        out_specs=tile,
        out_shape=jax.ShapeDtypeStruct((B, S, DENSE_W), BF16),
        scratch_shapes=[pltpu.VMEM((tq, LANES), F32), pltpu.VMEM((tq, DENSE_HEAD_DIM), F32),
                        pltpu.VMEM((tq, ATT_KEYS), F32)],
        compiler_params=_cparams(("arbitrary", "arbitrary", "arbitrary")),
        name="sb_attn",
    )(sq, sk, sv, upper, sgate)


def _even_out_kernel(x_ref, mf_ref, ms_ref, wf_ref, ws_ref, o_ref):
    o_ref[...] = (x_ref[...]
                  + jnp.dot(mf_ref[...], wf_ref[...], preferred_element_type=F32)
                  + jnp.dot(ms_ref[...], ws_ref[...], preferred_element_type=F32))


def _even_out(x, mf, ms, wo_f, wo_s):
    B, S, D = x.shape
    tm = PROJ_ROWS
    row = lambda width: pl.BlockSpec((None, tm, width), lambda b, i: (b, i, 0))
    const = lambda shape: pl.BlockSpec(shape, lambda b, i: (0,) * len(shape))
    return pl.pallas_call(
        _even_out_kernel,
        grid=(B, S // tm),
        in_specs=[row(D), row(DENSE_W), row(DENSE_W), const(wo_f.shape), const(wo_s.shape)],
        out_specs=row(D),
        out_shape=jax.ShapeDtypeStruct((B, S, D), F32),
        compiler_params=_cparams(("arbitrary", "arbitrary")),
        name="even_out",
    )(x, mf, ms, wo_f, wo_s)


def _odd_out_kernel(x_ref, m_ref, w_ref, o_ref):
    o_ref[...] = x_ref[...] + jnp.dot(m_ref[...], w_ref[...], preferred_element_type=F32)


def _odd_out(x, mixed, wo):
    B, S, D = x.shape
    tm = PROJ_ROWS
    row = lambda width: pl.BlockSpec((None, tm, width), lambda b, i: (b, i, 0))
    return pl.pallas_call(
        _odd_out_kernel,
        grid=(B, S // tm),
        in_specs=[row(D), row(DIL_GROUP_W), pl.BlockSpec(wo.shape, lambda b, i: (0, 0))],
        out_specs=row(D),
        out_shape=jax.ShapeDtypeStruct((B, S, D), F32),
        compiler_params=_cparams(("arbitrary", "arbitrary")),
        name="odd_out",
    )(x, mixed, wo)


def _odd_proj_kernel(x_ref, gn_ref, w_ref, gq_ref, gk_ref, *rest):
    out_refs, sg_ref, tmp_sc = rest[:9], rest[9], rest[10]
    rows = x_ref.shape[0]
    h = _row_rms(x_ref[...], gn_ref[...]).astype(BF16)

    lane_lo = lax.broadcasted_iota(jnp.int32, (rows, LANES), 1) < DIL_HEAD_DIM

    def head_norm(acc, g_ref):
        cols = []
        for cb in range(DIL_GROUP_W // LANES):
            x = acc[:, cb * LANES:(cb + 1) * LANES]
            x2 = x * x
            lo = jnp.sum(jnp.where(lane_lo, x2, 0.0), axis=-1, keepdims=True)
            hi = jnp.sum(jnp.where(lane_lo, 0.0, x2), axis=-1, keepdims=True)
            ms = jnp.where(lane_lo, lo, hi) * (1.0 / DIL_HEAD_DIM)
            cols.append(x * lax.rsqrt(ms + RMS_EPS))
        return jnp.concatenate(cols, axis=1) * g_ref[...]

    for c in range(9):
        kind, g = divmod(c, 3)
        acc = jnp.dot(h, w_ref[:, c * DIL_GROUP_W:(c + 1) * DIL_GROUP_W], preferred_element_type=F32)
        if kind == 0:
            acc = head_norm(acc, gq_ref)
        elif kind == 1:
            acc = head_norm(acc, gk_ref)
        dil = DILATIONS[g]
        if dil == 1:
            out_refs[c][0] = acc.astype(BF16)
        else:
            for cb in range(DIL_GROUP_W // LANES):
                tmp_sc[cb] = acc[:, cb * LANES:(cb + 1) * LANES]
            for r in range(dil):
                for cb in range(DIL_GROUP_W // LANES):
                    out_refs[c][r, :, cb * LANES:(cb + 1) * LANES] = (
                        tmp_sc[cb, pl.ds(r, rows // dil, stride=dil), :].astype(BF16))
    gate = jnp.dot(h, w_ref[:, 9 * DIL_GROUP_W:10 * DIL_GROUP_W], preferred_element_type=F32)
    sg_ref[...] = _silu(gate).astype(BF16)


def _odd_proj(x, g_norm, w, g_q, g_k):
    B, S, D = x.shape
    tm = PROJ_ROWS
    const = lambda shape: pl.BlockSpec(shape, lambda b, i: (0,) * len(shape))
    out_specs, out_shape = [], []
    for _ in range(3):
        for dil in DILATIONS:
            out_specs.append(pl.BlockSpec((None, dil, tm // dil, DIL_GROUP_W), lambda b, i: (b, 0, i, 0)))
            out_shape.append(jax.ShapeDtypeStruct((B, dil, S // dil, DIL_GROUP_W), BF16))
    out_specs.append(pl.BlockSpec((None, tm, DIL_GROUP_W), lambda b, i: (b, i, 0)))
    out_shape.append(jax.ShapeDtypeStruct((B, S, DIL_GROUP_W), BF16))
    return pl.pallas_call(
        _odd_proj_kernel,
        grid=(B, S // tm),
        in_specs=[pl.BlockSpec((None, tm, D), lambda b, i: (b, i, 0)), const((1, D)), const(w.shape),
                  const(g_q.shape), const(g_k.shape)],
        out_specs=out_specs,
        out_shape=out_shape,
        scratch_shapes=[pltpu.VMEM((DIL_GROUP_W // LANES, tm, LANES), F32)],
        compiler_params=_cparams(("arbitrary", "arbitrary")),
        name="odd_proj",
    )(x, g_norm, w, g_q, g_k)


def _dil_kernel(slope_ref, *refs):
    ins, (sg_ref, o_ref), scr = refs[:15], refs[15:17], refs[17:]
    kbufs, vbufs, (o_sc, m_sc, l_sc) = scr[0:3], scr[3:6], scr[6:9]
    hp, st = pl.program_id(1), pl.program_id(2)
    blk = DIL_BLK

    a_io = lax.broadcasted_iota(jnp.int32, (blk, 2 * blk), 0)
    c_io = lax.broadcasted_iota(jnp.int32, (blk, 2 * blk), 1)
    dist = a_io - c_io + blk
    band = jnp.logical_and(dist >= 0, dist <= DIL_SPAN)
    neg = jnp.where(band, 0.0, -jnp.inf)
    neg_first = jnp.where(jnp.logical_and(band, c_io >= blk), 0.0, -jnp.inf)
    dist_f = dist.astype(F32)
    lane_lo = lax.broadcasted_iota(jnp.int32, (blk, LANES), 1) < DIL_HEAD_DIM

    for g, dil in enumerate(DILATIONS):
        q_ref, kc_ref, kh_ref, vc_ref, vh_ref = ins[5 * g:5 * g + 5]
        kbuf, vbuf = kbufs[g], vbufs[g]
        length = DIL_POS // dil
        nb = length // blk
        kbuf[:, 0:blk, :] = kh_ref[...]
        kbuf[:, blk:, :] = kc_ref[...]
        vbuf[:, 0:blk, 0:LANES] = vh_ref[...]
        vbuf[:, blk:, 0:LANES] = vc_ref[...]
        vbuf[:, :, LANES:] = jnp.ones((dil, length + blk, LANES), BF16)
        bias = []
        for hd in range(2):
            slope = slope_ref[g * N_DIL_HEADS + 2 * hp + hd]
            alibi = dist_f * (slope * float(dil))
            bias.append((neg - alibi, neg_first - alibi))

        def block(idx, carry, g=g, dil=dil, nb=nb, q_ref=q_ref, kbuf=kbuf, vbuf=vbuf, bias=bias):
            r = idx // nb
            n = idx - r * nb
            row0 = pl.multiple_of(n * blk, blk)
            q2 = q_ref[r, pl.ds(row0, blk), :]
            k2 = kbuf[r, pl.ds(row0, 2 * blk), :]
            v2 = vbuf[r, pl.ds(row0, 2 * blk), :]
            first = jnp.logical_and(st == 0, n == 0)
            outs = []
            for hd in range(2):
                keep = lane_lo if hd == 0 else jnp.logical_not(lane_lo)
                qm = jnp.where(keep, q2, jnp.zeros_like(q2))
                s = lax.dot_general(qm, k2, _NT, preferred_element_type=F32)
                s = s + jnp.where(first, bias[hd][1], bias[hd][0])
                m = jnp.max(s, axis=-1, keepdims=True)
                p = jnp.exp(s - m)
                ol = jnp.dot(p.astype(BF16), v2, preferred_element_type=F32)
                outs.append((ol[:, :LANES], m, ol[:, LANES:]))
            o = jnp.where(lane_lo, outs[0][0], outs[1][0])
            m = jnp.where(lane_lo, outs[0][1], outs[1][1])
            l = jnp.where(lane_lo, outs[0][2], outs[1][2])
            start = n * (blk * dil) + r
            if dil == 1:
                dst = pl.ds(pl.multiple_of(start, blk), blk)
            else:
                dst = pl.ds(start, blk, stride=dil)
            o_sc[g, dst, :] = o
            m_sc[g, dst, :] = m
            l_sc[g, dst, :] = l
            return carry

        lax.fori_loop(0, dil * nb, block, 0, unroll=DIL_UNROLL)

    chunk = 256

    def merge(ci, carry):
        rows = pl.ds(pl.multiple_of(ci * chunk, chunk), chunk)
        ms = [m_sc[g, rows, :] for g in range(3)]
        m_all = jnp.maximum(jnp.maximum(ms[0], ms[1]), ms[2])
        num = jnp.zeros((chunk, LANES), F32)
        den = jnp.zeros((chunk, LANES), F32)
        for g in range(3):
            e = jnp.exp(ms[g] - m_all)
            num = num + e * o_sc[g, rows, :]
            den = den + e * l_sc[g, rows, :]
        o_ref[rows, :] = (num / den * sg_ref[rows, :].astype(F32)).astype(BF16)
        return carry

    lax.fori_loop(0, DIL_POS // chunk, merge, 0)


def _dil_attn(slopes, qkv, sgate):
    B, S, _ = sgate.shape
    n_pairs = N_DIL_HEADS * DIL_HEAD_DIM // LANES
    blk = DIL_BLK
    in_specs = [pl.BlockSpec(memory_space=pltpu.SMEM)]
    args = [slopes]
    scratch_k, scratch_v = [], []
    for g, dil in enumerate(DILATIONS):
        length = DIL_POS // dil
        nb = length // blk
        cur = pl.BlockSpec((None, dil, length, LANES), lambda b, hp, st: (b, 0, st, hp))
        halo = pl.BlockSpec((None, dil, blk, LANES),
                            lambda b, hp, st, nb=nb: (b, 0, jnp.maximum(st * nb - 1, 0), hp))
        q, k, v = qkv[g], qkv[3 + g], qkv[6 + g]
        in_specs += [cur, cur, halo, cur, halo]
        args += [q, k, k, v, v]
        scratch_k.append(pltpu.VMEM((dil, length + blk, LANES), BF16))
        scratch_v.append(pltpu.VMEM((dil, length + blk, 2 * LANES), BF16))
    pos = pl.BlockSpec((None, DIL_POS, LANES), lambda b, hp, st: (b, st, hp))
    in_specs.append(pos)
    args.append(sgate)
    return pl.pallas_call(
        _dil_kernel,
        grid=(B, n_pairs, S // DIL_POS),
        in_specs=in_specs,
        out_specs=pos,
        out_shape=jax.ShapeDtypeStruct((B, S, DIL_GROUP_W), BF16),
        scratch_shapes=scratch_k + scratch_v + [pltpu.VMEM((3, DIL_POS, LANES), F32)] * 3,
        compiler_params=_cparams(("arbitrary", "arbitrary", "arbitrary")),
        name="dil_attn",
    )(*args)


def _even_layer(x, g_norm, w_in, b_f, g_q, g_k, w_out):
    B, S, D = x.shape
    n_f = N_DENSE_HEADS
    cut0, cut1 = 3 * DENSE_W, 3 * DENSE_W + n_f
    w = jnp.concatenate([w_in[:, :cut0], w_in[:, cut1:]], axis=1).astype(BF16)
    wf = jnp.zeros((16, D), F32).at[:n_f].set(w_in[:, cut0:cut1].T).astype(BF16)
    bf = jnp.zeros((8, LANES), F32).at[:n_f].set(jnp.broadcast_to(b_f[:, None], (n_f, LANES)))
    gq = (g_q * DENSE_HEAD_DIM ** -0.5).reshape(1, DENSE_HEAD_DIM)
    gk = g_k.reshape(1, DENSE_HEAD_DIM)
    fq, fk, fv, sq, sk, sv, sgate, cum = _even_proj(x, g_norm.reshape(1, D), w, wf, bf, gq, gk)

    t = ATT_KEYS
    cend = cum[:, :n_f, t - 1::t].reshape(B * n_f, S // t)
    qk_bound = math.sqrt(DENSE_HEAD_DIM) * jnp.max(jnp.abs(g_q)) * jnp.max(jnp.abs(g_k)) * 1.02
    thr = (EXP_ZERO + 1.0 + 2.0 * qk_bound).reshape(1).astype(F32)
    mixed_f = _fox_attn(cend, thr, fq, fk, fv, cum, sgate)

    idx = jnp.arange(t)
    upper = (idx[:, None] > idx[None, :]).astype(BF16)
    mixed_s = _sb_attn(sq, sk, sv, upper, sgate)
    return _even_out(x, mixed_f, mixed_s, w_out[:DENSE_W].astype(BF16), w_out[DENSE_W:].astype(BF16))


def _odd_layer(x, g_norm, w_in, g_q, g_k, w_out):
    B, S, D = x.shape
    w = w_in.astype(BF16)
    gq = jnp.tile(g_q * DIL_HEAD_DIM ** -0.5, N_DIL_HEADS).reshape(1, DIL_GROUP_W)
    gk = jnp.tile(g_k, N_DIL_HEADS).reshape(1, DIL_GROUP_W)
    outs = _odd_proj(x, g_norm.reshape(1, D), w, gq, gk)
    n_all = len(DILATIONS) * N_DIL_HEADS
    slopes = jnp.asarray([2.0 ** (-8.0 * (i + 1) / n_all) for i in range(n_all)], F32)
    mixed = _dil_attn(slopes, outs[:9], outs[9])
    return _odd_out(x, mixed, w_out.astype(BF16))


def kernel(x, even_norm, even_w_in, even_b_f, even_q_gain, even_k_gain, even_w_out,
           odd_norm, odd_w_in, odd_q_gain, odd_k_gain, odd_w_out):
    assert x.shape[1] % DIL_POS == 0 and x.shape[2] == D_MODEL
    h = _even_layer(x, even_norm[0], even_w_in[0], even_b_f[0], even_q_gain[0], even_k_gain[0],
                    even_w_out[0])
    return _odd_layer(h, odd_norm[0], odd_w_in[0], odd_q_gain[0], odd_k_gain[0], odd_w_out[0])
```

```python
import functools
import math

import jax
import jax.numpy as jnp
from jax import lax
from jax.experimental import pallas as pl
from jax.experimental.pallas import tpu as pltpu

F32 = jnp.float32
BF16 = jnp.bfloat16

D_MODEL = 1024
DENSE_HEAD_DIM = 128
N_DENSE_HEADS = 4
DENSE_W = N_DENSE_HEADS * DENSE_HEAD_DIM
DIL_HEAD_DIM = 64
N_DIL_HEADS = 8
DIL_GROUP_W = N_DIL_HEADS * DIL_HEAD_DIM
DILATIONS = (1, 4, 16)
DIL_SPAN = 128
DIL_BLK = 128
RMS_EPS = 1e-6

LANES = 128
PROJ_ROWS = 512
ATT_ROWS = 512
FOX_KEYS = 512
SB_KEYS = 256
DIL_POS = DILATIONS[-1] * DIL_BLK
DIL_UNROLL = 16
VMEM_LIMIT = 56 * 1024 * 1024

EXP_ZERO = 104.0
LOG2E = math.log2(math.e)

_NT = (((1,), (1,)), ((), ()))


def _cparams(sem):
    return pltpu.CompilerParams(dimension_semantics=sem, vmem_limit_bytes=VMEM_LIMIT)


def _log_sigmoid(z):
    return jnp.minimum(z, 0.0) - jnp.log(1.0 + jnp.exp(-jnp.abs(z)))


def _log2_sigmoid_pair(z2):
    soft = jnp.log2(1.0 + jnp.exp2(-jnp.abs(z2)))
    log_beta = jnp.minimum(z2, 0.0) - soft
    return log_beta, log_beta - z2


def _silu(x):
    return x * (1.0 / (1.0 + jnp.exp(-x)))


def _row_rms(x, g):
    ms = jnp.mean(x * x, axis=-1, keepdims=True)
    return x * lax.rsqrt(ms + RMS_EPS) * g


def _even_proj_kernel(x_ref, gn_ref, w_ref, wf_ref, bf_ref, gq_ref, gk_ref,
                      fq_ref, fk_ref, fv_ref, sq_ref, sk_ref, sv_ref, sg_ref, cum_ref, carry_sc):
    i = pl.program_id(1)
    rows = x_ref.shape[0]
    h = _row_rms(x_ref[...], gn_ref[...]).astype(BF16)

    def proj(c0, width):
        return jnp.dot(h, w_ref[:, c0:c0 + width], preferred_element_type=F32)

    def head_norm(acc, g_ref, out_ref):
        for hh in range(N_DENSE_HEADS):
            sl = slice(hh * DENSE_HEAD_DIM, (hh + 1) * DENSE_HEAD_DIM)
            out_ref[:, sl] = _row_rms(acc[:, sl], g_ref[...]).astype(BF16)

    head_norm(proj(0, DENSE_W), gq_ref, fq_ref)
    head_norm(proj(DENSE_W, DENSE_W), gk_ref, fk_ref)
    fv_ref[...] = proj(2 * DENSE_W, DENSE_W).astype(BF16)
    sq_ref[...] = (proj(3 * DENSE_W, DENSE_W) * (DENSE_HEAD_DIM ** -0.5 * LOG2E)).astype(BF16)
    sk_ref[...] = proj(4 * DENSE_W, DENSE_W).astype(BF16)
    sv_ref[...] = proj(5 * DENSE_W, DENSE_W).astype(BF16)
    for c in range(2):
        sl = slice(c * DENSE_W, (c + 1) * DENSE_W)
        sg_ref[:, sl] = _silu(proj(6 * DENSE_W + c * DENSE_W, DENSE_W)).astype(BF16)

    fl = lax.dot_general(wf_ref[...], h, _NT, preferred_element_type=F32)[:8] + bf_ref[:, 0:1]
    lf = _log_sigmoid(fl)
    lane = lax.broadcasted_iota(jnp.int32, lf.shape, 1)
    shift = 1
    while shift < rows:
        lf = lf + jnp.where(lane >= shift, pltpu.roll(lf, shift, axis=1), 0.0)
        shift *= 2

    @pl.when(i == 0)
    def _():
        carry_sc[...] = jnp.zeros_like(carry_sc)

    cum = lf + carry_sc[:, 0:1]
    cum_ref[...] = cum * LOG2E
    carry_sc[...] = jnp.broadcast_to(cum[:, rows - 1:rows], carry_sc.shape)


def _even_proj(x, g_norm, w, wf, b_f, g_q, g_k):
    B, S, D = x.shape
    tm = PROJ_ROWS
    row = lambda width: pl.BlockSpec((None, tm, width), lambda b, i: (b, i, 0))
    const = lambda shape: pl.BlockSpec(shape, lambda b, i: (0,) * len(shape))
    act = lambda width: jax.ShapeDtypeStruct((B, S, width), BF16)
    return pl.pallas_call(
        _even_proj_kernel,
        grid=(B, S // tm),
        in_specs=[row(D), const((1, D)), const(w.shape), const(wf.shape), const(b_f.shape),
                  const(g_q.shape), const(g_k.shape)],
        out_specs=[row(DENSE_W)] * 6 + [row(2 * DENSE_W),
                                        pl.BlockSpec((None, 8, tm), lambda b, i: (b, 0, i))],
        out_shape=[act(DENSE_W)] * 6 + [act(2 * DENSE_W), jax.ShapeDtypeStruct((B, 8, S), F32)],
        scratch_shapes=[pltpu.VMEM((8, LANES), F32)],
        compiler_params=_cparams(("arbitrary", "arbitrary")),
        name="even_proj",
    )(x, g_norm, w, wf, b_f, g_q, g_k)


def _fox_kernel(cend_ref, thr_ref, q_ref, k_ref, v_ref, c_ref, sg_ref, o_ref, m_sc, acc_sc, s_sc):
    h, i = pl.program_id(0), pl.program_id(1)
    nbatch, tq = q_ref.shape[0], q_ref.shape[1]
    tk = FOX_KEYS
    nsub = tq // tk
    batches = range(nbatch)
    c_base = [jnp.where(i > 0, cend_ref[b * N_DENSE_HEADS + h, jnp.maximum(i * nsub - 1, 0)], 0.0)
              for b in batches]

    m_sc[...] = jnp.full_like(m_sc, -jnp.inf)
    acc_sc[...] = jnp.zeros_like(acc_sc)

    def logits(b, kt, rows):
        start = pl.multiple_of(kt * tk, tk)
        s = lax.dot_general(q_ref[b, rows, :], k_ref[b, pl.ds(start, tk), :], _NT,
                            preferred_element_type=F32)
        return s + (c_base[b] - c_ref[b, pl.ds(h, 1), pl.ds(start, tk)])

    def update(b, s, kt, rows):
        start = pl.multiple_of(kt * tk, tk)
        m_prev = m_sc[b, rows, :]
        m_new = jnp.maximum(m_prev, jnp.max(s, axis=-1, keepdims=True))
        alpha = jnp.exp2(m_prev - m_new)
        p = jnp.exp2(s - jnp.concatenate([m_new] * (tk // LANES), axis=1))
        v_ones = jnp.concatenate([v_ref[b, pl.ds(start, tk), :], jnp.ones((tk, LANES), BF16)], axis=1)
        pv = jnp.dot(p.astype(BF16), v_ones, preferred_element_type=F32)
        acc_sc[b, rows, :] = jnp.concatenate([alpha, alpha], axis=1) * acc_sc[b, rows, :] + pv
        m_sc[b, rows, :] = m_new

    for dd in range(nsub):
        rows = slice(dd * tk, tq)
        for b in batches:
            s = logits(b, i * nsub + dd, rows)
            r_io = lax.broadcasted_iota(jnp.int32, s.shape, 0)
            c_io = lax.broadcasted_iota(jnp.int32, s.shape, 1)
            update(b, jnp.where(c_io <= r_io, s, -jnp.inf), i * nsub + dd, rows)

    def wanted(kt):
        ktc = jnp.maximum(kt, 0)
        far = c_base[0] - cend_ref[h, ktc]
        for b in batches[1:]:
            far = jnp.maximum(far, c_base[b] - cend_ref[b * N_DENSE_HEADS + h, ktc])
        return jnp.logical_and(kt >= 0, far >= -thr_ref[0])

    everything = slice(0, tq)
    kt0 = i * nsub - 1
    for b in batches:
        s_sc[b] = logits(b, jnp.maximum(kt0, 0), everything)

    def body(c):
        kt, _ = c
        for b in batches:
            s = s_sc[b]
            s_sc[b] = logits(b, jnp.maximum(kt - 1, 0), everything)
            update(b, s, kt, everything)
        return kt - 1, wanted(kt - 1)

    lax.while_loop(lambda c: c[1], body, (kt0, wanted(kt0)))
    for b in batches:
        o_ref[b] = (acc_sc[b, :, :LANES] * (1.0 / acc_sc[b, :, LANES:])
                    * sg_ref[b].astype(F32)).astype(BF16)


def _fox_attn(cend, thr, fq, fk, fv, cum, sgate):
    B, S, _ = fq.shape
    tq = ATT_ROWS
    tile = pl.BlockSpec((B, tq, DENSE_HEAD_DIM), lambda h, i: (0, i, h))
    full = pl.BlockSpec((B, S, DENSE_HEAD_DIM), lambda h, i: (0, 0, h))
    smem = pl.BlockSpec(memory_space=pltpu.SMEM)
    return pl.pallas_call(
        _fox_kernel,
        grid=(N_DENSE_HEADS, S // tq),
        in_specs=[smem, smem, tile, full, full,
                  pl.BlockSpec((B, 8, S), lambda h, i: (0, 0, 0)), tile],
        out_specs=tile,
        out_shape=jax.ShapeDtypeStruct((B, S, DENSE_W), BF16),
        scratch_shapes=[pltpu.VMEM((B, tq, LANES), F32), pltpu.VMEM((B, tq, 2 * LANES), F32),
                        pltpu.VMEM((B, tq, FOX_KEYS), F32)],
        compiler_params=_cparams(("arbitrary", "arbitrary")),
        name="fox_attn",
    )(cend, thr, fq, fk, fv, cum, sgate)


def _sb_kernel(q_ref, k_ref, v_ref, u_ref, sg_ref, o_ref, off_sc, acc_sc, z_sc):
    i = pl.program_id(1)
    nbatch, tq = q_ref.shape[0], q_ref.shape[1]
    tk = SB_KEYS
    nsub = tq // tk
    batches = range(nbatch)
    off_sc[...] = jnp.zeros_like(off_sc)
    acc_sc[...] = jnp.zeros_like(acc_sc)

    def logits(b, kt, rows):
        start = pl.multiple_of(kt * tk, tk)
        return lax.dot_general(q_ref[b, rows, :], k_ref[b, pl.ds(start, tk), :], _NT,
                               preferred_element_type=F32)

    def update(b, z, kt, rows, diagonal):
        start = pl.multiple_of(kt * tk, tk)
        log_beta, log_om = _log2_sigmoid_pair(z)
        if diagonal:
            r_io = lax.broadcasted_iota(jnp.int32, z.shape, 0)
            c_io = lax.broadcasted_iota(jnp.int32, z.shape, 1)
            strict = c_io < r_io
            log_om = jnp.where(strict, log_om, 0.0)
        off = off_sc[b, rows, :]
        later = (jnp.dot(log_om.astype(BF16), u_ref[...], preferred_element_type=F32)
                 + jnp.concatenate([off] * (tk // LANES), axis=1))
        w = jnp.exp2(log_beta + later)
        if diagonal:
            w = jnp.where(strict, w, 0.0)
        acc_sc[b, rows, :] += jnp.dot(w.astype(BF16), v_ref[b, pl.ds(start, tk), :],
                                      preferred_element_type=F32)
        off_sc[b, rows, :] = off + jnp.sum(log_om, axis=-1, keepdims=True)

    for dd in reversed(range(nsub)):
        rows = slice(dd * tk, tq)
        for b in batches:
            update(b, logits(b, i * nsub + dd, rows), i * nsub + dd, rows, True)

    everything = slice(0, tq)
    kt0 = i * nsub - 1
    for b in batches:
        z_sc[b] = logits(b, jnp.maximum(kt0, 0), everything)

    def body(c):
        kt, _ = c
        for b in batches:
            z = z_sc[b]
            z_sc[b] = logits(b, jnp.maximum(kt - 1, 0), everything)
            update(b, z, kt, everything, False)
        alive = jnp.max(off_sc[...]) >= -(EXP_ZERO + 8.0) * LOG2E
        return kt - 1, jnp.logical_and(kt >= 1, alive)

    lax.while_loop(lambda c: c[1], body, (kt0, i >= 1))
    for b in batches:
        o_ref[b] = (acc_sc[b] * sg_ref[b].astype(F32)).astype(BF16)


def _sb_attn(sq, sk, sv, upper, sgate):
    B, S, _ = sq.shape
    tq, tk = ATT_ROWS, SB_KEYS
    tile = pl.BlockSpec((B, tq, DENSE_HEAD_DIM), lambda h, i: (0, i, h))
    gate_tile = pl.BlockSpec((B, tq, DENSE_HEAD_DIM), lambda h, i: (0, i, N_DENSE_HEADS + h))
    full = pl.BlockSpec((B, S, DENSE_HEAD_DIM), lambda h, i: (0, 0, h))
    return pl.pallas_call(
        _sb_kernel,
        grid=(N_DENSE_HEADS, S // tq),
        in_specs=[tile, full, full, pl.BlockSpec((tk, tk), lambda h, i: (0, 0)), gate_tile],
        out_specs=tile,
        out_shape=jax.ShapeDtypeStruct((B, S, DENSE_W), BF16),
        scratch_shapes=[pltpu.VMEM((B, tq, LANES), F32), pltpu.VMEM((B, tq, DENSE_HEAD_DIM), F32),
                        pltpu.VMEM((B, tq, SB_KEYS), F32)],
        compiler_params=_cparams(("arbitrary", "arbitrary")),
        name="sb_attn",
    )(sq, sk, sv, upper, sgate)


def _even_out_kernel(x_ref, mf_ref, ms_ref, wf_ref, ws_ref, o_ref):
    o_ref[...] = (x_ref[...]
                  + jnp.dot(mf_ref[...], wf_ref[...], preferred_element_type=F32)
                  + jnp.dot(ms_ref[...], ws_ref[...], preferred_element_type=F32))


def _even_out(x, mf, ms, wo_f, wo_s):
    B, S, D = x.shape
    tm = PROJ_ROWS
    row = lambda width: pl.BlockSpec((None, tm, width), lambda b, i: (b, i, 0))
    const = lambda shape: pl.BlockSpec(shape, lambda b, i: (0,) * len(shape))
    return pl.pallas_call(
        _even_out_kernel,
        grid=(B, S // tm),
        in_specs=[row(D), row(DENSE_W), row(DENSE_W), const(wo_f.shape), const(wo_s.shape)],
        out_specs=row(D),
        out_shape=jax.ShapeDtypeStruct((B, S, D), F32),
        compiler_params=_cparams(("arbitrary", "arbitrary")),
        name="even_out",
    )(x, mf, ms, wo_f, wo_s)


def _odd_out_kernel(x_ref, m_ref, w_ref, o_ref):
    o_ref[...] = x_ref[...] + jnp.dot(m_ref[...], w_ref[...], preferred_element_type=F32)


def _odd_out(x, mixed, wo):
    B, S, D = x.shape
    tm = PROJ_ROWS
    row = lambda width: pl.BlockSpec((None, tm, width), lambda b, i: (b, i, 0))
    return pl.pallas_call(
        _odd_out_kernel,
        grid=(B, S // tm),
        in_specs=[row(D), row(DIL_GROUP_W), pl.BlockSpec(wo.shape, lambda b, i: (0, 0))],
        out_specs=row(D),
        out_shape=jax.ShapeDtypeStruct((B, S, D), F32),
        compiler_params=_cparams(("arbitrary", "arbitrary")),
        name="odd_out",
    )(x, mixed, wo)


def _odd_proj_kernel(x_ref, gn_ref, w_ref, gq_ref, gk_ref, *rest):
    out_refs, sg_ref, tmp_sc = rest[:9], rest[9], rest[10]
    rows = x_ref.shape[0]
    h = _row_rms(x_ref[...], gn_ref[...]).astype(BF16)

    lane_lo = lax.broadcasted_iota(jnp.int32, (rows, LANES), 1) < DIL_HEAD_DIM

    def head_norm(acc, g_ref):
        cols = []
        for cb in range(DIL_GROUP_W // LANES):
            x = acc[:, cb * LANES:(cb + 1) * LANES]
            x2 = x * x
            lo = jnp.sum(jnp.where(lane_lo, x2, 0.0), axis=-1, keepdims=True)
            hi = jnp.sum(jnp.where(lane_lo, 0.0, x2), axis=-1, keepdims=True)
            ms = jnp.where(lane_lo, lo, hi) * (1.0 / DIL_HEAD_DIM)
            cols.append(x * lax.rsqrt(ms + RMS_EPS))
        return jnp.concatenate(cols, axis=1) * g_ref[...]

    for c in range(9):
        kind, g = divmod(c, 3)
        acc = jnp.dot(h, w_ref[:, c * DIL_GROUP_W:(c + 1) * DIL_GROUP_W], preferred_element_type=F32)
        if kind == 0:
            acc = head_norm(acc, gq_ref)
        elif kind == 1:
            acc = head_norm(acc, gk_ref)
        dil = DILATIONS[g]
        if dil == 1:
            out_refs[c][0] = acc.astype(BF16)
        else:
            for cb in range(DIL_GROUP_W // LANES):
                tmp_sc[cb] = acc[:, cb * LANES:(cb + 1) * LANES]
            for r in range(dil):
                for cb in range(DIL_GROUP_W // LANES):
                    out_refs[c][r, :, cb * LANES:(cb + 1) * LANES] = (
                        tmp_sc[cb, pl.ds(r, rows // dil, stride=dil), :].astype(BF16))
    gate = jnp.dot(h, w_ref[:, 9 * DIL_GROUP_W:10 * DIL_GROUP_W], preferred_element_type=F32)
    sg_ref[...] = _silu(gate).astype(BF16)


def _odd_proj(x, g_norm, w, g_q, g_k):
    B, S, D = x.shape
    tm = PROJ_ROWS
    const = lambda shape: pl.BlockSpec(shape, lambda b, i: (0,) * len(shape))
    out_specs, out_shape = [], []
    for _ in range(3):
        for dil in DILATIONS:
            out_specs.append(pl.BlockSpec((None, dil, tm // dil, DIL_GROUP_W), lambda b, i: (b, 0, i, 0)))
            out_shape.append(jax.ShapeDtypeStruct((B, dil, S // dil, DIL_GROUP_W), BF16))
    out_specs.append(pl.BlockSpec((None, tm, DIL_GROUP_W), lambda b, i: (b, i, 0)))
    out_shape.append(jax.ShapeDtypeStruct((B, S, DIL_GROUP_W), BF16))
    return pl.pallas_call(
        _odd_proj_kernel,
        grid=(B, S // tm),
        in_specs=[pl.BlockSpec((None, tm, D), lambda b, i: (b, i, 0)), const((1, D)), const(w.shape),
                  const(g_q.shape), const(g_k.shape)],
        out_specs=out_specs,
        out_shape=out_shape,
        scratch_shapes=[pltpu.VMEM((DIL_GROUP_W // LANES, tm, LANES), F32)],
        compiler_params=_cparams(("arbitrary", "arbitrary")),
        name="odd_proj",
    )(x, g_norm, w, g_q, g_k)


def _dil_kernel(slope_ref, *refs):
    ins, (sg_ref, o_ref), scr = refs[:15], refs[15:17], refs[17:]
    kbufs, vbufs, (o_sc, m_sc, l_sc) = scr[0:3], scr[3:6], scr[6:9]
    hp, st = pl.program_id(1), pl.program_id(2)
    blk = DIL_BLK

    a_io = lax.broadcasted_iota(jnp.int32, (blk, 2 * blk), 0)
    c_io = lax.broadcasted_iota(jnp.int32, (blk, 2 * blk), 1)
    dist = a_io - c_io + blk
    band = jnp.logical_and(dist >= 0, dist <= DIL_SPAN)
    neg = jnp.where(band, 0.0, -jnp.inf)
    neg_first = jnp.where(jnp.logical_and(band, c_io >= blk), 0.0, -jnp.inf)
    dist_f = dist.astype(F32)
    lane_lo = lax.broadcasted_iota(jnp.int32, (blk, LANES), 1) < DIL_HEAD_DIM

    for g, dil in enumerate(DILATIONS):
        q_ref, kc_ref, kh_ref, vc_ref, vh_ref = ins[5 * g:5 * g + 5]
        kbuf, vbuf = kbufs[g], vbufs[g]
        length = DIL_POS // dil
        nb = length // blk
        kbuf[:, 0:blk, :] = kh_ref[...]
        kbuf[:, blk:, :] = kc_ref[...]
        vbuf[:, 0:blk, 0:LANES] = vh_ref[...]
        vbuf[:, blk:, 0:LANES] = vc_ref[...]
        vbuf[:, :, LANES:] = jnp.ones((dil, length + blk, LANES), BF16)
        bias = []
        for hd in range(2):
            slope = slope_ref[g * N_DIL_HEADS + 2 * hp + hd]
            alibi = dist_f * (slope * float(dil))
            bias.append((neg - alibi, neg_first - alibi))

        def block(idx, carry, g=g, dil=dil, nb=nb, q_ref=q_ref, kbuf=kbuf, vbuf=vbuf, bias=bias):
            r = idx // nb
            n = idx - r * nb
            row0 = pl.multiple_of(n * blk, blk)
            q2 = q_ref[r, pl.ds(row0, blk), :]
            k2 = kbuf[r, pl.ds(row0, 2 * blk), :]
            v2 = vbuf[r, pl.ds(row0, 2 * blk), :]
            first = jnp.logical_and(st == 0, n == 0)
            outs = []
            for hd in range(2):
                keep = lane_lo if hd == 0 else jnp.logical_not(lane_lo)
                qm = jnp.where(keep, q2, jnp.zeros_like(q2))
                s = lax.dot_general(qm, k2, _NT, preferred_element_type=F32)
                s = s + jnp.where(first, bias[hd][1], bias[hd][0])
                m = jnp.max(s, axis=-1, keepdims=True)
                p = jnp.exp2(s - m)
                ol = jnp.dot(p.astype(BF16), v2, preferred_element_type=F32)
                outs.append((ol[:, :LANES], m, ol[:, LANES:]))
            o = jnp.where(lane_lo, outs[0][0], outs[1][0])
            m = jnp.where(lane_lo, outs[0][1], outs[1][1])
            l = jnp.where(lane_lo, outs[0][2], outs[1][2])
            start = n * (blk * dil) + r
            if dil == 1:
                dst = pl.ds(pl.multiple_of(start, blk), blk)
            else:
                dst = pl.ds(start, blk, stride=dil)
            o_sc[g, dst, :] = o
            m_sc[g, dst, :] = m
            l_sc[g, dst, :] = l
            return carry

        lax.fori_loop(0, dil * nb, block, 0, unroll=DIL_UNROLL)

    chunk = 256

    def merge(ci, carry):
        rows = pl.ds(pl.multiple_of(ci * chunk, chunk), chunk)
        ms = [m_sc[g, rows, :] for g in range(3)]
        m_all = jnp.maximum(jnp.maximum(ms[0], ms[1]), ms[2])
        num = jnp.zeros((chunk, LANES), F32)
        den = jnp.zeros((chunk, LANES), F32)
        for g in range(3):
            e = jnp.exp2(ms[g] - m_all)
            num = num + e * o_sc[g, rows, :]
            den = den + e * l_sc[g, rows, :]
        o_ref[rows, :] = (num / den * sg_ref[rows, :].astype(F32)).astype(BF16)
        return carry

    lax.fori_loop(0, DIL_POS // chunk, merge, 0)


def _dil_attn(slopes, qkv, sgate):
    B, S, _ = sgate.shape
    n_pairs = N_DIL_HEADS * DIL_HEAD_DIM // LANES
    blk = DIL_BLK
    in_specs = [pl.BlockSpec(memory_space=pltpu.SMEM)]
    args = [slopes]
    scratch_k, scratch_v = [], []
    for g, dil in enumerate(DILATIONS):
        length = DIL_POS // dil
        nb = length // blk
        cur = pl.BlockSpec((None, dil, length, LANES), lambda b, hp, st: (b, 0, st, hp))
        halo = pl.BlockSpec((None, dil, blk, LANES),
                            lambda b, hp, st, nb=nb: (b, 0, jnp.maximum(st * nb - 1, 0), hp))
        q, k, v = qkv[g], qkv[3 + g], qkv[6 + g]
        in_specs += [cur, cur, halo, cur, halo]
        args += [q, k, k, v, v]
        scratch_k.append(pltpu.VMEM((dil, length + blk, LANES), BF16))
        scratch_v.append(pltpu.VMEM((dil, length + blk, 2 * LANES), BF16))
    pos = pl.BlockSpec((None, DIL_POS, LANES), lambda b, hp, st: (b, st, hp))
    in_specs.append(pos)
    args.append(sgate)
    return pl.pallas_call(
        _dil_kernel,
        grid=(B, n_pairs, S // DIL_POS),
        in_specs=in_specs,
        out_specs=pos,
        out_shape=jax.ShapeDtypeStruct((B, S, DIL_GROUP_W), BF16),
        scratch_shapes=scratch_k + scratch_v + [pltpu.VMEM((3, DIL_POS, LANES), F32)] * 3,
        compiler_params=_cparams(("arbitrary", "arbitrary", "arbitrary")),
        name="dil_attn",
    )(*args)


def _even_layer(x, g_norm, w_in, b_f, g_q, g_k, w_out):
    B, S, D = x.shape
    n_f = N_DENSE_HEADS
    cut0, cut1 = 3 * DENSE_W, 3 * DENSE_W + n_f
    w = jnp.concatenate([w_in[:, :cut0], w_in[:, cut1:]], axis=1).astype(BF16)
    wf = jnp.zeros((16, D), F32).at[:n_f].set(w_in[:, cut0:cut1].T).astype(BF16)
    bf = jnp.zeros((8, LANES), F32).at[:n_f].set(jnp.broadcast_to(b_f[:, None], (n_f, LANES)))
    gq = (g_q * (DENSE_HEAD_DIM ** -0.5 * LOG2E)).reshape(1, DENSE_HEAD_DIM)
    gk = g_k.reshape(1, DENSE_HEAD_DIM)
    fq, fk, fv, sq, sk, sv, sgate, cum = _even_proj(x, g_norm.reshape(1, D), w, wf, bf, gq, gk)

    t = FOX_KEYS
    cend = cum[:, :n_f, t - 1::t].reshape(B * n_f, S // t)
    qk_bound = math.sqrt(DENSE_HEAD_DIM) * jnp.max(jnp.abs(g_q)) * jnp.max(jnp.abs(g_k)) * 1.02
    thr = ((EXP_ZERO + 1.0 + 2.0 * qk_bound) * LOG2E).reshape(1).astype(F32)
    mixed_f = _fox_attn(cend, thr, fq, fk, fv, cum, sgate)

    idx = jnp.arange(SB_KEYS)
    upper = (idx[:, None] > idx[None, :]).astype(BF16)
    mixed_s = _sb_attn(sq, sk, sv, upper, sgate)
    return _even_out(x, mixed_f, mixed_s, w_out[:DENSE_W].astype(BF16), w_out[DENSE_W:].astype(BF16))


def _odd_layer(x, g_norm, w_in, g_q, g_k, w_out):
    B, S, D = x.shape
    w = w_in.astype(BF16)
    gq = jnp.tile(g_q * (DIL_HEAD_DIM ** -0.5 * LOG2E), N_DIL_HEADS).reshape(1, DIL_GROUP_W)
    gk = jnp.tile(g_k, N_DIL_HEADS).reshape(1, DIL_GROUP_W)
    outs = _odd_proj(x, g_norm.reshape(1, D), w, gq, gk)
    n_all = len(DILATIONS) * N_DIL_HEADS
    slopes = jnp.asarray([LOG2E * 2.0 ** (-8.0 * (i + 1) / n_all) for i in range(n_all)], F32)
    mixed = _dil_attn(slopes, outs[:9], outs[9])
    return _odd_out(x, mixed, w_out.astype(BF16))


def kernel(x, even_norm, even_w_in, even_b_f, even_q_gain, even_k_gain, even_w_out,
           odd_norm, odd_w_in, odd_q_gain, odd_k_gain, odd_w_out):
    assert x.shape[1] % DIL_POS == 0 and x.shape[2] == D_MODEL
    h = _even_layer(x, even_norm[0], even_w_in[0], even_b_f[0], even_q_gain[0], even_k_gain[0],
                    even_w_out[0])
    return _odd_layer(h, odd_norm[0], odd_w_in[0], odd_q_gain[0], odd_k_gain[0], odd_w_out[0])
```

```python
import functools
import math

import jax
import jax.numpy as jnp
from jax import lax
from jax.experimental import pallas as pl
from jax.experimental.pallas import tpu as pltpu

F32 = jnp.float32
BF16 = jnp.bfloat16

D_MODEL = 1024
DENSE_HEAD_DIM = 128
N_DENSE_HEADS = 4
DENSE_W = N_DENSE_HEADS * DENSE_HEAD_DIM
DIL_HEAD_DIM = 64
N_DIL_HEADS = 8
DIL_GROUP_W = N_DIL_HEADS * DIL_HEAD_DIM
DILATIONS = (1, 4, 16)
DIL_SPAN = 128
DIL_BLK = 128
RMS_EPS = 1e-6

LANES = 128
PROJ_ROWS = 1024
ODD_PROJ_ROWS = 512
ATT_ROWS = 512
FOX_KEYS = 512
SB_KEYS = 256
DIL_POS = DILATIONS[-1] * DIL_BLK
DIL_UNROLL = 16
VMEM_LIMIT = 56 * 1024 * 1024

EXP_ZERO = 104.0
LOG2E = math.log2(math.e)

_NT = (((1,), (1,)), ((), ()))


def _cparams(sem):
    return pltpu.CompilerParams(dimension_semantics=sem, vmem_limit_bytes=VMEM_LIMIT)


def _resident_spec(shape):
    return pl.BlockSpec(shape, lambda *_: (0,) * len(shape), pipeline_mode=pl.Buffered(1))


def _log_sigmoid(z):
    return jnp.minimum(z, 0.0) - jnp.log(1.0 + jnp.exp(-jnp.abs(z)))


def _log2_sigmoid_pair(z2):
    soft = jnp.log2(1.0 + jnp.exp2(-jnp.abs(z2)))
    log_beta = jnp.minimum(z2, 0.0) - soft
    return log_beta, log_beta - z2


def _silu(x):
    return x * (1.0 / (1.0 + jnp.exp(-x)))


def _row_rms(x, g):
    ms = jnp.mean(x * x, axis=-1, keepdims=True)
    return x * lax.rsqrt(ms + RMS_EPS) * g


def _even_proj_kernel(x_ref, gn_ref, w_ref, wf_ref, bf_ref, gq_ref, gk_ref,
                      fq_ref, fk_ref, fv_ref, sq_ref, sk_ref, sv_ref, sg_ref, cum_ref, carry_sc):
    i = pl.program_id(1)
    rows = x_ref.shape[0]
    h = _row_rms(x_ref[...], gn_ref[...]).astype(BF16)

    def proj(c0, width):
        return jnp.dot(h, w_ref[:, c0:c0 + width], preferred_element_type=F32)

    def head_norm(acc, g_ref, out_ref):
        for hh in range(N_DENSE_HEADS):
            sl = slice(hh * DENSE_HEAD_DIM, (hh + 1) * DENSE_HEAD_DIM)
            out_ref[:, sl] = _row_rms(acc[:, sl], g_ref[...]).astype(BF16)

    head_norm(proj(0, DENSE_W), gq_ref, fq_ref)
    head_norm(proj(DENSE_W, DENSE_W), gk_ref, fk_ref)
    fv_ref[...] = proj(2 * DENSE_W, DENSE_W).astype(BF16)
    sq_ref[...] = (proj(3 * DENSE_W, DENSE_W) * (DENSE_HEAD_DIM ** -0.5 * LOG2E)).astype(BF16)
    sk_ref[...] = proj(4 * DENSE_W, DENSE_W).astype(BF16)
    sv_ref[...] = proj(5 * DENSE_W, DENSE_W).astype(BF16)
    for c in range(2):
        sl = slice(c * DENSE_W, (c + 1) * DENSE_W)
        sg_ref[:, sl] = _silu(proj(6 * DENSE_W + c * DENSE_W, DENSE_W)).astype(BF16)

    fl = lax.dot_general(wf_ref[...], h, _NT, preferred_element_type=F32)[:8] + bf_ref[:, 0:1]
    lf = _log_sigmoid(fl)
    lane = lax.broadcasted_iota(jnp.int32, lf.shape, 1)
    shift = 1
    while shift < rows:
        lf = lf + jnp.where(lane >= shift, pltpu.roll(lf, shift, axis=1), 0.0)
        shift *= 2

    @pl.when(i == 0)
    def _():
        carry_sc[...] = jnp.zeros_like(carry_sc)

    cum = lf + carry_sc[:, 0:1]
    cum_ref[...] = cum * LOG2E
    carry_sc[...] = jnp.broadcast_to(cum[:, rows - 1:rows], carry_sc.shape)


def _even_proj(x, g_norm, w, wf, b_f, g_q, g_k):
    B, S, D = x.shape
    tm = PROJ_ROWS
    row = lambda width: pl.BlockSpec((None, tm, width), lambda b, i: (b, i, 0))
    const = _resident_spec
    act = lambda width: jax.ShapeDtypeStruct((B, S, width), BF16)
    return pl.pallas_call(
        _even_proj_kernel,
        grid=(B, S // tm),
        in_specs=[row(D), const((1, D)), const(w.shape), const(wf.shape), const(b_f.shape),
                  const(g_q.shape), const(g_k.shape)],
        out_specs=[row(DENSE_W)] * 6 + [row(2 * DENSE_W),
                                        pl.BlockSpec((None, 8, tm), lambda b, i: (b, 0, i))],
        out_shape=[act(DENSE_W)] * 6 + [act(2 * DENSE_W), jax.ShapeDtypeStruct((B, 8, S), F32)],
        scratch_shapes=[pltpu.VMEM((8, LANES), F32)],
        compiler_params=_cparams(("arbitrary", "arbitrary")),
        name="even_proj",
    )(x, g_norm, w, wf, b_f, g_q, g_k)


def _fox_kernel(cend_ref, thr_ref, q_ref, k_ref, v_ref, c_ref, sg_ref, o_ref, m_sc, acc_sc, s_sc):
    h, i = pl.program_id(0), pl.program_id(1)
    nbatch, tq = q_ref.shape[0], q_ref.shape[1]
    tk = FOX_KEYS
    nsub = tq // tk
    batches = range(nbatch)
    c_base = [jnp.where(i > 0, cend_ref[b * N_DENSE_HEADS + h, jnp.maximum(i * nsub - 1, 0)], 0.0)
              for b in batches]

    m_sc[...] = jnp.full_like(m_sc, -jnp.inf)
    acc_sc[...] = jnp.zeros_like(acc_sc)

    def logits(b, kt, rows):
        start = pl.multiple_of(kt * tk, tk)
        s = lax.dot_general(q_ref[b, rows, :], k_ref[b, pl.ds(start, tk), :], _NT,
                            preferred_element_type=F32)
        return s + (c_base[b] - c_ref[b, pl.ds(h, 1), pl.ds(start, tk)])

    def update(b, s, kt, rows):
        start = pl.multiple_of(kt * tk, tk)
        m_prev = m_sc[b, rows, :]
        m_new = jnp.maximum(m_prev, jnp.max(s, axis=-1, keepdims=True))
        alpha = jnp.exp2(m_prev - m_new)
        p = jnp.exp2(s - jnp.concatenate([m_new] * (tk // LANES), axis=1))
        v_ones = jnp.concatenate([v_ref[b, pl.ds(start, tk), :], jnp.ones((tk, LANES), BF16)], axis=1)
        pv = jnp.dot(p.astype(BF16), v_ones, preferred_element_type=F32)
        acc_sc[b, rows, :] = jnp.concatenate([alpha, alpha], axis=1) * acc_sc[b, rows, :] + pv
        m_sc[b, rows, :] = m_new

    for dd in range(nsub):
        rows = slice(dd * tk, tq)
        for b in batches:
            s = logits(b, i * nsub + dd, rows)
            r_io = lax.broadcasted_iota(jnp.int32, s.shape, 0)
            c_io = lax.broadcasted_iota(jnp.int32, s.shape, 1)
            update(b, jnp.where(c_io <= r_io, s, -jnp.inf), i * nsub + dd, rows)

    def wanted(kt):
        ktc = jnp.maximum(kt, 0)
        far = c_base[0] - cend_ref[h, ktc]
        for b in batches[1:]:
            far = jnp.maximum(far, c_base[b] - cend_ref[b * N_DENSE_HEADS + h, ktc])
        return jnp.logical_and(kt >= 0, far >= -thr_ref[0])

    everything = slice(0, tq)
    kt0 = i * nsub - 1
    for b in batches:
        s_sc[b] = logits(b, jnp.maximum(kt0, 0), everything)

    def body(c):
        kt, _ = c
        for b in batches:
            s = s_sc[b]
            s_sc[b] = logits(b, jnp.maximum(kt - 1, 0), everything)
            update(b, s, kt, everything)
        return kt - 1, wanted(kt - 1)

    lax.while_loop(lambda c: c[1], body, (kt0, wanted(kt0)))
    for b in batches:
        o_ref[b] = (acc_sc[b, :, :LANES] * (1.0 / acc_sc[b, :, LANES:])
                    * sg_ref[b].astype(F32)).astype(BF16)


def _fox_attn(cend, thr, fq, fk, fv, cum, sgate):
    B, S, _ = fq.shape
    tq = ATT_ROWS
    tile = pl.BlockSpec((B, tq, DENSE_HEAD_DIM), lambda h, i: (0, i, h))
    full = pl.BlockSpec((B, S, DENSE_HEAD_DIM), lambda h, i: (0, 0, h))
    smem = pl.BlockSpec(memory_space=pltpu.SMEM)
    return pl.pallas_call(
        _fox_kernel,
        grid=(N_DENSE_HEADS, S // tq),
        in_specs=[smem, smem, tile, full, full,
                  pl.BlockSpec((B, 8, S), lambda h, i: (0, 0, 0)), tile],
        out_specs=tile,
        out_shape=jax.ShapeDtypeStruct((B, S, DENSE_W), BF16),
        scratch_shapes=[pltpu.VMEM((B, tq, LANES), F32), pltpu.VMEM((B, tq, 2 * LANES), F32),
                        pltpu.VMEM((B, tq, FOX_KEYS), F32)],
        compiler_params=_cparams(("arbitrary", "arbitrary")),
        name="fox_attn",
    )(cend, thr, fq, fk, fv, cum, sgate)


def _sb_kernel(q_ref, k_ref, v_ref, u_ref, sg_ref, o_ref, off_sc, acc_sc, z_sc):
    i = pl.program_id(1)
    nbatch, tq = q_ref.shape[0], q_ref.shape[1]
    tk = SB_KEYS
    nsub = tq // tk
    batches = range(nbatch)
    off_sc[...] = jnp.zeros_like(off_sc)
    acc_sc[...] = jnp.zeros_like(acc_sc)

    def logits(b, kt, rows):
        start = pl.multiple_of(kt * tk, tk)
        return lax.dot_general(q_ref[b, rows, :], k_ref[b, pl.ds(start, tk), :], _NT,
                               preferred_element_type=F32)

    def update(b, z, kt, rows, diagonal):
        start = pl.multiple_of(kt * tk, tk)
        log_beta, log_om = _log2_sigmoid_pair(z)
        if diagonal:
            r_io = lax.broadcasted_iota(jnp.int32, z.shape, 0)
            c_io = lax.broadcasted_iota(jnp.int32, z.shape, 1)
            strict = c_io < r_io
            log_om = jnp.where(strict, log_om, 0.0)
        off = off_sc[b, rows, :]
        later = (jnp.dot(log_om.astype(BF16), u_ref[...], preferred_element_type=F32)
                 + jnp.concatenate([off] * (tk // LANES), axis=1))
        w = jnp.exp2(log_beta + later)
        if diagonal:
            w = jnp.where(strict, w, 0.0)
        acc_sc[b, rows, :] += jnp.dot(w.astype(BF16), v_ref[b, pl.ds(start, tk), :],
                                      preferred_element_type=F32)
        off_sc[b, rows, :] = off + jnp.sum(log_om, axis=-1, keepdims=True)

    for dd in reversed(range(nsub)):
        rows = slice(dd * tk, tq)
        for b in batches:
            update(b, logits(b, i * nsub + dd, rows), i * nsub + dd, rows, True)

    everything = slice(0, tq)
    kt0 = i * nsub - 1
    for b in batches:
        z_sc[b] = logits(b, jnp.maximum(kt0, 0), everything)

    def body(c):
        kt, _ = c
        for b in batches:
            z = z_sc[b]
            z_sc[b] = logits(b, jnp.maximum(kt - 1, 0), everything)
            update(b, z, kt, everything, False)
        alive = jnp.max(off_sc[...]) >= -(EXP_ZERO + 8.0) * LOG2E
        return kt - 1, jnp.logical_and(kt >= 1, alive)

    lax.while_loop(lambda c: c[1], body, (kt0, i >= 1))
    for b in batches:
        o_ref[b] = (acc_sc[b] * sg_ref[b].astype(F32)).astype(BF16)


def _sb_attn(sq, sk, sv, upper, sgate):
    B, S, _ = sq.shape
    tq, tk = ATT_ROWS, SB_KEYS
    tile = pl.BlockSpec((B, tq, DENSE_HEAD_DIM), lambda h, i: (0, i, h))
    gate_tile = pl.BlockSpec((B, tq, DENSE_HEAD_DIM), lambda h, i: (0, i, N_DENSE_HEADS + h))
    full = pl.BlockSpec((B, S, DENSE_HEAD_DIM), lambda h, i: (0, 0, h))
    return pl.pallas_call(
        _sb_kernel,
        grid=(N_DENSE_HEADS, S // tq),
        in_specs=[tile, full, full, pl.BlockSpec((tk, tk), lambda h, i: (0, 0)), gate_tile],
        out_specs=tile,
        out_shape=jax.ShapeDtypeStruct((B, S, DENSE_W), BF16),
        scratch_shapes=[pltpu.VMEM((B, tq, LANES), F32), pltpu.VMEM((B, tq, DENSE_HEAD_DIM), F32),
                        pltpu.VMEM((B, tq, SB_KEYS), F32)],
        compiler_params=_cparams(("arbitrary", "arbitrary")),
        name="sb_attn",
    )(sq, sk, sv, upper, sgate)


def _odd_out_kernel(x_ref, m_ref, w_ref, o_ref):
    o_ref[...] = x_ref[...] + jnp.dot(m_ref[...], w_ref[...], preferred_element_type=F32)


def _odd_out(x, mixed, wo):
    B, S, D = x.shape
    tm = PROJ_ROWS
    row = lambda width: pl.BlockSpec((None, tm, width), lambda b, i: (b, i, 0))
    return pl.pallas_call(
        _odd_out_kernel,
        grid=(B, S // tm),
        in_specs=[row(D), row(DIL_GROUP_W), _resident_spec(wo.shape)],
        out_specs=row(D),
        out_shape=jax.ShapeDtypeStruct((B, S, D), F32),
        compiler_params=_cparams(("arbitrary", "arbitrary")),
        name="odd_out",
    )(x, mixed, wo)


def _odd_proj_kernel(x_ref, mf_ref, ms_ref, wof_ref, wos_ref, gn_ref, w_ref, gq_ref, gk_ref, res_ref, *rest):
    out_refs, sg_ref, h_sc = rest[:9], rest[9], rest[10]
    rows = x_ref.shape[0]
    n_slab = D_MODEL // LANES
    res = (x_ref[...]
           + jnp.dot(mf_ref[...], wof_ref[...], preferred_element_type=F32)
           + jnp.dot(ms_ref[...], wos_ref[...], preferred_element_type=F32))
    res_ref[...] = res
    hf = _row_rms(res, gn_ref[...])
    h = hf.astype(BF16)
    for cb in range(n_slab):
        h_sc[cb] = hf[:, cb * LANES:(cb + 1) * LANES]

    def by_residue(dil):
        parts = [jnp.concatenate([h_sc[cb, pl.ds(r, rows // dil, stride=dil), :] for cb in range(n_slab)],
                                 axis=1) for r in range(dil)]
        return jnp.concatenate(parts, axis=0).astype(BF16)

    lhs = [h] + [by_residue(dil) for dil in DILATIONS[1:]]

    lane_lo = lax.broadcasted_iota(jnp.int32, (rows, LANES), 1) < DIL_HEAD_DIM

    def head_norm(acc, g_ref):
        cols = []
        for cb in range(DIL_GROUP_W // LANES):
            x = acc[:, cb * LANES:(cb + 1) * LANES]
            x2 = x * x
            lo = jnp.sum(jnp.where(lane_lo, x2, 0.0), axis=-1, keepdims=True)
            hi = jnp.sum(jnp.where(lane_lo, 0.0, x2), axis=-1, keepdims=True)
            ms = jnp.where(lane_lo, lo, hi) * (1.0 / DIL_HEAD_DIM)
            cols.append(x * lax.rsqrt(ms + RMS_EPS))
        return jnp.concatenate(cols, axis=1) * g_ref[...]

    for c in range(9):
        kind, g = divmod(c, 3)
        acc = jnp.dot(lhs[g], w_ref[:, c * DIL_GROUP_W:(c + 1) * DIL_GROUP_W],
                      preferred_element_type=F32)
        if kind == 0:
            acc = head_norm(acc, gq_ref)
        elif kind == 1:
            acc = head_norm(acc, gk_ref)
        dil = DILATIONS[g]
        per = rows // dil
        for r in range(dil):
            out_refs[c][r] = acc[r * per:(r + 1) * per, :].astype(BF16)
    gate = jnp.dot(h, w_ref[:, 9 * DIL_GROUP_W:10 * DIL_GROUP_W], preferred_element_type=F32)
    sg_ref[...] = _silu(gate).astype(BF16)


def _odd_proj(x, mixed_f, mixed_s, wo_f, wo_s, g_norm, w, g_q, g_k):
    B, S, D = x.shape
    tm = ODD_PROJ_ROWS
    const = _resident_spec
    row = lambda width: pl.BlockSpec((None, tm, width), lambda b, i: (b, i, 0))
    out_specs, out_shape = [row(D)], [jax.ShapeDtypeStruct((B, S, D), F32)]
    for _ in range(3):
        for dil in DILATIONS:
            out_specs.append(pl.BlockSpec((None, dil, tm // dil, DIL_GROUP_W), lambda b, i: (b, 0, i, 0)))
            out_shape.append(jax.ShapeDtypeStruct((B, dil, S // dil, DIL_GROUP_W), BF16))
    out_specs.append(pl.BlockSpec((None, tm, DIL_GROUP_W), lambda b, i: (b, i, 0)))
    out_shape.append(jax.ShapeDtypeStruct((B, S, DIL_GROUP_W), BF16))
    return pl.pallas_call(
        _odd_proj_kernel,
        grid=(B, S // tm),
        in_specs=[row(D), row(DENSE_W), row(DENSE_W), const(wo_f.shape), const(wo_s.shape),
                  const((1, D)), const(w.shape), const(g_q.shape), const(g_k.shape)],
        out_specs=out_specs,
        out_shape=out_shape,
        scratch_shapes=[pltpu.VMEM((D // LANES, tm, LANES), F32)],
        compiler_params=_cparams(("arbitrary", "arbitrary")),
        name="odd_proj",
    )(x, mixed_f, mixed_s, wo_f, wo_s, g_norm, w, g_q, g_k)


def _dil_kernel(slope_ref, *refs):
    ins, (sg_ref, o_ref), scr = refs[:15], refs[15:17], refs[17:]
    kbufs, vbufs, (o_sc, m_sc, l_sc) = scr[0:3], scr[3:6], scr[6:9]
    hp, st = pl.program_id(1), pl.program_id(2)
    blk = DIL_BLK

    a_io = lax.broadcasted_iota(jnp.int32, (blk, 2 * blk), 0)
    c_io = lax.broadcasted_iota(jnp.int32, (blk, 2 * blk), 1)
    dist = a_io - c_io + blk
    band = jnp.logical_and(dist >= 0, dist <= DIL_SPAN)
    neg = jnp.where(band, 0.0, -jnp.inf)
    neg_first = jnp.where(jnp.logical_and(band, c_io >= blk), 0.0, -jnp.inf)
    dist_f = dist.astype(F32)
    lane_lo = lax.broadcasted_iota(jnp.int32, (blk, LANES), 1) < DIL_HEAD_DIM

    for g, dil in enumerate(DILATIONS):
        q_ref, kc_ref, kh_ref, vc_ref, vh_ref = ins[5 * g:5 * g + 5]
        kbuf, vbuf = kbufs[g], vbufs[g]
        length = DIL_POS // dil
        nb = length // blk
        kbuf[:, 0:blk, :] = kh_ref[...]
        kbuf[:, blk:, :] = kc_ref[...]
        vbuf[:, 0:blk, 0:LANES] = vh_ref[...]
        vbuf[:, blk:, 0:LANES] = vc_ref[...]
        vbuf[:, :, LANES:] = jnp.ones((dil, length + blk, LANES), BF16)
        bias = []
        for hd in range(2):
            slope = slope_ref[g * N_DIL_HEADS + 2 * hp + hd]
            alibi = dist_f * (slope * float(dil))
            bias.append((neg - alibi, neg_first - alibi))

        def block(idx, carry, g=g, dil=dil, nb=nb, q_ref=q_ref, kbuf=kbuf, vbuf=vbuf, bias=bias):
            r = idx // nb
            n = idx - r * nb
            row0 = pl.multiple_of(n * blk, blk)
            q2 = q_ref[r, pl.ds(row0, blk), :]
            k2 = kbuf[r, pl.ds(row0, 2 * blk), :]
            v2 = vbuf[r, pl.ds(row0, 2 * blk), :]
            first = jnp.logical_and(st == 0, n == 0)
            outs = []
            for hd in range(2):
                keep = lane_lo if hd == 0 else jnp.logical_not(lane_lo)
                qm = jnp.where(keep, q2, jnp.zeros_like(q2))
                s = lax.dot_general(qm, k2, _NT, preferred_element_type=F32)
                s = s + jnp.where(first, bias[hd][1], bias[hd][0])
                m = jnp.max(s, axis=-1, keepdims=True)
                p = jnp.exp2(s - m)
                ol = jnp.dot(p.astype(BF16), v2, preferred_element_type=F32)
                outs.append((ol[:, :LANES], m, ol[:, LANES:]))
            o = jnp.where(lane_lo, outs[0][0], outs[1][0])
            m = jnp.where(lane_lo, outs[0][1], outs[1][1])
            l = jnp.where(lane_lo, outs[0][2], outs[1][2])
            start = n * (blk * dil) + r
            if dil == 1:
                dst = pl.ds(pl.multiple_of(start, blk), blk)
            else:
                dst = pl.ds(start, blk, stride=dil)
            o_sc[g, dst, :] = o
            m_sc[g, dst, :] = m
            l_sc[g, dst, :] = l
            return carry

        lax.fori_loop(0, dil * nb, block, 0, unroll=DIL_UNROLL)

    chunk = 256

    def merge(ci, carry):
        rows = pl.ds(pl.multiple_of(ci * chunk, chunk), chunk)
        ms = [m_sc[g, rows, :] for g in range(3)]
        m_all = jnp.maximum(jnp.maximum(ms[0], ms[1]), ms[2])
        num = jnp.zeros((chunk, LANES), F32)
        den = jnp.zeros((chunk, LANES), F32)
        for g in range(3):
            e = jnp.exp2(ms[g] - m_all)
            num = num + e * o_sc[g, rows, :]
            den = den + e * l_sc[g, rows, :]
        o_ref[rows, :] = (num / den * sg_ref[rows, :].astype(F32)).astype(BF16)
        return carry

    lax.fori_loop(0, DIL_POS // chunk, merge, 0)


def _dil_attn(slopes, qkv, sgate):
    B, S, _ = sgate.shape
    n_pairs = N_DIL_HEADS * DIL_HEAD_DIM // LANES
    blk = DIL_BLK
    in_specs = [pl.BlockSpec(memory_space=pltpu.SMEM)]
    args = [slopes]
    scratch_k, scratch_v = [], []
    for g, dil in enumerate(DILATIONS):
        length = DIL_POS // dil
        nb = length // blk
        cur = pl.BlockSpec((None, dil, length, LANES), lambda b, hp, st: (b, 0, st, hp))
        halo = pl.BlockSpec((None, dil, blk, LANES),
                            lambda b, hp, st, nb=nb: (b, 0, jnp.maximum(st * nb - 1, 0), hp))
        q, k, v = qkv[g], qkv[3 + g], qkv[6 + g]
        in_specs += [cur, cur, halo, cur, halo]
        args += [q, k, k, v, v]
        scratch_k.append(pltpu.VMEM((dil, length + blk, LANES), BF16))
        scratch_v.append(pltpu.VMEM((dil, length + blk, 2 * LANES), BF16))
    pos = pl.BlockSpec((None, DIL_POS, LANES), lambda b, hp, st: (b, st, hp))
    in_specs.append(pos)
    args.append(sgate)
    return pl.pallas_call(
        _dil_kernel,
        grid=(B, n_pairs, S // DIL_POS),
        in_specs=in_specs,
        out_specs=pos,
        out_shape=jax.ShapeDtypeStruct((B, S, DIL_GROUP_W), BF16),
        scratch_shapes=scratch_k + scratch_v + [pltpu.VMEM((3, DIL_POS, LANES), F32)] * 3,
        compiler_params=_cparams(("arbitrary", "arbitrary", "arbitrary")),
        name="dil_attn",
    )(*args)


def _even_mixers(x, g_norm, w_in, b_f, g_q, g_k):
    B, S, D = x.shape
    n_f = N_DENSE_HEADS
    cut0, cut1 = 3 * DENSE_W, 3 * DENSE_W + n_f
    w = jnp.concatenate([w_in[:, :cut0], w_in[:, cut1:]], axis=1).astype(BF16)
    wf = jnp.zeros((16, D), F32).at[:n_f].set(w_in[:, cut0:cut1].T).astype(BF16)
    bf = jnp.zeros((8, LANES), F32).at[:n_f].set(jnp.broadcast_to(b_f[:, None], (n_f, LANES)))
    gq = (g_q * (DENSE_HEAD_DIM ** -0.5 * LOG2E)).reshape(1, DENSE_HEAD_DIM)
    gk = g_k.reshape(1, DENSE_HEAD_DIM)
    fq, fk, fv, sq, sk, sv, sgate, cum = _even_proj(x, g_norm.reshape(1, D), w, wf, bf, gq, gk)

    t = FOX_KEYS
    cend = cum[:, :n_f, t - 1::t].reshape(B * n_f, S // t)
    qk_bound = math.sqrt(DENSE_HEAD_DIM) * jnp.max(jnp.abs(g_q)) * jnp.max(jnp.abs(g_k)) * 1.02
    thr = ((EXP_ZERO + 1.0 + 2.0 * qk_bound) * LOG2E).reshape(1).astype(F32)
    mixed_f = _fox_attn(cend, thr, fq, fk, fv, cum, sgate)

    idx = jnp.arange(SB_KEYS)
    upper = (idx[:, None] > idx[None, :]).astype(BF16)
    mixed_s = _sb_attn(sq, sk, sv, upper, sgate)
    return mixed_f, mixed_s


def _odd_layer(x, mixed_f, mixed_s, even_w_out, g_norm, w_in, g_q, g_k, w_out):
    B, S, D = x.shape
    wo_f, wo_s = even_w_out[:DENSE_W].astype(BF16), even_w_out[DENSE_W:].astype(BF16)
    w = w_in.astype(BF16)
    gq = jnp.tile(g_q * (DIL_HEAD_DIM ** -0.5 * LOG2E), N_DIL_HEADS).reshape(1, DIL_GROUP_W)
    gk = jnp.tile(g_k, N_DIL_HEADS).reshape(1, DIL_GROUP_W)
    outs = _odd_proj(x, mixed_f, mixed_s, wo_f, wo_s, g_norm.reshape(1, D), w, gq, gk)
    n_all = len(DILATIONS) * N_DIL_HEADS
    slopes = jnp.asarray([LOG2E * 2.0 ** (-8.0 * (i + 1) / n_all) for i in range(n_all)], F32)
    mixed = _dil_attn(slopes, outs[1:10], outs[10])
    return _odd_out(outs[0], mixed, w_out.astype(BF16))


def kernel(x, even_norm, even_w_in, even_b_f, even_q_gain, even_k_gain, even_w_out,
           odd_norm, odd_w_in, odd_q_gain, odd_k_gain, odd_w_out):
    assert x.shape[1] % DIL_POS == 0 and x.shape[2] == D_MODEL
    mixed_f, mixed_s = _even_mixers(x, even_norm[0], even_w_in[0], even_b_f[0], even_q_gain[0],
                                    even_k_gain[0])
    return _odd_layer(x, mixed_f, mixed_s, even_w_out[0], odd_norm[0], odd_w_in[0], odd_q_gain[0],
                      odd_k_gain[0], odd_w_out[0])
```

```python
import functools
import math

import jax
import jax.numpy as jnp
from jax import lax
from jax.experimental import pallas as pl
from jax.experimental.pallas import tpu as pltpu

F32 = jnp.float32
BF16 = jnp.bfloat16

D_MODEL = 1024
DENSE_HEAD_DIM = 128
N_DENSE_HEADS = 4
DENSE_W = N_DENSE_HEADS * DENSE_HEAD_DIM
DIL_HEAD_DIM = 64
N_DIL_HEADS = 8
DIL_GROUP_W = N_DIL_HEADS * DIL_HEAD_DIM
DILATIONS = (1, 4, 16)
DIL_SPAN = 128
DIL_BLK = 128
RMS_EPS = 1e-6

LANES = 128
PROJ_ROWS = 1024
PROJ_SUB_ROWS = 512
ODD_PROJ_ROWS = 512
ATT_ROWS = 512
FOX_KEYS = 512
SB_KEYS = 256
DIL_POS = DILATIONS[-1] * DIL_BLK
VMEM_LIMIT = 56 * 1024 * 1024

EXP_ZERO = 104.0
LOG2E = math.log2(math.e)

_NT = (((1,), (1,)), ((), ()))


def _cparams(sem):
    return pltpu.CompilerParams(dimension_semantics=sem, vmem_limit_bytes=VMEM_LIMIT)


def _resident_spec(shape):
    return pl.BlockSpec(shape, lambda *_: (0,) * len(shape), pipeline_mode=pl.Buffered(1))


def _log_sigmoid(z):
    return jnp.minimum(z, 0.0) - jnp.log(1.0 + jnp.exp(-jnp.abs(z)))


def _log2_sigmoid_pair(z2):
    soft = jnp.log2(1.0 + jnp.exp2(-jnp.abs(z2)))
    log_beta = jnp.minimum(z2, 0.0) - soft
    return log_beta, log_beta - z2


def _silu(x):
    return x * (1.0 / (1.0 + jnp.exp(-x)))


def _row_rms(x, g):
    ms = jnp.mean(x * x, axis=-1, keepdims=True)
    return x * lax.rsqrt(ms + RMS_EPS) * g


def _even_proj_kernel(x_ref, gn_ref, w_ref, wf_ref, bf_ref, gq_ref, gk_ref,
                      fq_ref, fk_ref, fv_ref, sq_ref, sk_ref, sv_ref, sg_ref, cum_ref, carry_sc):
    i = pl.program_id(1)
    rows = x_ref.shape[0]
    sub = PROJ_SUB_ROWS
    lane = lax.broadcasted_iota(jnp.int32, (8, sub), 1)

    @pl.when(i == 0)
    def _():
        carry_sc[...] = jnp.zeros_like(carry_sc)

    carry = carry_sc[:, 0:1]
    for r0 in range(0, rows, sub):
        rs = slice(r0, r0 + sub)
        h = _row_rms(x_ref[rs, :], gn_ref[...]).astype(BF16)

        fl = lax.dot_general(wf_ref[...], h, _NT, preferred_element_type=F32)[:8] + bf_ref[:, 0:1]
        lf = _log_sigmoid(fl)
        shift = 1
        while shift < sub:
            lf = lf + jnp.where(lane >= shift, pltpu.roll(lf, shift, axis=1), 0.0)
            shift *= 2
        cum = lf + carry
        cum_ref[:, rs] = cum * LOG2E
        carry = cum[:, sub - 1:sub]

        def proj(c0, width, h=h):
            return jnp.dot(h, w_ref[:, c0:c0 + width], preferred_element_type=F32)

        def head_norm(acc, g_ref, out_ref, rs=rs):
            for hh in range(N_DENSE_HEADS):
                sl = slice(hh * DENSE_HEAD_DIM, (hh + 1) * DENSE_HEAD_DIM)
                out_ref[rs, sl] = _row_rms(acc[:, sl], g_ref[...]).astype(BF16)

        head_norm(proj(0, DENSE_W), gq_ref, fq_ref)
        head_norm(proj(DENSE_W, DENSE_W), gk_ref, fk_ref)
        fv_ref[rs, :] = proj(2 * DENSE_W, DENSE_W).astype(BF16)
        sq_ref[rs, :] = (proj(3 * DENSE_W, DENSE_W) * (DENSE_HEAD_DIM ** -0.5 * LOG2E)).astype(BF16)
        sk_ref[rs, :] = proj(4 * DENSE_W, DENSE_W).astype(BF16)
        sv_ref[rs, :] = proj(5 * DENSE_W, DENSE_W).astype(BF16)
        for c in range(2):
            sl = slice(c * DENSE_W, (c + 1) * DENSE_W)
            sg_ref[rs, sl] = _silu(proj(6 * DENSE_W + c * DENSE_W, DENSE_W)).astype(BF16)
    carry_sc[...] = jnp.broadcast_to(carry, carry_sc.shape)


def _even_proj(x, g_norm, w, wf, b_f, g_q, g_k):
    B, S, D = x.shape
    tm = PROJ_ROWS
    row = lambda width: pl.BlockSpec((None, tm, width), lambda b, i: (b, i, 0))
    const = _resident_spec
    act = lambda width: jax.ShapeDtypeStruct((B, S, width), BF16)
    return pl.pallas_call(
        _even_proj_kernel,
        grid=(B, S // tm),
        in_specs=[row(D), const((1, D)), const(w.shape), const(wf.shape), const(b_f.shape),
                  const(g_q.shape), const(g_k.shape)],
        out_specs=[row(DENSE_W)] * 6 + [row(2 * DENSE_W),
                                        pl.BlockSpec((None, 8, tm), lambda b, i: (b, 0, i))],
        out_shape=[act(DENSE_W)] * 6 + [act(2 * DENSE_W), jax.ShapeDtypeStruct((B, 8, S), F32)],
        scratch_shapes=[pltpu.VMEM((8, LANES), F32)],
        compiler_params=_cparams(("arbitrary", "arbitrary")),
        name="even_proj",
    )(x, g_norm, w, wf, b_f, g_q, g_k)


def _fox_kernel(cend_ref, thr_ref, q_ref, k_ref, v_ref, c_ref, sg_ref, o_ref, m_sc, acc_sc, s_sc):
    h, i = pl.program_id(0), pl.program_id(1)
    nbatch, tq = q_ref.shape[0], q_ref.shape[1]
    tk = FOX_KEYS
    nsub = tq // tk
    batches = range(nbatch)
    c_base = [jnp.where(i > 0, cend_ref[b * N_DENSE_HEADS + h, jnp.maximum(i * nsub - 1, 0)], 0.0)
              for b in batches]

    m_sc[...] = jnp.full_like(m_sc, -jnp.inf)
    acc_sc[...] = jnp.zeros_like(acc_sc)

    def logits(b, start, width, rows):
        keys = pl.ds(pl.multiple_of(start, width), width)
        s = lax.dot_general(q_ref[b, rows, :], k_ref[b, keys, :], _NT, preferred_element_type=F32)
        return s + (c_base[b] - c_ref[b, pl.ds(h, 1), keys])

    def update(b, s, start, width, rows):
        keys = pl.ds(pl.multiple_of(start, width), width)
        m_prev = m_sc[b, rows, :]
        m_new = jnp.maximum(m_prev, jnp.max(s, axis=-1, keepdims=True))
        alpha = jnp.exp2(m_prev - m_new)
        p = jnp.exp2(s - jnp.concatenate([m_new] * (width // LANES), axis=1))
        v_ones = jnp.concatenate([v_ref[b, keys, :], jnp.ones((width, LANES), BF16)], axis=1)
        pv = jnp.dot(p.astype(BF16), v_ones, preferred_element_type=F32)
        acc_sc[b, rows, :] = jnp.concatenate([alpha, alpha], axis=1) * acc_sc[b, rows, :] + pv
        m_sc[b, rows, :] = m_new

    for dd in range(nsub):
        rows = slice(dd * tk, tq)
        for b in batches:
            s = logits(b, i * tq + dd * tk, tk, rows)
            r_io = lax.broadcasted_iota(jnp.int32, s.shape, 0)
            c_io = lax.broadcasted_iota(jnp.int32, s.shape, 1)
            update(b, jnp.where(c_io <= r_io, s, -jnp.inf), i * tq + dd * tk, tk, rows)

    def wanted(kt):
        ktc = jnp.maximum(kt, 0)
        far = c_base[0] - cend_ref[h, ktc]
        for b in batches[1:]:
            far = jnp.maximum(far, c_base[b] - cend_ref[b * N_DENSE_HEADS + h, ktc])
        return jnp.logical_and(kt >= 0, far >= -thr_ref[0])

    everything = slice(0, tq)
    kt0 = i * nsub - 1
    for b in batches:
        s_sc[b] = logits(b, jnp.maximum(kt0, 0) * tk, tk, everything)

    def body(c):
        kt, _ = c
        for b in batches:
            s = s_sc[b]
            s_sc[b] = logits(b, jnp.maximum(kt - 1, 0) * tk, tk, everything)
            update(b, s, kt * tk, tk, everything)
        return kt - 1, wanted(kt - 1)

    lax.while_loop(lambda c: c[1], body, (kt0, wanted(kt0)))
    for b in batches:
        o_ref[b] = (acc_sc[b, :, :LANES] * (1.0 / acc_sc[b, :, LANES:])
                    * sg_ref[b].astype(F32)).astype(BF16)


def _fox_attn(cend, thr, fq, fk, fv, cum, sgate):
    B, S, _ = fq.shape
    tq = ATT_ROWS
    tile = pl.BlockSpec((B, tq, DENSE_HEAD_DIM), lambda h, i: (0, i, h))
    full = pl.BlockSpec((B, S, DENSE_HEAD_DIM), lambda h, i: (0, 0, h))
    smem = pl.BlockSpec(memory_space=pltpu.SMEM)
    return pl.pallas_call(
        _fox_kernel,
        grid=(N_DENSE_HEADS, S // tq),
        in_specs=[smem, smem, tile, full, full,
                  pl.BlockSpec((B, 8, S), lambda h, i: (0, 0, 0)), tile],
        out_specs=tile,
        out_shape=jax.ShapeDtypeStruct((B, S, DENSE_W), BF16),
        scratch_shapes=[pltpu.VMEM((B, tq, LANES), F32), pltpu.VMEM((B, tq, 2 * LANES), F32),
                        pltpu.VMEM((B, tq, FOX_KEYS), F32)],
        compiler_params=_cparams(("arbitrary", "arbitrary")),
        name="fox_attn",
    )(cend, thr, fq, fk, fv, cum, sgate)


def _sb_kernel(q_ref, k_ref, v_ref, u_ref, sg_ref, o_ref, off_sc, acc_sc, z_sc):
    i = pl.program_id(1)
    nbatch, tq = q_ref.shape[0], q_ref.shape[1]
    tk = SB_KEYS
    nsub = tq // tk
    batches = range(nbatch)
    off_sc[...] = jnp.zeros_like(off_sc)
    acc_sc[...] = jnp.zeros_like(acc_sc)

    def logits(b, kt, rows):
        start = pl.multiple_of(kt * tk, tk)
        return lax.dot_general(q_ref[b, rows, :], k_ref[b, pl.ds(start, tk), :], _NT,
                               preferred_element_type=F32)

    def update(b, z, kt, rows, diagonal):
        start = pl.multiple_of(kt * tk, tk)
        log_beta, log_om = _log2_sigmoid_pair(z)
        if diagonal:
            r_io = lax.broadcasted_iota(jnp.int32, z.shape, 0)
            c_io = lax.broadcasted_iota(jnp.int32, z.shape, 1)
            strict = c_io < r_io
            log_om = jnp.where(strict, log_om, 0.0)
        off = off_sc[b, rows, :]
        later = (jnp.dot(log_om.astype(BF16), u_ref[...], preferred_element_type=F32)
                 + jnp.concatenate([off] * (tk // LANES), axis=1))
        w = jnp.exp2(log_beta + later)
        if diagonal:
            w = jnp.where(strict, w, 0.0)
        acc_sc[b, rows, :] += jnp.dot(w.astype(BF16), v_ref[b, pl.ds(start, tk), :],
                                      preferred_element_type=F32)
        off_sc[b, rows, :] = off + jnp.sum(log_om, axis=-1, keepdims=True)

    for dd in reversed(range(nsub)):
        rows = slice(dd * tk, tq)
        for b in batches:
            update(b, logits(b, i * nsub + dd, rows), i * nsub + dd, rows, True)

    everything = slice(0, tq)
    kt0 = i * nsub - 1
    for b in batches:
        z_sc[b] = logits(b, jnp.maximum(kt0, 0), everything)

    def body(c):
        kt, _ = c
        for b in batches:
            z = z_sc[b]
            z_sc[b] = logits(b, jnp.maximum(kt - 1, 0), everything)
            update(b, z, kt, everything, False)
        alive = jnp.max(off_sc[...]) >= -(EXP_ZERO + 8.0) * LOG2E
        return kt - 1, jnp.logical_and(kt >= 1, alive)

    lax.while_loop(lambda c: c[1], body, (kt0, i >= 1))
    for b in batches:
        o_ref[b] = (acc_sc[b] * sg_ref[b].astype(F32)).astype(BF16)


def _sb_attn(sq, sk, sv, upper, sgate):
    B, S, _ = sq.shape
    tq, tk = ATT_ROWS, SB_KEYS
    tile = pl.BlockSpec((B, tq, DENSE_HEAD_DIM), lambda h, i: (0, i, h))
    gate_tile = pl.BlockSpec((B, tq, DENSE_HEAD_DIM), lambda h, i: (0, i, N_DENSE_HEADS + h))
    full = pl.BlockSpec((B, S, DENSE_HEAD_DIM), lambda h, i: (0, 0, h))
    return pl.pallas_call(
        _sb_kernel,
        grid=(N_DENSE_HEADS, S // tq),
        in_specs=[tile, full, full, pl.BlockSpec((tk, tk), lambda h, i: (0, 0)), gate_tile],
        out_specs=tile,
        out_shape=jax.ShapeDtypeStruct((B, S, DENSE_W), BF16),
        scratch_shapes=[pltpu.VMEM((B, tq, LANES), F32), pltpu.VMEM((B, tq, DENSE_HEAD_DIM), F32),
                        pltpu.VMEM((B, tq, SB_KEYS), F32)],
        compiler_params=_cparams(("arbitrary", "arbitrary")),
        name="sb_attn",
    )(sq, sk, sv, upper, sgate)


def _odd_out_kernel(x_ref, m_ref, w_ref, o_ref):
    o_ref[...] = x_ref[...] + jnp.dot(m_ref[...], w_ref[...], preferred_element_type=F32)


def _odd_out(x, mixed, wo):
    B, S, D = x.shape
    tm = PROJ_ROWS
    row = lambda width: pl.BlockSpec((None, tm, width), lambda b, i: (b, i, 0))
    return pl.pallas_call(
        _odd_out_kernel,
        grid=(B, S // tm),
        in_specs=[row(D), row(DIL_GROUP_W), _resident_spec(wo.shape)],
        out_specs=row(D),
        out_shape=jax.ShapeDtypeStruct((B, S, D), F32),
        compiler_params=_cparams(("arbitrary", "arbitrary")),
        name="odd_out",
    )(x, mixed, wo)


def _odd_proj_kernel(x_ref, mf_ref, ms_ref, wof_ref, wos_ref, gn_ref, w_ref, gq_ref, gk_ref, res_ref, *rest):
    out_refs, sg_ref, h_sc = rest[:9], rest[9], rest[10]
    rows = x_ref.shape[0]
    n_slab = D_MODEL // LANES
    res = (x_ref[...]
           + jnp.dot(mf_ref[...], wof_ref[...], preferred_element_type=F32)
           + jnp.dot(ms_ref[...], wos_ref[...], preferred_element_type=F32))
    res_ref[...] = res
    hf = _row_rms(res, gn_ref[...])
    h = hf.astype(BF16)
    for cb in range(n_slab):
        h_sc[cb] = hf[:, cb * LANES:(cb + 1) * LANES]

    def by_residue(dil):
        parts = [jnp.concatenate([h_sc[cb, pl.ds(r, rows // dil, stride=dil), :] for cb in range(n_slab)],
                                 axis=1) for r in range(dil)]
        return jnp.concatenate(parts, axis=0).astype(BF16)

    lhs = [h] + [by_residue(dil) for dil in DILATIONS[1:]]

    lane_lo = lax.broadcasted_iota(jnp.int32, (rows, LANES), 1) < DIL_HEAD_DIM

    def head_norm(acc, g_ref):
        cols = []
        for cb in range(DIL_GROUP_W // LANES):
            x = acc[:, cb * LANES:(cb + 1) * LANES]
            x2 = x * x
            lo = jnp.sum(jnp.where(lane_lo, x2, 0.0), axis=-1, keepdims=True)
            hi = jnp.sum(jnp.where(lane_lo, 0.0, x2), axis=-1, keepdims=True)
            ms = jnp.where(lane_lo, lo, hi) * (1.0 / DIL_HEAD_DIM)
            cols.append(x * lax.rsqrt(ms + RMS_EPS))
        return jnp.concatenate(cols, axis=1) * g_ref[...]

    for c in range(9):
        kind, g = divmod(c, 3)
        acc = jnp.dot(lhs[g], w_ref[:, c * DIL_GROUP_W:(c + 1) * DIL_GROUP_W],
                      preferred_element_type=F32)
        if kind == 0:
            acc = head_norm(acc, gq_ref)
        elif kind == 1:
            acc = head_norm(acc, gk_ref)
        dil = DILATIONS[g]
        per = rows // dil
        for r in range(dil):
            out_refs[c][r] = acc[r * per:(r + 1) * per, :].astype(BF16)
    gate = jnp.dot(h, w_ref[:, 9 * DIL_GROUP_W:10 * DIL_GROUP_W], preferred_element_type=F32)
    sg_ref[...] = _silu(gate).astype(BF16)


def _odd_proj(x, mixed_f, mixed_s, wo_f, wo_s, g_norm, w, g_q, g_k):
    B, S, D = x.shape
    tm = ODD_PROJ_ROWS
    const = _resident_spec
    row = lambda width: pl.BlockSpec((None, tm, width), lambda b, i: (b, i, 0))
    out_specs, out_shape = [row(D)], [jax.ShapeDtypeStruct((B, S, D), F32)]
    for _ in range(3):
        for dil in DILATIONS:
            out_specs.append(pl.BlockSpec((None, dil, tm // dil, DIL_GROUP_W), lambda b, i: (b, 0, i, 0)))
            out_shape.append(jax.ShapeDtypeStruct((B, dil, S // dil, DIL_GROUP_W), BF16))
    out_specs.append(pl.BlockSpec((None, tm, DIL_GROUP_W), lambda b, i: (b, i, 0)))
    out_shape.append(jax.ShapeDtypeStruct((B, S, DIL_GROUP_W), BF16))
    return pl.pallas_call(
        _odd_proj_kernel,
        grid=(B, S // tm),
        in_specs=[row(D), row(DENSE_W), row(DENSE_W), const(wo_f.shape), const(wo_s.shape),
                  const((1, D)), const(w.shape), const(g_q.shape), const(g_k.shape)],
        out_specs=out_specs,
        out_shape=out_shape,
        scratch_shapes=[pltpu.VMEM((D // LANES, tm, LANES), F32)],
        compiler_params=_cparams(("arbitrary", "arbitrary")),
        name="odd_proj",
    )(x, mixed_f, mixed_s, wo_f, wo_s, g_norm, w, g_q, g_k)


def _dil_kernel(slope_ref, *refs):
    ins, (sg_ref, o_ref), (o_sc, m_sc, l_sc) = refs[:15], refs[15:17], refs[17:]
    hp, st = pl.program_id(1), pl.program_id(2)
    blk = DIL_BLK

    a_io = lax.broadcasted_iota(jnp.int32, (blk, 2 * blk), 0)
    c_io = lax.broadcasted_iota(jnp.int32, (blk, 2 * blk), 1)
    dist = a_io - c_io + blk
    band = jnp.logical_and(dist >= 0, dist <= DIL_SPAN)
    neg = jnp.where(band, 0.0, -jnp.inf)
    neg_first = jnp.where(jnp.logical_and(band, c_io >= blk), 0.0, -jnp.inf)
    dist_f = dist.astype(F32)
    lane_lo = lax.broadcasted_iota(jnp.int32, (blk, LANES), 1) < DIL_HEAD_DIM

    for g, dil in enumerate(DILATIONS):
        q_ref, kc_ref, kh_ref, vc_ref, vh_ref = ins[5 * g:5 * g + 5]
        nb = DIL_POS // dil // blk
        alibi = [dist_f * (slope_ref[g * N_DIL_HEADS + 2 * hp + hd] * float(dil)) for hd in range(2)]
        bias = jnp.concatenate([neg - alibi[0], neg - alibi[1]], axis=0)
        bias_first = jnp.concatenate([neg_first - alibi[0], neg_first - alibi[1]], axis=0)
        ones = jnp.ones((2 * blk, LANES), BF16)

        for r in range(dil):
            for n in range(nb):
                q2 = q_ref[r, n * blk:(n + 1) * blk, :]
                if n == 0:
                    k2 = jnp.concatenate([kh_ref[r], kc_ref[r, 0:blk, :]], axis=0)
                    v2 = jnp.concatenate([vh_ref[r], vc_ref[r, 0:blk, :]], axis=0)
                    b2 = jnp.where(st == 0, bias_first, bias)
                else:
                    k2 = kc_ref[r, (n - 1) * blk:(n + 1) * blk, :]
                    v2 = vc_ref[r, (n - 1) * blk:(n + 1) * blk, :]
                    b2 = bias
                zero = jnp.zeros_like(q2)
                q_st = jnp.concatenate([jnp.where(lane_lo, q2, zero), jnp.where(lane_lo, zero, q2)], axis=0)
                s = lax.dot_general(q_st, k2, _NT, preferred_element_type=F32) + b2
                m_st = jnp.max(s, axis=-1, keepdims=True)
                p = jnp.exp2(s - m_st)
                ol = jnp.dot(p.astype(BF16), jnp.concatenate([v2, ones], axis=1), preferred_element_type=F32)
                o = jnp.where(lane_lo, ol[:blk, :LANES], ol[blk:, :LANES])
                m = jnp.where(lane_lo, m_st[:blk], m_st[blk:])
                l = jnp.where(lane_lo, ol[:blk, LANES:], ol[blk:, LANES:])
                if dil == 1:
                    dst = pl.ds(n * blk, blk)
                else:
                    dst = pl.ds(n * blk * dil + r, blk, stride=dil)
                o_sc[g, dst, :] = o
                m_sc[g, dst, :] = m
                l_sc[g, dst, :] = l

    chunk = 256

    def merge(ci, carry):
        rows = pl.ds(pl.multiple_of(ci * chunk, chunk), chunk)
        ms = [m_sc[g, rows, :] for g in range(3)]
        m_all = jnp.maximum(jnp.maximum(ms[0], ms[1]), ms[2])
        num = jnp.zeros((chunk, LANES), F32)
        den = jnp.zeros((chunk, LANES), F32)
        for g in range(3):
            e = jnp.exp2(ms[g] - m_all)
            num = num + e * o_sc[g, rows, :]
            den = den + e * l_sc[g, rows, :]
        o_ref[rows, :] = (num / den * sg_ref[rows, :].astype(F32)).astype(BF16)
        return carry

    lax.fori_loop(0, DIL_POS // chunk, merge, 0)


def _dil_attn(slopes, qkv, sgate):
    B, S, _ = sgate.shape
    n_pairs = N_DIL_HEADS * DIL_HEAD_DIM // LANES
    blk = DIL_BLK
    in_specs = [pl.BlockSpec(memory_space=pltpu.SMEM)]
    args = [slopes]
    for g, dil in enumerate(DILATIONS):
        length = DIL_POS // dil
        nb = length // blk
        cur = pl.BlockSpec((None, dil, length, LANES), lambda b, hp, st: (b, 0, st, hp))
        halo = pl.BlockSpec((None, dil, blk, LANES),
                            lambda b, hp, st, nb=nb: (b, 0, jnp.maximum(st * nb - 1, 0), hp))
        q, k, v = qkv[g], qkv[3 + g], qkv[6 + g]
        in_specs += [cur, cur, halo, cur, halo]
        args += [q, k, k, v, v]
    pos = pl.BlockSpec((None, DIL_POS, LANES), lambda b, hp, st: (b, st, hp))
    in_specs.append(pos)
    args.append(sgate)
    return pl.pallas_call(
        _dil_kernel,
        grid=(B, n_pairs, S // DIL_POS),
        in_specs=in_specs,
        out_specs=pos,
        out_shape=jax.ShapeDtypeStruct((B, S, DIL_GROUP_W), BF16),
        scratch_shapes=[pltpu.VMEM((3, DIL_POS, LANES), F32)] * 3,
        compiler_params=_cparams(("arbitrary", "arbitrary", "arbitrary")),
        name="dil_attn",
    )(*args)


def _even_mixers(x, g_norm, w_in, b_f, g_q, g_k):
    B, S, D = x.shape
    n_f = N_DENSE_HEADS
    cut0, cut1 = 3 * DENSE_W, 3 * DENSE_W + n_f
    w = jnp.concatenate([w_in[:, :cut0], w_in[:, cut1:]], axis=1).astype(BF16)
    wf = jnp.zeros((16, D), F32).at[:n_f].set(w_in[:, cut0:cut1].T).astype(BF16)
    bf = jnp.zeros((8, LANES), F32).at[:n_f].set(jnp.broadcast_to(b_f[:, None], (n_f, LANES)))
    gq = (g_q * (DENSE_HEAD_DIM ** -0.5 * LOG2E)).reshape(1, DENSE_HEAD_DIM)
    gk = g_k.reshape(1, DENSE_HEAD_DIM)
    fq, fk, fv, sq, sk, sv, sgate, cum = _even_proj(x, g_norm.reshape(1, D), w, wf, bf, gq, gk)

    t = FOX_KEYS
    cend = cum[:, :n_f, t - 1::t].reshape(B * n_f, S // t)
    qk_bound = math.sqrt(DENSE_HEAD_DIM) * jnp.max(jnp.abs(g_q)) * jnp.max(jnp.abs(g_k)) * 1.02
    thr = ((EXP_ZERO + 1.0 + 2.0 * qk_bound) * LOG2E).reshape(1).astype(F32)
    mixed_f = _fox_attn(cend, thr, fq, fk, fv, cum, sgate)

    idx = jnp.arange(SB_KEYS)
    upper = (idx[:, None] > idx[None, :]).astype(BF16)
    mixed_s = _sb_attn(sq, sk, sv, upper, sgate)
    return mixed_f, mixed_s


def _odd_layer(x, mixed_f, mixed_s, even_w_out, g_norm, w_in, g_q, g_k, w_out):
    B, S, D = x.shape
    wo_f, wo_s = even_w_out[:DENSE_W].astype(BF16), even_w_out[DENSE_W:].astype(BF16)
    w = w_in.astype(BF16)
    gq = jnp.tile(g_q * (DIL_HEAD_DIM ** -0.5 * LOG2E), N_DIL_HEADS).reshape(1, DIL_GROUP_W)
    gk = jnp.tile(g_k, N_DIL_HEADS).reshape(1, DIL_GROUP_W)
    outs = _odd_proj(x, mixed_f, mixed_s, wo_f, wo_s, g_norm.reshape(1, D), w, gq, gk)
    n_all = len(DILATIONS) * N_DIL_HEADS
    slopes = jnp.asarray([LOG2E * 2.0 ** (-8.0 * (i + 1) / n_all) for i in range(n_all)], F32)
    mixed = _dil_attn(slopes, outs[1:10], outs[10])
    return _odd_out(outs[0], mixed, w_out.astype(BF16))


def kernel(x, even_norm, even_w_in, even_b_f, even_q_gain, even_k_gain, even_w_out,
           odd_norm, odd_w_in, odd_q_gain, odd_k_gain, odd_w_out):
    assert x.shape[1] % DIL_POS == 0 and x.shape[2] == D_MODEL
    mixed_f, mixed_s = _even_mixers(x, even_norm[0], even_w_in[0], even_b_f[0], even_q_gain[0],
                                    even_k_gain[0])
    return _odd_layer(x, mixed_f, mixed_s, even_w_out[0], odd_norm[0], odd_w_in[0], odd_q_gain[0],
                      odd_k_gain[0], odd_w_out[0])
```

```python
import functools
import math

import jax
import jax.numpy as jnp
from jax import lax
from jax.experimental import pallas as pl
from jax.experimental.pallas import tpu as pltpu

F32 = jnp.float32
BF16 = jnp.bfloat16

D_MODEL = 1024
DENSE_HEAD_DIM = 128
N_DENSE_HEADS = 4
DENSE_W = N_DENSE_HEADS * DENSE_HEAD_DIM
DIL_HEAD_DIM = 64
N_DIL_HEADS = 8
DIL_GROUP_W = N_DIL_HEADS * DIL_HEAD_DIM
DILATIONS = (1, 4, 16)
DIL_SPAN = 128
DIL_BLK = 128
RMS_EPS = 1e-6

LANES = 128
PROJ_ROWS = 1024
PROJ_SUB_ROWS = 512
ODD_PROJ_ROWS = 512
ATT_ROWS = 512
FOX_KEYS = 512
SB_KEYS = 256
SB_BLOCK = 256
DIL_POS = DILATIONS[-1] * DIL_BLK
VMEM_LIMIT = 56 * 1024 * 1024

EXP_ZERO = 104.0
LOG2E = math.log2(math.e)

_NT = (((1,), (1,)), ((), ()))


def _cparams(sem):
    return pltpu.CompilerParams(dimension_semantics=sem, vmem_limit_bytes=VMEM_LIMIT)


def _resident_spec(shape):
    return pl.BlockSpec(shape, lambda *_: (0,) * len(shape), pipeline_mode=pl.Buffered(1))


def _log_sigmoid(z):
    return jnp.minimum(z, 0.0) - jnp.log(1.0 + jnp.exp(-jnp.abs(z)))


def _log2_sigmoid_pair(z2):
    soft = jnp.log2(1.0 + jnp.exp2(-jnp.abs(z2)))
    log_beta = jnp.minimum(z2, 0.0) - soft
    return log_beta, log_beta - z2


def _silu(x):
    return x * (1.0 / (1.0 + jnp.exp(-x)))


def _row_rms(x, g):
    ms = jnp.mean(x * x, axis=-1, keepdims=True)
    return x * lax.rsqrt(ms + RMS_EPS) * g


def _even_proj_kernel(x_ref, gn_ref, wa_ref, wb_ref, wf_ref, bf_ref, gq_ref, gk_ref,
                      fq_ref, fk_ref, fv_ref, sq_ref, sk_ref, sv_ref, sg_ref, cum_ref, carry_sc):
    i = pl.program_id(1)
    rows = x_ref.shape[0]
    sub = PROJ_SUB_ROWS
    lane = lax.broadcasted_iota(jnp.int32, (8, sub), 1)

    @pl.when(i == 0)
    def _():
        carry_sc[...] = jnp.zeros_like(carry_sc)

    carry = carry_sc[:, 0:1]
    for r0 in range(0, rows, sub):
        rs = slice(r0, r0 + sub)
        h = _row_rms(x_ref[rs, :], gn_ref[...]).astype(BF16)

        fl = lax.dot_general(wf_ref[...], h, _NT, preferred_element_type=F32)[:8] + bf_ref[:, 0:1]
        lf = _log_sigmoid(fl)
        shift = 1
        while shift < sub:
            lf = lf + jnp.where(lane >= shift, pltpu.roll(lf, shift, axis=1), 0.0)
            shift *= 2
        cum = lf + carry
        cum_ref[:, rs] = cum * LOG2E
        carry = cum[:, sub - 1:sub]

        def proj(w_ref, chunk, h=h):
            return jnp.dot(h, w_ref[:, chunk * DENSE_W:(chunk + 1) * DENSE_W], preferred_element_type=F32)

        def head_norm(acc, g_ref, out_ref, rs=rs):
            for hh in range(N_DENSE_HEADS):
                sl = slice(hh * DENSE_HEAD_DIM, (hh + 1) * DENSE_HEAD_DIM)
                out_ref[rs, sl] = _row_rms(acc[:, sl], g_ref[...]).astype(BF16)

        head_norm(proj(wa_ref, 0), gq_ref, fq_ref)
        head_norm(proj(wa_ref, 1), gk_ref, fk_ref)
        fv_ref[rs, :] = proj(wa_ref, 2).astype(BF16)
        sq_ref[rs, :] = (proj(wb_ref, 0) * (DENSE_HEAD_DIM ** -0.5 * LOG2E)).astype(BF16)
        sk_ref[rs, :] = proj(wb_ref, 1).astype(BF16)
        sv_ref[rs, :] = proj(wb_ref, 2).astype(BF16)
        for c in range(2):
            sl = slice(c * DENSE_W, (c + 1) * DENSE_W)
            sg_ref[rs, sl] = _silu(proj(wb_ref, 3 + c)).astype(BF16)
    carry_sc[...] = jnp.broadcast_to(carry, carry_sc.shape)


def _even_proj(x, g_norm, wa, wb, wf, b_f, g_q, g_k):
    B, S, D = x.shape
    tm = PROJ_ROWS
    row = lambda width: pl.BlockSpec((None, tm, width), lambda b, i: (b, i, 0))
    const = _resident_spec
    act = lambda width: jax.ShapeDtypeStruct((B, S, width), BF16)
    return pl.pallas_call(
        _even_proj_kernel,
        grid=(B, S // tm),
        in_specs=[row(D), const((1, D)), const(wa.shape), const(wb.shape), const(wf.shape),
                  const(b_f.shape), const(g_q.shape), const(g_k.shape)],
        out_specs=[row(DENSE_W)] * 6 + [row(2 * DENSE_W),
                                        pl.BlockSpec((None, 8, tm), lambda b, i: (b, 0, i))],
        out_shape=[act(DENSE_W)] * 6 + [act(2 * DENSE_W), jax.ShapeDtypeStruct((B, 8, S), F32)],
        scratch_shapes=[pltpu.VMEM((8, LANES), F32)],
        compiler_params=_cparams(("arbitrary", "arbitrary")),
        name="even_proj",
    )(x, g_norm, wa, wb, wf, b_f, g_q, g_k)


def _fox_kernel(cend_ref, thr_ref, q_ref, k_ref, v_ref, c_ref, sg_ref, o_ref, m_sc, acc_sc, s_sc):
    h, i = pl.program_id(0), pl.program_id(1)
    nbatch, tq = q_ref.shape[0], q_ref.shape[1]
    tk = FOX_KEYS
    nsub = tq // tk
    batches = range(nbatch)
    c_base = [jnp.where(i > 0, cend_ref[b * N_DENSE_HEADS + h, jnp.maximum(i * nsub - 1, 0)], 0.0)
              for b in batches]

    m_sc[...] = jnp.full_like(m_sc, -jnp.inf)
    acc_sc[...] = jnp.zeros_like(acc_sc)

    def logits(b, start, width, rows):
        keys = pl.ds(pl.multiple_of(start, width), width)
        s = lax.dot_general(q_ref[b, rows, :], k_ref[b, keys, :], _NT, preferred_element_type=F32)
        return s + (c_base[b] - c_ref[b, pl.ds(h, 1), keys])

    def update(b, s, start, width, rows):
        keys = pl.ds(pl.multiple_of(start, width), width)
        m_prev = m_sc[b, rows, :]
        m_new = jnp.maximum(m_prev, jnp.max(s, axis=-1, keepdims=True))
        alpha = jnp.exp2(m_prev - m_new)
        p = jnp.exp2(s - jnp.concatenate([m_new] * (width // LANES), axis=1))
        v_ones = jnp.concatenate([v_ref[b, keys, :], jnp.ones((width, LANES), BF16)], axis=1)
        pv = jnp.dot(p.astype(BF16), v_ones, preferred_element_type=F32)
        acc_sc[b, rows, :] = jnp.concatenate([alpha, alpha], axis=1) * acc_sc[b, rows, :] + pv
        m_sc[b, rows, :] = m_new

    for dd in range(nsub):
        rows = slice(dd * tk, tq)
        for b in batches:
            s = logits(b, i * tq + dd * tk, tk, rows)
            r_io = lax.broadcasted_iota(jnp.int32, s.shape, 0)
            c_io = lax.broadcasted_iota(jnp.int32, s.shape, 1)
            update(b, jnp.where(c_io <= r_io, s, -jnp.inf), i * tq + dd * tk, tk, rows)

    def wanted(kt):
        ktc = jnp.maximum(kt, 0)
        far = c_base[0] - cend_ref[h, ktc]
        for b in batches[1:]:
            far = jnp.maximum(far, c_base[b] - cend_ref[b * N_DENSE_HEADS + h, ktc])
        return jnp.logical_and(kt >= 0, far >= -thr_ref[0])

    everything = slice(0, tq)
    kt0 = i * nsub - 1
    for b in batches:
        s_sc[b] = logits(b, jnp.maximum(kt0, 0) * tk, tk, everything)

    def body(c):
        kt, _ = c
        for b in batches:
            s = s_sc[b]
            s_sc[b] = logits(b, jnp.maximum(kt - 1, 0) * tk, tk, everything)
            update(b, s, kt * tk, tk, everything)
        return kt - 1, wanted(kt - 1)

    lax.while_loop(lambda c: c[1], body, (kt0, wanted(kt0)))
    for b in batches:
        o_ref[b] = (acc_sc[b, :, :LANES] * (1.0 / acc_sc[b, :, LANES:])
                    * sg_ref[b].astype(F32)).astype(BF16)


def _fox_attn(cend, thr, fq, fk, fv, cum, sgate):
    B, S, _ = fq.shape
    tq = ATT_ROWS
    tile = pl.BlockSpec((B, tq, DENSE_HEAD_DIM), lambda h, i: (0, i, h))
    full = pl.BlockSpec((B, S, DENSE_HEAD_DIM), lambda h, i: (0, 0, h))
    smem = pl.BlockSpec(memory_space=pltpu.SMEM)
    return pl.pallas_call(
        _fox_kernel,
        grid=(N_DENSE_HEADS, S // tq),
        in_specs=[smem, smem, tile, full, full,
                  pl.BlockSpec((B, 8, S), lambda h, i: (0, 0, 0)), tile],
        out_specs=tile,
        out_shape=jax.ShapeDtypeStruct((B, S, DENSE_W), BF16),
        scratch_shapes=[pltpu.VMEM((B, tq, LANES), F32), pltpu.VMEM((B, tq, 2 * LANES), F32),
                        pltpu.VMEM((B, tq, FOX_KEYS), F32)],
        compiler_params=_cparams(("arbitrary", "arbitrary")),
        name="fox_attn",
    )(cend, thr, fq, fk, fv, cum, sgate)


def _sb_kernel(q_ref, k_ref, v_ref, u_ref, sg_ref, o_ref, off_sc, acc_sc, z_sc):
    i = pl.program_id(1)
    nbatch, tq = q_ref.shape[0], q_ref.shape[1]
    tk = SB_KEYS
    nsub = tq // tk
    batches = range(nbatch)
    off_sc[...] = jnp.zeros_like(off_sc)
    acc_sc[...] = jnp.zeros_like(acc_sc)

    def logits(b, kt, rows):
        start = pl.multiple_of(kt * tk, tk)
        return lax.dot_general(q_ref[b, rows, :], k_ref[b, pl.ds(start, tk), :], _NT,
                               preferred_element_type=F32)

    def update(b, z, kt, rows, diagonal):
        start = pl.multiple_of(kt * tk, tk)
        log_beta, log_om = _log2_sigmoid_pair(z)
        if diagonal:
            r_io = lax.broadcasted_iota(jnp.int32, z.shape, 0)
            c_io = lax.broadcasted_iota(jnp.int32, z.shape, 1)
            strict = c_io < r_io
            log_om = jnp.where(strict, log_om, 0.0)
        off = off_sc[b, rows, :]
        later = (jnp.dot(log_om.astype(BF16), u_ref[...], preferred_element_type=F32)
                 + jnp.concatenate([off] * (tk // LANES), axis=1))
        w = jnp.exp2(log_beta + later)
        if diagonal:
            w = jnp.where(strict, w, 0.0)
        acc_sc[b, rows, :] += jnp.dot(w.astype(BF16), v_ref[b, pl.ds(start, tk), :],
                                      preferred_element_type=F32)
        off_sc[b, rows, :] = off + jnp.sum(log_om, axis=-1, keepdims=True)

    for dd in reversed(range(nsub)):
        rows = slice(dd * tk, tq)
        for b in batches:
            update(b, logits(b, i * nsub + dd, rows), i * nsub + dd, rows, True)

    everything = slice(0, tq)
    kt0 = i * nsub - 1
    for b in batches:
        z_sc[b] = logits(b, jnp.maximum(kt0, 0), everything)

    def body(c):
        kt, _ = c
        for b in batches:
            z = z_sc[b]
            z_sc[b] = logits(b, jnp.maximum(kt - 1, 0), everything)
            update(b, z, kt, everything, False)
        alive = jnp.max(off_sc[...]) >= -(EXP_ZERO + 8.0) * LOG2E
        return kt - 1, jnp.logical_and(kt >= 1, alive)

    lax.while_loop(lambda c: c[1], body, (kt0, i >= 1))
    for b in batches:
        o_ref[b] = (acc_sc[b] * sg_ref[b].astype(F32)).astype(BF16)


def _sb_blocks_kernel(q_ref, k_ref, v_ref, u_ref, sg_ref, o_ref, off_sc, acc_sc):
    i = pl.program_id(1)
    nbatch, tq = q_ref.shape[0], q_ref.shape[1]
    r = SB_BLOCK
    nblk = tq // r
    chains = [(b, rb) for b in range(nbatch) for rb in range(nblk)]
    off_sc[...] = jnp.zeros_like(off_sc)
    acc_sc[...] = jnp.zeros_like(acc_sc)
    r_io = lax.broadcasted_iota(jnp.int32, (r, r), 0)
    c_io = lax.broadcasted_iota(jnp.int32, (r, r), 1)
    strict = c_io < r_io

    def walk(j, diagonal):
        rows = [slice(rb * r, (rb + 1) * r) for _, rb in chains]
        kts = [i * nblk + rb - j for _, rb in chains]
        keys = [pl.ds(pl.multiple_of(jnp.maximum(kt, 0) * r, r), r) for kt in kts]
        zs = [lax.dot_general(q_ref[b, rows[c], :], k_ref[b, keys[c], :], _NT, preferred_element_type=F32)
              for c, (b, _) in enumerate(chains)]
        pairs, offs, laters = [], [], []
        for c, (b, _) in enumerate(chains):
            log_beta, log_om = _log2_sigmoid_pair(zs[c])
            off = off_sc[b, rows[c], :]
            if diagonal:
                log_om = jnp.where(strict, log_om, 0.0)
            else:
                off = jnp.where(kts[c] < 0, -1e30, off)
            pairs.append((log_beta, log_om))
            offs.append(off)
            laters.append(jnp.dot(log_om.astype(BF16), u_ref[...], preferred_element_type=F32)
                          + jnp.concatenate([off] * (r // LANES), axis=1))
        for c, (b, _) in enumerate(chains):
            log_beta, log_om = pairs[c]
            w = jnp.exp2(log_beta + laters[c])
            if diagonal:
                w = jnp.where(strict, w, 0.0)
            acc_sc[b, rows[c], :] += jnp.dot(w.astype(BF16), v_ref[b, keys[c], :],
                                             preferred_element_type=F32)
            off_sc[b, rows[c], :] = offs[c] + jnp.sum(log_om, axis=-1, keepdims=True)

    walk(0, True)

    def body(c):
        j, _ = c
        walk(j, False)
        alive = jnp.max(off_sc[...]) >= -(EXP_ZERO + 8.0) * LOG2E
        keys_left = i * nblk + nblk - 1 - (j + 1) >= 0
        return j + 1, jnp.logical_and(keys_left, alive)

    lax.while_loop(lambda c: c[1], body, (1, True))
    for b in range(nbatch):
        o_ref[b] = (acc_sc[b] * sg_ref[b].astype(F32)).astype(BF16)


def _sb_attn(sq, sk, sv, upper, sgate):
    B, S, _ = sq.shape
    tq, tk = ATT_ROWS, SB_BLOCK
    tile = pl.BlockSpec((B, tq, DENSE_HEAD_DIM), lambda h, i: (0, i, h))
    gate_tile = pl.BlockSpec((B, tq, DENSE_HEAD_DIM), lambda h, i: (0, i, N_DENSE_HEADS + h))
    full = pl.BlockSpec((B, S, DENSE_HEAD_DIM), lambda h, i: (0, 0, h))
    return pl.pallas_call(
        _sb_blocks_kernel,
        grid=(N_DENSE_HEADS, S // tq),
        in_specs=[tile, full, full, pl.BlockSpec((tk, tk), lambda h, i: (0, 0)), gate_tile],
        out_specs=tile,
        out_shape=jax.ShapeDtypeStruct((B, S, DENSE_W), BF16),
        scratch_shapes=[pltpu.VMEM((B, tq, LANES), F32), pltpu.VMEM((B, tq, DENSE_HEAD_DIM), F32)],
        compiler_params=_cparams(("arbitrary", "arbitrary")),
        name="sb_attn",
    )(sq, sk, sv, upper, sgate)


def _sb_attn_old(sq, sk, sv, upper, sgate):
    B, S, _ = sq.shape
    tq, tk = ATT_ROWS, SB_KEYS
    tile = pl.BlockSpec((B, tq, DENSE_HEAD_DIM), lambda h, i: (0, i, h))
    gate_tile = pl.BlockSpec((B, tq, DENSE_HEAD_DIM), lambda h, i: (0, i, N_DENSE_HEADS + h))
    full = pl.BlockSpec((B, S, DENSE_HEAD_DIM), lambda h, i: (0, 0, h))
    return pl.pallas_call(
        _sb_kernel,
        grid=(N_DENSE_HEADS, S // tq),
        in_specs=[tile, full, full, pl.BlockSpec((tk, tk), lambda h, i: (0, 0)), gate_tile],
        out_specs=tile,
        out_shape=jax.ShapeDtypeStruct((B, S, DENSE_W), BF16),
        scratch_shapes=[pltpu.VMEM((B, tq, LANES), F32), pltpu.VMEM((B, tq, DENSE_HEAD_DIM), F32),
                        pltpu.VMEM((B, tq, SB_KEYS), F32)],
        compiler_params=_cparams(("arbitrary", "arbitrary")),
        name="sb_attn",
    )(sq, sk, sv, upper, sgate)


def _odd_out_kernel(x_ref, m_ref, w_ref, o_ref):
    o_ref[...] = x_ref[...] + jnp.dot(m_ref[...], w_ref[...], preferred_element_type=F32)


def _odd_out(x, mixed, wo):
    B, S, D = x.shape
    tm = PROJ_ROWS
    row = lambda width: pl.BlockSpec((None, tm, width), lambda b, i: (b, i, 0))
    return pl.pallas_call(
        _odd_out_kernel,
        grid=(B, S // tm),
        in_specs=[row(D), row(DIL_GROUP_W), _resident_spec(wo.shape)],
        out_specs=row(D),
        out_shape=jax.ShapeDtypeStruct((B, S, D), F32),
        compiler_params=_cparams(("arbitrary", "arbitrary")),
        name="odd_out",
    )(x, mixed, wo)


def _odd_proj_kernel(x_ref, mf_ref, ms_ref, wof_ref, wos_ref, gn_ref, w_ref, gq_ref, gk_ref, res_ref, *rest):
    out_refs, sg_ref, h_sc = rest[:9], rest[9], rest[10]
    rows = x_ref.shape[0]
    n_slab = D_MODEL // LANES
    res = (x_ref[...]
           + jnp.dot(mf_ref[...], wof_ref[...], preferred_element_type=F32)
           + jnp.dot(ms_ref[...], wos_ref[...], preferred_element_type=F32))
    res_ref[...] = res
    hf = _row_rms(res, gn_ref[...])
    h = hf.astype(BF16)
    for cb in range(n_slab):
        h_sc[cb] = hf[:, cb * LANES:(cb + 1) * LANES]

    def by_residue(dil):
        parts = [jnp.concatenate([h_sc[cb, pl.ds(r, rows // dil, stride=dil), :] for cb in range(n_slab)],
                                 axis=1) for r in range(dil)]
        return jnp.concatenate(parts, axis=0).astype(BF16)

    lhs = [h] + [by_residue(dil) for dil in DILATIONS[1:]]

    lane_lo = lax.broadcasted_iota(jnp.int32, (rows, LANES), 1) < DIL_HEAD_DIM

    def head_norm(acc, g_ref):
        cols = []
        for cb in range(DIL_GROUP_W // LANES):
            x = acc[:, cb * LANES:(cb + 1) * LANES]
            x2 = x * x
            lo = jnp.sum(jnp.where(lane_lo, x2, 0.0), axis=-1, keepdims=True)
            hi = jnp.sum(jnp.where(lane_lo, 0.0, x2), axis=-1, keepdims=True)
            ms = jnp.where(lane_lo, lo, hi) * (1.0 / DIL_HEAD_DIM)
            cols.append(x * lax.rsqrt(ms + RMS_EPS))
        return jnp.concatenate(cols, axis=1) * g_ref[...]

    for c in range(9):
        kind, g = divmod(c, 3)
        acc = jnp.dot(lhs[g], w_ref[:, c * DIL_GROUP_W:(c + 1) * DIL_GROUP_W],
                      preferred_element_type=F32)
        if kind == 0:
            acc = head_norm(acc, gq_ref)
        elif kind == 1:
            acc = head_norm(acc, gk_ref)
        dil = DILATIONS[g]
        per = rows // dil
        for r in range(dil):
            out_refs[c][r] = acc[r * per:(r + 1) * per, :].astype(BF16)
    gate = jnp.dot(h, w_ref[:, 9 * DIL_GROUP_W:10 * DIL_GROUP_W], preferred_element_type=F32)
    sg_ref[...] = _silu(gate).astype(BF16)


def _odd_proj(x, mixed_f, mixed_s, wo_f, wo_s, g_norm, w, g_q, g_k):
    B, S, D = x.shape
    tm = ODD_PROJ_ROWS
    const = _resident_spec
    row = lambda width: pl.BlockSpec((None, tm, width), lambda b, i: (b, i, 0))
    out_specs, out_shape = [row(D)], [jax.ShapeDtypeStruct((B, S, D), F32)]
    for _ in range(3):
        for dil in DILATIONS:
            out_specs.append(pl.BlockSpec((None, dil, tm // dil, DIL_GROUP_W), lambda b, i: (b, 0, i, 0)))
            out_shape.append(jax.ShapeDtypeStruct((B, dil, S // dil, DIL_GROUP_W), BF16))
    out_specs.append(pl.BlockSpec((None, tm, DIL_GROUP_W), lambda b, i: (b, i, 0)))
    out_shape.append(jax.ShapeDtypeStruct((B, S, DIL_GROUP_W), BF16))
    return pl.pallas_call(
        _odd_proj_kernel,
        grid=(B, S // tm),
        in_specs=[row(D), row(DENSE_W), row(DENSE_W), const(wo_f.shape), const(wo_s.shape),
                  const((1, D)), const(w.shape), const(g_q.shape), const(g_k.shape)],
        out_specs=out_specs,
        out_shape=out_shape,
        scratch_shapes=[pltpu.VMEM((D // LANES, tm, LANES), F32)],
        compiler_params=_cparams(("arbitrary", "arbitrary")),
        name="odd_proj",
    )(x, mixed_f, mixed_s, wo_f, wo_s, g_norm, w, g_q, g_k)


def _dil_kernel(slope_ref, *refs):
    ins, (sg_ref, o_ref), (o_sc, m_sc, l_sc) = refs[:15], refs[15:17], refs[17:]
    hp, st = pl.program_id(1), pl.program_id(2)
    blk = DIL_BLK

    a_io = lax.broadcasted_iota(jnp.int32, (blk, 2 * blk), 0)
    c_io = lax.broadcasted_iota(jnp.int32, (blk, 2 * blk), 1)
    dist = a_io - c_io + blk
    band = jnp.logical_and(dist >= 0, dist <= DIL_SPAN)
    neg = jnp.where(band, 0.0, -jnp.inf)
    neg_first = jnp.where(jnp.logical_and(band, c_io >= blk), 0.0, -jnp.inf)
    dist_f = dist.astype(F32)
    lane_lo = lax.broadcasted_iota(jnp.int32, (blk, LANES), 1) < DIL_HEAD_DIM

    for g, dil in enumerate(DILATIONS):
        q_ref, kc_ref, kh_ref, vc_ref, vh_ref = ins[5 * g:5 * g + 5]
        nb = DIL_POS // dil // blk
        alibi = [dist_f * (slope_ref[g * N_DIL_HEADS + 2 * hp + hd] * float(dil)) for hd in range(2)]
        bias = jnp.concatenate([neg - alibi[0], neg - alibi[1]], axis=0)
        bias_first = jnp.concatenate([neg_first - alibi[0], neg_first - alibi[1]], axis=0)
        ones = jnp.ones((2 * blk, LANES), BF16)

        for r in range(dil):
            for n in range(nb):
                q2 = q_ref[r, n * blk:(n + 1) * blk, :]
                if n == 0:
                    k2 = jnp.concatenate([kh_ref[r], kc_ref[r, 0:blk, :]], axis=0)
                    v2 = jnp.concatenate([vh_ref[r], vc_ref[r, 0:blk, :]], axis=0)
                    b2 = jnp.where(st == 0, bias_first, bias)
                else:
                    k2 = kc_ref[r, (n - 1) * blk:(n + 1) * blk, :]
                    v2 = vc_ref[r, (n - 1) * blk:(n + 1) * blk, :]
                    b2 = bias
                zero = jnp.zeros_like(q2)
                q_st = jnp.concatenate([jnp.where(lane_lo, q2, zero), jnp.where(lane_lo, zero, q2)], axis=0)
                s = lax.dot_general(q_st, k2, _NT, preferred_element_type=F32) + b2
                m_st = jnp.max(s, axis=-1, keepdims=True)
                p = jnp.exp2(s - m_st)
                ol = jnp.dot(p.astype(BF16), jnp.concatenate([v2, ones], axis=1), preferred_element_type=F32)
                o = jnp.where(lane_lo, ol[:blk, :LANES], ol[blk:, :LANES])
                m = jnp.where(lane_lo, m_st[:blk], m_st[blk:])
                l = jnp.where(lane_lo, ol[:blk, LANES:], ol[blk:, LANES:])
                if dil == 1:
                    dst = pl.ds(n * blk, blk)
                else:
                    dst = pl.ds(n * blk * dil + r, blk, stride=dil)
                o_sc[g, dst, :] = o
                m_sc[g, dst, :] = m
                l_sc[g, dst, :] = l

    chunk = 256

    def merge(ci, carry):
        rows = pl.ds(pl.multiple_of(ci * chunk, chunk), chunk)
        ms = [m_sc[g, rows, :] for g in range(3)]
        m_all = jnp.maximum(jnp.maximum(ms[0], ms[1]), ms[2])
        num = jnp.zeros((chunk, LANES), F32)
        den = jnp.zeros((chunk, LANES), F32)
        for g in range(3):
            e = jnp.exp2(ms[g] - m_all)
            num = num + e * o_sc[g, rows, :]
            den = den + e * l_sc[g, rows, :]
        o_ref[rows, :] = (num / den * sg_ref[rows, :].astype(F32)).astype(BF16)
        return carry

    lax.fori_loop(0, DIL_POS // chunk, merge, 0)


def _dil_attn(slopes, qkv, sgate):
    B, S, _ = sgate.shape
    n_pairs = N_DIL_HEADS * DIL_HEAD_DIM // LANES
    blk = DIL_BLK
    in_specs = [pl.BlockSpec(memory_space=pltpu.SMEM)]
    args = [slopes]
    for g, dil in enumerate(DILATIONS):
        length = DIL_POS // dil
        nb = length // blk
        cur = pl.BlockSpec((None, dil, length, LANES), lambda b, hp, st: (b, 0, st, hp))
        halo = pl.BlockSpec((None, dil, blk, LANES),
                            lambda b, hp, st, nb=nb: (b, 0, jnp.maximum(st * nb - 1, 0), hp))
        q, k, v = qkv[g], qkv[3 + g], qkv[6 + g]
        in_specs += [cur, cur, halo, cur, halo]
        args += [q, k, k, v, v]
    pos = pl.BlockSpec((None, DIL_POS, LANES), lambda b, hp, st: (b, st, hp))
    in_specs.append(pos)
    args.append(sgate)
    return pl.pallas_call(
        _dil_kernel,
        grid=(B, n_pairs, S // DIL_POS),
        in_specs=in_specs,
        out_specs=pos,
        out_shape=jax.ShapeDtypeStruct((B, S, DIL_GROUP_W), BF16),
        scratch_shapes=[pltpu.VMEM((3, DIL_POS, LANES), F32)] * 3,
        compiler_params=_cparams(("arbitrary", "arbitrary", "arbitrary")),
        name="dil_attn",
    )(*args)


def _even_mixers(x, g_norm, w_in, b_f, g_q, g_k):
    B, S, D = x.shape
    n_f = N_DENSE_HEADS
    cut0, cut1 = 3 * DENSE_W, 3 * DENSE_W + n_f
    wa, wb = w_in[:, :cut0].astype(BF16), w_in[:, cut1:].astype(BF16)
    wf = jnp.zeros((16, D), F32).at[:n_f].set(w_in[:, cut0:cut1].T).astype(BF16)
    bf = jnp.zeros((8, LANES), F32).at[:n_f].set(jnp.broadcast_to(b_f[:, None], (n_f, LANES)))
    gq = (g_q * (DENSE_HEAD_DIM ** -0.5 * LOG2E)).reshape(1, DENSE_HEAD_DIM)
    gk = g_k.reshape(1, DENSE_HEAD_DIM)
    fq, fk, fv, sq, sk, sv, sgate, cum = _even_proj(x, g_norm.reshape(1, D), wa, wb, wf, bf, gq, gk)

    t = FOX_KEYS
    cend = cum[:, :n_f, t - 1::t].reshape(B * n_f, S // t)
    qk_bound = math.sqrt(DENSE_HEAD_DIM) * jnp.max(jnp.abs(g_q)) * jnp.max(jnp.abs(g_k)) * 1.02
    thr = ((EXP_ZERO + 1.0 + 2.0 * qk_bound) * LOG2E).reshape(1).astype(F32)
    mixed_f = _fox_attn(cend, thr, fq, fk, fv, cum, sgate)

    idx = jnp.arange(SB_BLOCK)
    upper = (idx[:, None] > idx[None, :]).astype(BF16)
    mixed_s = _sb_attn(sq, sk, sv, upper, sgate)
    return mixed_f, mixed_s


def _odd_layer(x, mixed_f, mixed_s, even_w_out, g_norm, w_in, g_q, g_k, w_out):
    B, S, D = x.shape
    wo_f, wo_s = even_w_out[:DENSE_W].astype(BF16), even_w_out[DENSE_W:].astype(BF16)
    w = w_in.astype(BF16)
    gq = jnp.tile(g_q * (DIL_HEAD_DIM ** -0.5 * LOG2E), N_DIL_HEADS).reshape(1, DIL_GROUP_W)
    gk = jnp.tile(g_k, N_DIL_HEADS).reshape(1, DIL_GROUP_W)
    outs = _odd_proj(x, mixed_f, mixed_s, wo_f, wo_s, g_norm.reshape(1, D), w, gq, gk)
    n_all = len(DILATIONS) * N_DIL_HEADS
    slopes = jnp.asarray([LOG2E * 2.0 ** (-8.0 * (i + 1) / n_all) for i in range(n_all)], F32)
    mixed = _dil_attn(slopes, outs[1:10], outs[10])
    return _odd_out(outs[0], mixed, w_out.astype(BF16))


def kernel(x, even_norm, even_w_in, even_b_f, even_q_gain, even_k_gain, even_w_out,
           odd_norm, odd_w_in, odd_q_gain, odd_k_gain, odd_w_out):
    assert x.shape[1] % DIL_POS == 0 and x.shape[2] == D_MODEL
    mixed_f, mixed_s = _even_mixers(x, even_norm[0], even_w_in[0], even_b_f[0], even_q_gain[0],
                                    even_k_gain[0])
    return _odd_layer(x, mixed_f, mixed_s, even_w_out[0], odd_norm[0], odd_w_in[0], odd_q_gain[0],
                      odd_k_gain[0], odd_w_out[0])
```

```python
import math

import jax
import jax.numpy as jnp
from jax import lax
from jax.experimental import pallas as pl
from jax.experimental.pallas import tpu as pltpu

F32 = jnp.float32
BF16 = jnp.bfloat16

D_MODEL = 1024
DENSE_HEAD_DIM = 128
N_DENSE_HEADS = 4
DENSE_W = N_DENSE_HEADS * DENSE_HEAD_DIM
DIL_HEAD_DIM = 64
N_DIL_HEADS = 8
DIL_GROUP_W = N_DIL_HEADS * DIL_HEAD_DIM
DILATIONS = (1, 4, 16)
DIL_SPAN = 128
DIL_BLK = 128
RMS_EPS = 1e-6

LANES = 128
PROJ_ROWS = 1024
PROJ_SUB_ROWS = 512
ODD_PROJ_ROWS = 512
ATT_ROWS = 512
FOX_KEYS = 512
SB_BLOCK = 256
DIL_POS = DILATIONS[-1] * DIL_BLK
VMEM_LIMIT = 56 * 1024 * 1024

EXP_ZERO = 104.0
LOG2E = math.log2(math.e)

_NT = (((1,), (1,)), ((), ()))


def _cparams(sem):
    return pltpu.CompilerParams(dimension_semantics=sem, vmem_limit_bytes=VMEM_LIMIT)


def _resident_spec(shape):
    return pl.BlockSpec(shape, lambda *_: (0,) * len(shape), pipeline_mode=pl.Buffered(1))


def _log_sigmoid(z):
    return jnp.minimum(z, 0.0) - jnp.log(1.0 + jnp.exp(-jnp.abs(z)))


def _log2_sigmoid_pair(z2):
    soft = jnp.log2(1.0 + jnp.exp2(-jnp.abs(z2)))
    log_beta = jnp.minimum(z2, 0.0) - soft
    return log_beta, log_beta - z2


def _silu(x):
    return x * (1.0 / (1.0 + jnp.exp(-x)))


def _row_rms(x, g):
    ms = jnp.mean(x * x, axis=-1, keepdims=True)
    return x * lax.rsqrt(ms + RMS_EPS) * g


def _even_proj_kernel(x_ref, gn_ref, wa_ref, wb_ref, wf_ref, bf_ref, gq_ref, gk_ref,
                      fq_ref, fk_ref, fv_ref, sq_ref, sk_ref, sv_ref, sg_ref, cum_ref, carry_sc):
    i = pl.program_id(1)
    rows = x_ref.shape[0]
    sub = PROJ_SUB_ROWS
    lane = lax.broadcasted_iota(jnp.int32, (8, sub), 1)

    @pl.when(i == 0)
    def _():
        carry_sc[...] = jnp.zeros_like(carry_sc)

    carry = carry_sc[:, 0:1]
    for r0 in range(0, rows, sub):
        rs = slice(r0, r0 + sub)
        h = _row_rms(x_ref[rs, :], gn_ref[...]).astype(BF16)

        fl = lax.dot_general(wf_ref[...], h, _NT, preferred_element_type=F32)[:8] + bf_ref[:, 0:1]
        lf = _log_sigmoid(fl)
        shift = 1
        while shift < sub:
            lf = lf + jnp.where(lane >= shift, pltpu.roll(lf, shift, axis=1), 0.0)
            shift *= 2
        cum = lf + carry
        cum_ref[:, rs] = cum * LOG2E
        carry = cum[:, sub - 1:sub]

        def proj(w_ref, chunk, h=h):
            return jnp.dot(h, w_ref[:, chunk * DENSE_W:(chunk + 1) * DENSE_W], preferred_element_type=F32)

        def head_norm(acc, g_ref, out_ref, rs=rs):
            for hh in range(N_DENSE_HEADS):
                sl = slice(hh * DENSE_HEAD_DIM, (hh + 1) * DENSE_HEAD_DIM)
                out_ref[rs, sl] = _row_rms(acc[:, sl], g_ref[...]).astype(BF16)

        head_norm(proj(wa_ref, 0), gq_ref, fq_ref)
        head_norm(proj(wa_ref, 1), gk_ref, fk_ref)
        for c in range(2):
            sl = slice(c * DENSE_W, (c + 1) * DENSE_W)
            sg_ref[rs, sl] = _silu(proj(wb_ref, 3 + c)).astype(BF16)
        sq_ref[rs, :] = (proj(wb_ref, 0) * (DENSE_HEAD_DIM ** -0.5 * LOG2E)).astype(BF16)
        fv_ref[rs, :] = proj(wa_ref, 2).astype(BF16)
        sk_ref[rs, :] = proj(wb_ref, 1).astype(BF16)
        sv_ref[rs, :] = proj(wb_ref, 2).astype(BF16)
    carry_sc[...] = jnp.broadcast_to(carry, carry_sc.shape)


def _even_proj(x, g_norm, wa, wb, wf, b_f, g_q, g_k):
    B, S, D = x.shape
    tm = PROJ_ROWS
    row = lambda width: pl.BlockSpec((None, tm, width), lambda b, i: (b, i, 0))
    const = _resident_spec
    act = lambda width: jax.ShapeDtypeStruct((B, S, width), BF16)
    return pl.pallas_call(
        _even_proj_kernel,
        grid=(B, S // tm),
        in_specs=[row(D), const((1, D)), const(wa.shape), const(wb.shape), const(wf.shape),
                  const(b_f.shape), const(g_q.shape), const(g_k.shape)],
        out_specs=[row(DENSE_W)] * 6 + [row(2 * DENSE_W),
                                        pl.BlockSpec((None, 8, tm), lambda b, i: (b, 0, i))],
        out_shape=[act(DENSE_W)] * 6 + [act(2 * DENSE_W), jax.ShapeDtypeStruct((B, 8, S), F32)],
        scratch_shapes=[pltpu.VMEM((8, LANES), F32)],
        compiler_params=_cparams(("arbitrary", "arbitrary")),
        name="even_proj",
    )(x, g_norm, wa, wb, wf, b_f, g_q, g_k)


def _fox_kernel(cend_ref, thr_ref, q_ref, k_ref, v_ref, c_ref, sg_ref, o_ref, m_sc, acc_sc, s_sc):
    h, i = pl.program_id(0), pl.program_id(1)
    nbatch, tq = q_ref.shape[0], q_ref.shape[1]
    tk = FOX_KEYS
    nsub = tq // tk
    batches = range(nbatch)
    c_base = [jnp.where(i > 0, cend_ref[b * N_DENSE_HEADS + h, jnp.maximum(i * nsub - 1, 0)], 0.0)
              for b in batches]

    m_sc[...] = jnp.full_like(m_sc, -jnp.inf)
    acc_sc[...] = jnp.zeros_like(acc_sc)

    def logits(b, start, width, rows):
        keys = pl.ds(pl.multiple_of(start, width), width)
        s = lax.dot_general(q_ref[b, rows, :], k_ref[b, keys, :], _NT, preferred_element_type=F32)
        return s + (c_base[b] - c_ref[b, pl.ds(h, 1), keys])

    def update(b, s, start, width, rows):
        keys = pl.ds(pl.multiple_of(start, width), width)
        m_prev = m_sc[b, rows, :]
        m_new = jnp.maximum(m_prev, jnp.max(s, axis=-1, keepdims=True))
        alpha = jnp.exp2(m_prev - m_new)
        p = jnp.exp2(s - jnp.concatenate([m_new] * (width // LANES), axis=1))
        v_ones = jnp.concatenate([v_ref[b, keys, :], jnp.ones((width, LANES), BF16)], axis=1)
        pv = jnp.dot(p.astype(BF16), v_ones, preferred_element_type=F32)
        acc_sc[b, rows, :] = jnp.concatenate([alpha, alpha], axis=1) * acc_sc[b, rows, :] + pv
        m_sc[b, rows, :] = m_new

    for dd in range(nsub):
        rows = slice(dd * tk, tq)
        for b in batches:
            s = logits(b, i * tq + dd * tk, tk, rows)
            r_io = lax.broadcasted_iota(jnp.int32, s.shape, 0)
            c_io = lax.broadcasted_iota(jnp.int32, s.shape, 1)
            update(b, jnp.where(c_io <= r_io, s, -jnp.inf), i * tq + dd * tk, tk, rows)

    def wanted(kt):
        ktc = jnp.maximum(kt, 0)
        far = c_base[0] - cend_ref[h, ktc]
        for b in batches[1:]:
            far = jnp.maximum(far, c_base[b] - cend_ref[b * N_DENSE_HEADS + h, ktc])
        return jnp.logical_and(kt >= 0, far >= -thr_ref[0])

    everything = slice(0, tq)
    kt0 = i * nsub - 1
    for b in batches:
        s_sc[b] = logits(b, jnp.maximum(kt0, 0) * tk, tk, everything)

    def body(c):
        kt, _ = c
        for b in batches:
            s = s_sc[b]
            s_sc[b] = logits(b, jnp.maximum(kt - 1, 0) * tk, tk, everything)
            update(b, s, kt * tk, tk, everything)
        return kt - 1, wanted(kt - 1)

    lax.while_loop(lambda c: c[1], body, (kt0, wanted(kt0)))
    for b in batches:
        o_ref[b] = (acc_sc[b, :, :LANES] * (1.0 / acc_sc[b, :, LANES:])
                    * sg_ref[b].astype(F32)).astype(BF16)


def _fox_attn(cend, thr, fq, fk, fv, cum, sgate):
    B, S, _ = fq.shape
    tq = ATT_ROWS
    tile = pl.BlockSpec((B, tq, DENSE_HEAD_DIM), lambda h, i: (0, i, h))
    full = pl.BlockSpec((B, S, DENSE_HEAD_DIM), lambda h, i: (0, 0, h))
    smem = pl.BlockSpec(memory_space=pltpu.SMEM)
    return pl.pallas_call(
        _fox_kernel,
        grid=(N_DENSE_HEADS, S // tq),
        in_specs=[smem, smem, tile, full, full,
                  pl.BlockSpec((B, 8, S), lambda h, i: (0, 0, 0)), tile],
        out_specs=tile,
        out_shape=jax.ShapeDtypeStruct((B, S, DENSE_W), BF16),
        scratch_shapes=[pltpu.VMEM((B, tq, LANES), F32), pltpu.VMEM((B, tq, 2 * LANES), F32),
                        pltpu.VMEM((B, tq, FOX_KEYS), F32)],
        compiler_params=_cparams(("arbitrary", "arbitrary")),
        name="fox_attn",
    )(cend, thr, fq, fk, fv, cum, sgate)


def _sb_kernel(q_ref, k_ref, v_ref, u_ref, sg_ref, o_ref, off_sc, acc_sc):
    i = pl.program_id(1)
    nbatch, tq = q_ref.shape[0], q_ref.shape[1]
    r = SB_BLOCK
    nblk = tq // r
    chains = [(b, rb) for b in range(nbatch) for rb in range(nblk)]
    off_sc[...] = jnp.zeros_like(off_sc)
    acc_sc[...] = jnp.zeros_like(acc_sc)
    r_io = lax.broadcasted_iota(jnp.int32, (r, r), 0)
    c_io = lax.broadcasted_iota(jnp.int32, (r, r), 1)
    strict = c_io < r_io

    def walk(j, diagonal):
        rows = [slice(rb * r, (rb + 1) * r) for _, rb in chains]
        kts = [i * nblk + rb - j for _, rb in chains]
        keys = [pl.ds(pl.multiple_of(jnp.maximum(kt, 0) * r, r), r) for kt in kts]
        zs = [lax.dot_general(q_ref[b, rows[c], :], k_ref[b, keys[c], :], _NT, preferred_element_type=F32)
              for c, (b, _) in enumerate(chains)]
        pairs, offs, laters = [], [], []
        for c, (b, _) in enumerate(chains):
            log_beta, log_om = _log2_sigmoid_pair(zs[c])
            off = off_sc[b, rows[c], :]
            if diagonal:
                log_om = jnp.where(strict, log_om, 0.0)
            else:
                off = jnp.where(kts[c] < 0, -1e30, off)
            pairs.append((log_beta, log_om))
            offs.append(off)
            laters.append(jnp.dot(log_om.astype(BF16), u_ref[...], preferred_element_type=F32)
                          + jnp.concatenate([off] * (r // LANES), axis=1))
        for c, (b, _) in enumerate(chains):
            log_beta, log_om = pairs[c]
            w = jnp.exp2(log_beta + laters[c])
            if diagonal:
                w = jnp.where(strict, w, 0.0)
            acc_sc[b, rows[c], :] += jnp.dot(w.astype(BF16), v_ref[b, keys[c], :],
                                             preferred_element_type=F32)
            off_sc[b, rows[c], :] = offs[c] + jnp.sum(log_om, axis=-1, keepdims=True)

    walk(0, True)

    def body(c):
        j, _ = c
        walk(j, False)
        alive = jnp.max(off_sc[...]) >= -(EXP_ZERO + 8.0) * LOG2E
        keys_left = i * nblk + nblk - 1 - (j + 1) >= 0
        return j + 1, jnp.logical_and(keys_left, alive)

    lax.while_loop(lambda c: c[1], body, (1, True))
    for b in range(nbatch):
        o_ref[b] = (acc_sc[b] * sg_ref[b].astype(F32)).astype(BF16)


def _sb_attn(sq, sk, sv, upper, sgate):
    B, S, _ = sq.shape
    tq, tk = ATT_ROWS, SB_BLOCK
    tile = pl.BlockSpec((B, tq, DENSE_HEAD_DIM), lambda h, i: (0, i, h))
    gate_tile = pl.BlockSpec((B, tq, DENSE_HEAD_DIM), lambda h, i: (0, i, N_DENSE_HEADS + h))
    full = pl.BlockSpec((B, S, DENSE_HEAD_DIM), lambda h, i: (0, 0, h))
    return pl.pallas_call(
        _sb_kernel,
        grid=(N_DENSE_HEADS, S // tq),
        in_specs=[tile, full, full, _resident_spec((tk, tk)), gate_tile],
        out_specs=tile,
        out_shape=jax.ShapeDtypeStruct((B, S, DENSE_W), BF16),
        scratch_shapes=[pltpu.VMEM((B, tq, LANES), F32), pltpu.VMEM((B, tq, DENSE_HEAD_DIM), F32)],
        compiler_params=_cparams(("arbitrary", "arbitrary")),
        name="sb_attn",
    )(sq, sk, sv, upper, sgate)


def _odd_out_kernel(x_ref, m_ref, w_ref, o_ref):
    o_ref[...] = x_ref[...] + jnp.dot(m_ref[...], w_ref[...], preferred_element_type=F32)


def _odd_out(x, mixed, wo):
    B, S, D = x.shape
    tm = PROJ_ROWS
    row = lambda width: pl.BlockSpec((None, tm, width), lambda b, i: (b, i, 0))
    return pl.pallas_call(
        _odd_out_kernel,
        grid=(B, S // tm),
        in_specs=[row(D), row(DIL_GROUP_W), _resident_spec(wo.shape)],
        out_specs=row(D),
        out_shape=jax.ShapeDtypeStruct((B, S, D), F32),
        compiler_params=_cparams(("arbitrary", "arbitrary")),
        name="odd_out",
    )(x, mixed, wo)


def _odd_proj_kernel(x_ref, mf_ref, ms_ref, wof_ref, wos_ref, gn_ref, w_ref, gq_ref, gk_ref, res_ref, *rest):
    out_refs, sg_ref, h_sc = rest[:9], rest[9], rest[10]
    rows = x_ref.shape[0]
    n_slab = D_MODEL // LANES
    res = (x_ref[...]
           + jnp.dot(mf_ref[...], wof_ref[...], preferred_element_type=F32)
           + jnp.dot(ms_ref[...], wos_ref[...], preferred_element_type=F32))
    res_ref[...] = res
    hf = _row_rms(res, gn_ref[...])
    h = hf.astype(BF16)
    for cb in range(n_slab):
        h_sc[cb] = hf[:, cb * LANES:(cb + 1) * LANES]

    def by_residue(dil):
        parts = [jnp.concatenate([h_sc[cb, pl.ds(r, rows // dil, stride=dil), :] for cb in range(n_slab)],
                                 axis=1) for r in range(dil)]
        return jnp.concatenate(parts, axis=0).astype(BF16)

    lhs = [h] + [by_residue(dil) for dil in DILATIONS[1:]]

    lane_lo = lax.broadcasted_iota(jnp.int32, (rows, LANES), 1) < DIL_HEAD_DIM

    def head_norm(acc, g_ref):
        cols = []
        for cb in range(DIL_GROUP_W // LANES):
            x = acc[:, cb * LANES:(cb + 1) * LANES]
            x2 = x * x
            lo = jnp.sum(jnp.where(lane_lo, x2, 0.0), axis=-1, keepdims=True)
            hi = jnp.sum(jnp.where(lane_lo, 0.0, x2), axis=-1, keepdims=True)
            ms = jnp.where(lane_lo, lo, hi) * (1.0 / DIL_HEAD_DIM)
            cols.append(x * lax.rsqrt(ms + RMS_EPS))
        return jnp.concatenate(cols, axis=1) * g_ref[...]

    gate = jnp.dot(h, w_ref[:, 9 * DIL_GROUP_W:10 * DIL_GROUP_W], preferred_element_type=F32)
    sg_ref[...] = _silu(gate).astype(BF16)
    for c in sorted(range(9), key=lambda c: (c % 3, c)):
        kind, g = divmod(c, 3)
        acc = jnp.dot(lhs[g], w_ref[:, c * DIL_GROUP_W:(c + 1) * DIL_GROUP_W],
                      preferred_element_type=F32)
        if kind == 0:
            acc = head_norm(acc, gq_ref)
        elif kind == 1:
            acc = head_norm(acc, gk_ref)
        dil = DILATIONS[g]
        per = rows // dil
        for r in range(dil):
            out_refs[c][r] = acc[r * per:(r + 1) * per, :].astype(BF16)


def _odd_proj(x, mixed_f, mixed_s, wo_f, wo_s, g_norm, w, g_q, g_k):
    B, S, D = x.shape
    tm = ODD_PROJ_ROWS
    const = _resident_spec
    row = lambda width: pl.BlockSpec((None, tm, width), lambda b, i: (b, i, 0))
    out_specs, out_shape = [row(D)], [jax.ShapeDtypeStruct((B, S, D), F32)]
    for _ in range(3):
        for dil in DILATIONS:
            out_specs.append(pl.BlockSpec((None, dil, tm // dil, DIL_GROUP_W), lambda b, i: (b, 0, i, 0)))
            out_shape.append(jax.ShapeDtypeStruct((B, dil, S // dil, DIL_GROUP_W), BF16))
    out_specs.append(pl.BlockSpec((None, tm, DIL_GROUP_W), lambda b, i: (b, i, 0)))
    out_shape.append(jax.ShapeDtypeStruct((B, S, DIL_GROUP_W), BF16))
    return pl.pallas_call(
        _odd_proj_kernel,
        grid=(B, S // tm),
        in_specs=[row(D), row(DENSE_W), row(DENSE_W), const(wo_f.shape), const(wo_s.shape),
                  const((1, D)), const(w.shape), const(g_q.shape), const(g_k.shape)],
        out_specs=out_specs,
        out_shape=out_shape,
        scratch_shapes=[pltpu.VMEM((D // LANES, tm, LANES), F32)],
        compiler_params=_cparams(("arbitrary", "arbitrary")),
        name="odd_proj",
    )(x, mixed_f, mixed_s, wo_f, wo_s, g_norm, w, g_q, g_k)


def _dil_kernel(slope_ref, *refs):
    ins, (sg_ref, o_ref), (o_sc, m_sc, l_sc) = refs[:15], refs[15:17], refs[17:]
    hp, st = pl.program_id(1), pl.program_id(2)
    blk = DIL_BLK

    a_io = lax.broadcasted_iota(jnp.int32, (blk, 2 * blk), 0)
    c_io = lax.broadcasted_iota(jnp.int32, (blk, 2 * blk), 1)
    dist = a_io - c_io + blk
    band = jnp.logical_and(dist >= 0, dist <= DIL_SPAN)
    neg = jnp.where(band, 0.0, -jnp.inf)
    neg_first = jnp.where(jnp.logical_and(band, c_io >= blk), 0.0, -jnp.inf)
    dist_f = dist.astype(F32)
    lane_lo = lax.broadcasted_iota(jnp.int32, (blk, LANES), 1) < DIL_HEAD_DIM

    for g, dil in enumerate(DILATIONS):
        q_ref, kc_ref, kh_ref, vc_ref, vh_ref = ins[5 * g:5 * g + 5]
        nb = DIL_POS // dil // blk
        alibi = [dist_f * (slope_ref[g * N_DIL_HEADS + 2 * hp + hd] * float(dil)) for hd in range(2)]
        bias = jnp.concatenate([neg - alibi[0], neg - alibi[1]], axis=0)
        bias_first = jnp.concatenate([neg_first - alibi[0], neg_first - alibi[1]], axis=0)
        ones = jnp.ones((2 * blk, LANES), BF16)

        for r in range(dil):
            for n in range(nb):
                q2 = q_ref[r, n * blk:(n + 1) * blk, :]
                if n == 0:
                    k2 = jnp.concatenate([kh_ref[r], kc_ref[r, 0:blk, :]], axis=0)
                    v2 = jnp.concatenate([vh_ref[r], vc_ref[r, 0:blk, :]], axis=0)
                    b2 = jnp.where(st == 0, bias_first, bias)
                else:
                    k2 = kc_ref[r, (n - 1) * blk:(n + 1) * blk, :]
                    v2 = vc_ref[r, (n - 1) * blk:(n + 1) * blk, :]
                    b2 = bias
                zero = jnp.zeros_like(q2)
                q_st = jnp.concatenate([jnp.where(lane_lo, q2, zero), jnp.where(lane_lo, zero, q2)], axis=0)
                s = lax.dot_general(q_st, k2, _NT, preferred_element_type=F32) + b2
                m_st = jnp.max(s, axis=-1, keepdims=True)
                p = jnp.exp2(s - m_st)
                ol = jnp.dot(p.astype(BF16), jnp.concatenate([v2, ones], axis=1), preferred_element_type=F32)
                o = jnp.where(lane_lo, ol[:blk, :LANES], ol[blk:, :LANES])
                m = jnp.where(lane_lo, m_st[:blk], m_st[blk:])
                l = jnp.where(lane_lo, ol[:blk, LANES:], ol[blk:, LANES:])
                if dil == 1:
                    dst = pl.ds(n * blk, blk)
                else:
                    dst = pl.ds(n * blk * dil + r, blk, stride=dil)
                o_sc[g, dst, :] = o
                m_sc[g, dst, :] = m
                l_sc[g, dst, :] = l

    chunk = 256

    def merge(ci, carry):
        rows = pl.ds(pl.multiple_of(ci * chunk, chunk), chunk)
        ms = [m_sc[g, rows, :] for g in range(3)]
        m_all = jnp.maximum(jnp.maximum(ms[0], ms[1]), ms[2])
        num = jnp.zeros((chunk, LANES), F32)
        den = jnp.zeros((chunk, LANES), F32)
        for g in range(3):
            e = jnp.exp2(ms[g] - m_all)
            num = num + e * o_sc[g, rows, :]
            den = den + e * l_sc[g, rows, :]
        o_ref[rows, :] = (num / den * sg_ref[rows, :].astype(F32)).astype(BF16)
        return carry

    lax.fori_loop(0, DIL_POS // chunk, merge, 0)


def _dil_attn(slopes, qkv, sgate):
    B, S, _ = sgate.shape
    n_pairs = N_DIL_HEADS * DIL_HEAD_DIM // LANES
    blk = DIL_BLK
    in_specs = [pl.BlockSpec(memory_space=pltpu.SMEM)]
    args = [slopes]
    for g, dil in enumerate(DILATIONS):
        length = DIL_POS // dil
        nb = length // blk
        cur = pl.BlockSpec((None, dil, length, LANES), lambda b, hp, st: (b, 0, st, hp))
        halo = pl.BlockSpec((None, dil, blk, LANES),
                            lambda b, hp, st, nb=nb: (b, 0, jnp.maximum(st * nb - 1, 0), hp))
        q, k, v = qkv[g], qkv[3 + g], qkv[6 + g]
        in_specs += [cur, cur, halo, cur, halo]
        args += [q, k, k, v, v]
    pos = pl.BlockSpec((None, DIL_POS, LANES), lambda b, hp, st: (b, st, hp))
    in_specs.append(pos)
    args.append(sgate)
    return pl.pallas_call(
        _dil_kernel,
        grid=(B, n_pairs, S // DIL_POS),
        in_specs=in_specs,
        out_specs=pos,
        out_shape=jax.ShapeDtypeStruct((B, S, DIL_GROUP_W), BF16),
        scratch_shapes=[pltpu.VMEM((3, DIL_POS, LANES), F32)] * 3,
        compiler_params=_cparams(("arbitrary", "arbitrary", "arbitrary")),
        name="dil_attn",
    )(*args)


def _even_mixers(x, g_norm, w_in, b_f, g_q, g_k):
    B, S, D = x.shape
    n_f = N_DENSE_HEADS
    cut0, cut1 = 3 * DENSE_W, 3 * DENSE_W + n_f
    wa, wb = w_in[:, :cut0].astype(BF16), w_in[:, cut1:].astype(BF16)
    wf = jnp.zeros((16, D), F32).at[:n_f].set(w_in[:, cut0:cut1].T).astype(BF16)
    bf = jnp.zeros((8, LANES), F32).at[:n_f].set(jnp.broadcast_to(b_f[:, None], (n_f, LANES)))
    gq = (g_q * (DENSE_HEAD_DIM ** -0.5 * LOG2E)).reshape(1, DENSE_HEAD_DIM)
    gk = g_k.reshape(1, DENSE_HEAD_DIM)
    fq, fk, fv, sq, sk, sv, sgate, cum = _even_proj(x, g_norm.reshape(1, D), wa, wb, wf, bf, gq, gk)

    t = FOX_KEYS
    cend = cum[:, :n_f, t - 1::t].reshape(B * n_f, S // t)
    qk_bound = math.sqrt(DENSE_HEAD_DIM) * jnp.max(jnp.abs(g_q)) * jnp.max(jnp.abs(g_k)) * 1.02
    thr = ((EXP_ZERO + 1.0 + 2.0 * qk_bound) * LOG2E).reshape(1).astype(F32)
    mixed_f = _fox_attn(cend, thr, fq, fk, fv, cum, sgate)

    idx = jnp.arange(SB_BLOCK)
    upper = (idx[:, None] > idx[None, :]).astype(BF16)
    mixed_s = _sb_attn(sq, sk, sv, upper, sgate)
    return mixed_f, mixed_s


def _odd_layer(x, mixed_f, mixed_s, even_w_out, g_norm, w_in, g_q, g_k, w_out):
    B, S, D = x.shape
    wo_f, wo_s = even_w_out[:DENSE_W].astype(BF16), even_w_out[DENSE_W:].astype(BF16)
    w = w_in.astype(BF16)
    gq = jnp.tile(g_q * (DIL_HEAD_DIM ** -0.5 * LOG2E), N_DIL_HEADS).reshape(1, DIL_GROUP_W)
    gk = jnp.tile(g_k, N_DIL_HEADS).reshape(1, DIL_GROUP_W)
    outs = _odd_proj(x, mixed_f, mixed_s, wo_f, wo_s, g_norm.reshape(1, D), w, gq, gk)
    n_all = len(DILATIONS) * N_DIL_HEADS
    slopes = jnp.asarray([LOG2E * 2.0 ** (-8.0 * (i + 1) / n_all) for i in range(n_all)], F32)
    mixed = _dil_attn(slopes, outs[1:10], outs[10])
    return _odd_out(outs[0], mixed, w_out.astype(BF16))


def kernel(x, even_norm, even_w_in, even_b_f, even_q_gain, even_k_gain, even_w_out,
           odd_norm, odd_w_in, odd_q_gain, odd_k_gain, odd_w_out):
    assert x.shape[1] % DIL_POS == 0 and x.shape[2] == D_MODEL
    mixed_f, mixed_s = _even_mixers(x, even_norm[0], even_w_in[0], even_b_f[0], even_q_gain[0],
                                    even_k_gain[0])
    return _odd_layer(x, mixed_f, mixed_s, even_w_out[0], odd_norm[0], odd_w_in[0], odd_q_gain[0],
                      odd_k_gain[0], odd_w_out[0])
```

```python
import math

import jax
import jax.numpy as jnp
from jax import lax
from jax.experimental import pallas as pl
from jax.experimental.pallas import tpu as pltpu

F32 = jnp.float32
BF16 = jnp.bfloat16

D_MODEL = 1024
DENSE_HEAD_DIM = 128
N_DENSE_HEADS = 4
DENSE_W = N_DENSE_HEADS * DENSE_HEAD_DIM
DIL_HEAD_DIM = 64
N_DIL_HEADS = 8
DIL_GROUP_W = N_DIL_HEADS * DIL_HEAD_DIM
DILATIONS = (1, 4, 16)
DIL_SPAN = 128
DIL_BLK = 128
RMS_EPS = 1e-6

LANES = 128
PROJ_ROWS = 1024
PROJ_SUB_ROWS = 512
ODD_PROJ_ROWS = 512
ATT_ROWS = 512
FOX_KEYS = 512
SB_BLOCK = 256
DIL_POS = DILATIONS[-1] * DIL_BLK
VMEM_LIMIT = 56 * 1024 * 1024

EXP_ZERO = 104.0
LOG2E = math.log2(math.e)

_NT = (((1,), (1,)), ((), ()))


def _cparams(sem):
    return pltpu.CompilerParams(dimension_semantics=sem, vmem_limit_bytes=VMEM_LIMIT)


def _resident_spec(shape):
    return pl.BlockSpec(shape, lambda *_: (0,) * len(shape), pipeline_mode=pl.Buffered(1))


def _log_sigmoid(z):
    return jnp.minimum(z, 0.0) - jnp.log(1.0 + jnp.exp(-jnp.abs(z)))


def _log2_sigmoid_pair(z2):
    soft = jnp.log2(1.0 + jnp.exp2(-jnp.abs(z2)))
    log_beta = jnp.minimum(z2, 0.0) - soft
    return log_beta, log_beta - z2


def _silu(x):
    return x * (1.0 / (1.0 + jnp.exp(-x)))


def _row_rms(x, g):
    ms = jnp.mean(x * x, axis=-1, keepdims=True)
    return x * lax.rsqrt(ms + RMS_EPS) * g


def _even_proj_kernel(x_ref, gn_ref, wa_ref, wb_ref, wf_ref, bf_ref, gq_ref, gk_ref,
                      fq_ref, fk_ref, fv_ref, sq_ref, sk_ref, sv_ref, sg_ref, cum_ref, carry_sc):
    i = pl.program_id(1)
    rows = x_ref.shape[0]
    sub = PROJ_SUB_ROWS
    lane = lax.broadcasted_iota(jnp.int32, (8, sub), 1)

    @pl.when(i == 0)
    def _():
        carry_sc[...] = jnp.zeros_like(carry_sc)

    carry = carry_sc[:, 0:1]
    for r0 in range(0, rows, sub):
        rs = slice(r0, r0 + sub)
        h = _row_rms(x_ref[rs, :], gn_ref[...]).astype(BF16)

        fl = lax.dot_general(wf_ref[...], h, _NT, preferred_element_type=F32)[:8] + bf_ref[:, 0:1]
        lf = _log_sigmoid(fl)
        shift = 1
        while shift < sub:
            lf = lf + jnp.where(lane >= shift, pltpu.roll(lf, shift, axis=1), 0.0)
            shift *= 2
        cum = lf + carry
        cum_ref[:, rs] = cum * LOG2E
        carry = cum[:, sub - 1:sub]

        def proj(w_ref, chunk, h=h):
            return jnp.dot(h, w_ref[:, chunk * DENSE_W:(chunk + 1) * DENSE_W], preferred_element_type=F32)

        def head_norm(acc, g_ref, out_ref, rs=rs):
            for hh in range(N_DENSE_HEADS):
                sl = slice(hh * DENSE_HEAD_DIM, (hh + 1) * DENSE_HEAD_DIM)
                out_ref[rs, sl] = _row_rms(acc[:, sl], g_ref[...]).astype(BF16)

        head_norm(proj(wa_ref, 0), gq_ref, fq_ref)
        head_norm(proj(wa_ref, 1), gk_ref, fk_ref)
        for c in range(2):
            sl = slice(c * DENSE_W, (c + 1) * DENSE_W)
            sg_ref[rs, sl] = _silu(proj(wb_ref, 3 + c)).astype(BF16)
        sq_ref[rs, :] = (proj(wb_ref, 0) * (DENSE_HEAD_DIM ** -0.5 * LOG2E)).astype(BF16)
        fv_ref[rs, :] = proj(wa_ref, 2).astype(BF16)
        sk_ref[rs, :] = proj(wb_ref, 1).astype(BF16)
        sv_ref[rs, :] = proj(wb_ref, 2).astype(BF16)
    carry_sc[...] = jnp.broadcast_to(carry, carry_sc.shape)


def _even_proj(x, g_norm, wa, wb, wf, b_f, g_q, g_k):
    B, S, D = x.shape
    tm = PROJ_ROWS
    row = lambda width: pl.BlockSpec((None, tm, width), lambda b, i: (b, i, 0))
    const = _resident_spec
    act = lambda width: jax.ShapeDtypeStruct((B, S, width), BF16)
    return pl.pallas_call(
        _even_proj_kernel,
        grid=(B, S // tm),
        in_specs=[row(D), const((1, D)), const(wa.shape), const(wb.shape), const(wf.shape),
                  const(b_f.shape), const(g_q.shape), const(g_k.shape)],
        out_specs=[row(DENSE_W)] * 6 + [row(2 * DENSE_W),
                                        pl.BlockSpec((None, 8, tm), lambda b, i: (b, 0, i))],
        out_shape=[act(DENSE_W)] * 6 + [act(2 * DENSE_W), jax.ShapeDtypeStruct((B, 8, S), F32)],
        scratch_shapes=[pltpu.VMEM((8, LANES), F32)],
        compiler_params=_cparams(("arbitrary", "arbitrary")),
        name="even_proj",
    )(x, g_norm, wa, wb, wf, b_f, g_q, g_k)


def _fox_kernel(cend_ref, thr_ref, q_ref, k_ref, v_ref, c_ref, sg_ref, o_ref, m_sc, acc_sc, s_sc):
    h, i = pl.program_id(0), pl.program_id(1)
    nbatch, tq = q_ref.shape[0], q_ref.shape[1]
    tk = FOX_KEYS
    nsub = tq // tk
    batches = range(nbatch)
    c_base = [jnp.where(i > 0, cend_ref[b * N_DENSE_HEADS + h, jnp.maximum(i * nsub - 1, 0)], 0.0)
              for b in batches]

    m_sc[...] = jnp.full_like(m_sc, -jnp.inf)
    acc_sc[...] = jnp.zeros_like(acc_sc)

    def logits(b, start, width, rows):
        keys = pl.ds(pl.multiple_of(start, width), width)
        s = lax.dot_general(q_ref[b, rows, :], k_ref[b, keys, :], _NT, preferred_element_type=F32)
        return s + (c_base[b] - c_ref[b, pl.ds(h, 1), keys])

    def update(b, s, start, width, rows):
        keys = pl.ds(pl.multiple_of(start, width), width)
        m_prev = m_sc[b, rows, :]
        m_new = jnp.maximum(m_prev, jnp.max(s, axis=-1, keepdims=True))
        alpha = jnp.exp2(m_prev - m_new)
        p = jnp.exp2(s - jnp.concatenate([m_new] * (width // LANES), axis=1))
        v_ones = jnp.concatenate([v_ref[b, keys, :], jnp.ones((width, LANES), BF16)], axis=1)
        pv = jnp.dot(p.astype(BF16), v_ones, preferred_element_type=F32)
        acc_sc[b, rows, :] = jnp.concatenate([alpha, alpha], axis=1) * acc_sc[b, rows, :] + pv
        m_sc[b, rows, :] = m_new

    for dd in range(nsub):
        rows = slice(dd * tk, tq)
        for b in batches:
            s = logits(b, i * tq + dd * tk, tk, rows)
            r_io = lax.broadcasted_iota(jnp.int32, s.shape, 0)
            c_io = lax.broadcasted_iota(jnp.int32, s.shape, 1)
            update(b, jnp.where(c_io <= r_io, s, -jnp.inf), i * tq + dd * tk, tk, rows)

    reach = thr_ref[0] - jnp.min(m_sc[...])

    def wanted(kt):
        ktc = jnp.maximum(kt, 0)
        far = c_base[0] - cend_ref[h, ktc]
        for b in batches[1:]:
            far = jnp.maximum(far, c_base[b] - cend_ref[b * N_DENSE_HEADS + h, ktc])
        return jnp.logical_and(kt >= 0, far >= -reach)

    everything = slice(0, tq)
    kt0 = i * nsub - 1
    for b in batches:
        s_sc[b] = logits(b, jnp.maximum(kt0, 0) * tk, tk, everything)

    def body(c):
        kt, _ = c
        for b in batches:
            s = s_sc[b]
            s_sc[b] = logits(b, jnp.maximum(kt - 1, 0) * tk, tk, everything)
            update(b, s, kt * tk, tk, everything)
        return kt - 1, wanted(kt - 1)

    lax.while_loop(lambda c: c[1], body, (kt0, wanted(kt0)))
    for b in batches:
        o_ref[b] = (acc_sc[b, :, :LANES] * (1.0 / acc_sc[b, :, LANES:])
                    * sg_ref[b].astype(F32)).astype(BF16)


def _fox_attn(cend, thr, fq, fk, fv, cum, sgate):
    B, S, _ = fq.shape
    tq = ATT_ROWS
    tile = pl.BlockSpec((B, tq, DENSE_HEAD_DIM), lambda h, i: (0, i, h))
    full = pl.BlockSpec((B, S, DENSE_HEAD_DIM), lambda h, i: (0, 0, h))
    smem = pl.BlockSpec(memory_space=pltpu.SMEM)
    return pl.pallas_call(
        _fox_kernel,
        grid=(N_DENSE_HEADS, S // tq),
        in_specs=[smem, smem, tile, full, full,
                  pl.BlockSpec((B, 8, S), lambda h, i: (0, 0, 0)), tile],
        out_specs=tile,
        out_shape=jax.ShapeDtypeStruct((B, S, DENSE_W), BF16),
        scratch_shapes=[pltpu.VMEM((B, tq, LANES), F32), pltpu.VMEM((B, tq, 2 * LANES), F32),
                        pltpu.VMEM((B, tq, FOX_KEYS), F32)],
        compiler_params=_cparams(("arbitrary", "arbitrary")),
        name="fox_attn",
    )(cend, thr, fq, fk, fv, cum, sgate)


def _sb_kernel(q_ref, k_ref, v_ref, u_ref, sg_ref, o_ref, off_sc, acc_sc):
    i = pl.program_id(1)
    nbatch, tq = q_ref.shape[0], q_ref.shape[1]
    r = SB_BLOCK
    nblk = tq // r
    chains = [(b, rb) for b in range(nbatch) for rb in range(nblk)]
    off_sc[...] = jnp.zeros_like(off_sc)
    acc_sc[...] = jnp.zeros_like(acc_sc)
    r_io = lax.broadcasted_iota(jnp.int32, (r, r), 0)
    c_io = lax.broadcasted_iota(jnp.int32, (r, r), 1)
    strict = c_io < r_io

    def walk(j, diagonal):
        rows = [slice(rb * r, (rb + 1) * r) for _, rb in chains]
        kts = [i * nblk + rb - j for _, rb in chains]
        keys = [pl.ds(pl.multiple_of(jnp.maximum(kt, 0) * r, r), r) for kt in kts]
        zs = [lax.dot_general(q_ref[b, rows[c], :], k_ref[b, keys[c], :], _NT, preferred_element_type=F32)
              for c, (b, _) in enumerate(chains)]
        pairs, offs, laters = [], [], []
        for c, (b, _) in enumerate(chains):
            log_beta, log_om = _log2_sigmoid_pair(zs[c])
            off = off_sc[b, rows[c], :]
            if diagonal:
                log_om = jnp.where(strict, log_om, 0.0)
            else:
                off = jnp.where(kts[c] < 0, -1e30, off)
            pairs.append((log_beta, log_om))
            offs.append(off)
            laters.append(jnp.dot(log_om.astype(BF16), u_ref[...], preferred_element_type=F32)
                          + jnp.concatenate([off] * (r // LANES), axis=1))
        for c, (b, _) in enumerate(chains):
            log_beta, log_om = pairs[c]
            w = jnp.exp2(log_beta + laters[c])
            if diagonal:
                w = jnp.where(strict, w, 0.0)
            acc_sc[b, rows[c], :] += jnp.dot(w.astype(BF16), v_ref[b, keys[c], :],
                                             preferred_element_type=F32)
            off_sc[b, rows[c], :] = offs[c] + jnp.sum(log_om, axis=-1, keepdims=True)

    walk(0, True)

    def body(c):
        j, _ = c
        walk(j, False)
        alive = jnp.max(off_sc[...]) >= -(EXP_ZERO + 8.0) * LOG2E
        keys_left = i * nblk + nblk - 1 - (j + 1) >= 0
        return j + 1, jnp.logical_and(keys_left, alive)

    lax.while_loop(lambda c: c[1], body, (1, True))
    for b in range(nbatch):
        o_ref[b] = (acc_sc[b] * sg_ref[b].astype(F32)).astype(BF16)


def _sb_attn(sq, sk, sv, upper, sgate):
    B, S, _ = sq.shape
    tq, tk = ATT_ROWS, SB_BLOCK
    tile = pl.BlockSpec((B, tq, DENSE_HEAD_DIM), lambda h, i: (0, i, h))
    gate_tile = pl.BlockSpec((B, tq, DENSE_HEAD_DIM), lambda h, i: (0, i, N_DENSE_HEADS + h))
    full = pl.BlockSpec((B, S, DENSE_HEAD_DIM), lambda h, i: (0, 0, h))
    return pl.pallas_call(
        _sb_kernel,
        grid=(N_DENSE_HEADS, S // tq),
        in_specs=[tile, full, full, _resident_spec((tk, tk)), gate_tile],
        out_specs=tile,
        out_shape=jax.ShapeDtypeStruct((B, S, DENSE_W), BF16),
        scratch_shapes=[pltpu.VMEM((B, tq, LANES), F32), pltpu.VMEM((B, tq, DENSE_HEAD_DIM), F32)],
        compiler_params=_cparams(("arbitrary", "arbitrary")),
        name="sb_attn",
    )(sq, sk, sv, upper, sgate)


def _odd_out_kernel(x_ref, m_ref, w_ref, o_ref):
    o_ref[...] = x_ref[...] + jnp.dot(m_ref[...], w_ref[...], preferred_element_type=F32)


def _odd_out(x, mixed, wo):
    B, S, D = x.shape
    tm = PROJ_ROWS
    row = lambda width: pl.BlockSpec((None, tm, width), lambda b, i: (b, i, 0))
    return pl.pallas_call(
        _odd_out_kernel,
        grid=(B, S // tm),
        in_specs=[row(D), row(DIL_GROUP_W), _resident_spec(wo.shape)],
        out_specs=row(D),
        out_shape=jax.ShapeDtypeStruct((B, S, D), F32),
        compiler_params=_cparams(("arbitrary", "arbitrary")),
        name="odd_out",
    )(x, mixed, wo)


def _odd_proj_kernel(x_ref, mf_ref, ms_ref, wof_ref, wos_ref, gn_ref, w_ref, gq_ref, gk_ref, res_ref, *rest):
    out_refs, sg_ref, h_sc = rest[:9], rest[9], rest[10]
    rows = x_ref.shape[0]
    n_slab = D_MODEL // LANES
    res = (x_ref[...]
           + jnp.dot(mf_ref[...], wof_ref[...], preferred_element_type=F32)
           + jnp.dot(ms_ref[...], wos_ref[...], preferred_element_type=F32))
    res_ref[...] = res
    hf = _row_rms(res, gn_ref[...])
    h = hf.astype(BF16)
    for cb in range(n_slab):
        h_sc[cb] = hf[:, cb * LANES:(cb + 1) * LANES]

    def by_residue(dil):
        parts = [jnp.concatenate([h_sc[cb, pl.ds(r, rows // dil, stride=dil), :] for cb in range(n_slab)],
                                 axis=1) for r in range(dil)]
        return jnp.concatenate(parts, axis=0).astype(BF16)

    lhs = [h] + [by_residue(dil) for dil in DILATIONS[1:]]

    lane_lo = lax.broadcasted_iota(jnp.int32, (rows, LANES), 1) < DIL_HEAD_DIM

    def head_norm(acc, g_ref):
        cols = []
        for cb in range(DIL_GROUP_W // LANES):
            x = acc[:, cb * LANES:(cb + 1) * LANES]
            x2 = x * x
            lo = jnp.sum(jnp.where(lane_lo, x2, 0.0), axis=-1, keepdims=True)
            hi = jnp.sum(jnp.where(lane_lo, 0.0, x2), axis=-1, keepdims=True)
            ms = jnp.where(lane_lo, lo, hi) * (1.0 / DIL_HEAD_DIM)
            cols.append(x * lax.rsqrt(ms + RMS_EPS))
        return jnp.concatenate(cols, axis=1) * g_ref[...]

    gate = jnp.dot(h, w_ref[:, 9 * DIL_GROUP_W:10 * DIL_GROUP_W], preferred_element_type=F32)
    sg_ref[...] = _silu(gate).astype(BF16)
    for c in sorted(range(9), key=lambda c: (c % 3, c)):
        kind, g = divmod(c, 3)
        acc = jnp.dot(lhs[g], w_ref[:, c * DIL_GROUP_W:(c + 1) * DIL_GROUP_W],
                      preferred_element_type=F32)
        if kind == 0:
            acc = head_norm(acc, gq_ref)
        elif kind == 1:
            acc = head_norm(acc, gk_ref)
        dil = DILATIONS[g]
        per = rows // dil
        for r in range(dil):
            out_refs[c][r] = acc[r * per:(r + 1) * per, :].astype(BF16)


def _odd_proj(x, mixed_f, mixed_s, wo_f, wo_s, g_norm, w, g_q, g_k):
    B, S, D = x.shape
    tm = ODD_PROJ_ROWS
    const = _resident_spec
    row = lambda width: pl.BlockSpec((None, tm, width), lambda b, i: (b, i, 0))
    out_specs, out_shape = [row(D)], [jax.ShapeDtypeStruct((B, S, D), F32)]
    for _ in range(3):
        for dil in DILATIONS:
            out_specs.append(pl.BlockSpec((None, dil, tm // dil, DIL_GROUP_W), lambda b, i: (b, 0, i, 0)))
            out_shape.append(jax.ShapeDtypeStruct((B, dil, S // dil, DIL_GROUP_W), BF16))
    out_specs.append(pl.BlockSpec((None, tm, DIL_GROUP_W), lambda b, i: (b, i, 0)))
    out_shape.append(jax.ShapeDtypeStruct((B, S, DIL_GROUP_W), BF16))
    return pl.pallas_call(
        _odd_proj_kernel,
        grid=(B, S // tm),
        in_specs=[row(D), row(DENSE_W), row(DENSE_W), const(wo_f.shape), const(wo_s.shape),
                  const((1, D)), const(w.shape), const(g_q.shape), const(g_k.shape)],
        out_specs=out_specs,
        out_shape=out_shape,
        scratch_shapes=[pltpu.VMEM((D // LANES, tm, LANES), F32)],
        compiler_params=_cparams(("arbitrary", "arbitrary")),
        name="odd_proj",
    )(x, mixed_f, mixed_s, wo_f, wo_s, g_norm, w, g_q, g_k)


def _dil_kernel(slope_ref, *refs):
    ins, (sg_ref, o_ref), (o_sc, m_sc, l_sc) = refs[:15], refs[15:17], refs[17:]
    hp, st = pl.program_id(1), pl.program_id(2)
    blk = DIL_BLK

    a_io = lax.broadcasted_iota(jnp.int32, (blk, 2 * blk), 0)
    c_io = lax.broadcasted_iota(jnp.int32, (blk, 2 * blk), 1)
    dist = a_io - c_io + blk
    band = jnp.logical_and(dist >= 0, dist <= DIL_SPAN)
    neg = jnp.where(band, 0.0, -jnp.inf)
    neg_first = jnp.where(jnp.logical_and(band, c_io >= blk), 0.0, -jnp.inf)
    dist_f = dist.astype(F32)
    lane_lo = lax.broadcasted_iota(jnp.int32, (blk, LANES), 1) < DIL_HEAD_DIM

    for g, dil in enumerate(DILATIONS):
        q_ref, kc_ref, kh_ref, vc_ref, vh_ref = ins[5 * g:5 * g + 5]
        nb = DIL_POS // dil // blk
        alibi = [dist_f * (slope_ref[g * N_DIL_HEADS + 2 * hp + hd] * float(dil)) for hd in range(2)]
        bias = jnp.concatenate([neg - alibi[0], neg - alibi[1]], axis=0)
        bias_first = jnp.concatenate([neg_first - alibi[0], neg_first - alibi[1]], axis=0)
        ones = jnp.ones((2 * blk, LANES), BF16)

        for r in range(dil):
            for n in range(nb):
                q2 = q_ref[r, n * blk:(n + 1) * blk, :]
                if n == 0:
                    k2 = jnp.concatenate([kh_ref[r], kc_ref[r, 0:blk, :]], axis=0)
                    v2 = jnp.concatenate([vh_ref[r], vc_ref[r, 0:blk, :]], axis=0)
                    b2 = jnp.where(st == 0, bias_first, bias)
                else:
                    k2 = kc_ref[r, (n - 1) * blk:(n + 1) * blk, :]
                    v2 = vc_ref[r, (n - 1) * blk:(n + 1) * blk, :]
                    b2 = bias
                zero = jnp.zeros_like(q2)
                q_st = jnp.concatenate([jnp.where(lane_lo, q2, zero), jnp.where(lane_lo, zero, q2)], axis=0)
                s = lax.dot_general(q_st, k2, _NT, preferred_element_type=F32) + b2
                m_st = jnp.max(s, axis=-1, keepdims=True)
                p = jnp.exp2(s - m_st)
                ol = jnp.dot(p.astype(BF16), jnp.concatenate([v2, ones], axis=1), preferred_element_type=F32)
                o = jnp.where(lane_lo, ol[:blk, :LANES], ol[blk:, :LANES])
                m = jnp.where(lane_lo, m_st[:blk], m_st[blk:])
                l = jnp.where(lane_lo, ol[:blk, LANES:], ol[blk:, LANES:])
                if dil == 1:
                    dst = pl.ds(n * blk, blk)
                else:
                    dst = pl.ds(n * blk * dil + r, blk, stride=dil)
                o_sc[g, dst, :] = o
                m_sc[g, dst, :] = m
                l_sc[g, dst, :] = l

    chunk = 256

    def merge(ci, carry):
        rows = pl.ds(pl.multiple_of(ci * chunk, chunk), chunk)
        ms = [m_sc[g, rows, :] for g in range(3)]
        m_all = jnp.maximum(jnp.maximum(ms[0], ms[1]), ms[2])
        num = jnp.zeros((chunk, LANES), F32)
        den = jnp.zeros((chunk, LANES), F32)
        for g in range(3):
            e = jnp.exp2(ms[g] - m_all)
            num = num + e * o_sc[g, rows, :]
            den = den + e * l_sc[g, rows, :]
        o_ref[rows, :] = (num / den * sg_ref[rows, :].astype(F32)).astype(BF16)
        return carry

    lax.fori_loop(0, DIL_POS // chunk, merge, 0)


def _dil_attn(slopes, qkv, sgate):
    B, S, _ = sgate.shape
    n_pairs = N_DIL_HEADS * DIL_HEAD_DIM // LANES
    blk = DIL_BLK
    in_specs = [pl.BlockSpec(memory_space=pltpu.SMEM)]
    args = [slopes]
    for g, dil in enumerate(DILATIONS):
        length = DIL_POS // dil
        nb = length // blk
        cur = pl.BlockSpec((None, dil, length, LANES), lambda b, hp, st: (b, 0, st, hp))
        halo = pl.BlockSpec((None, dil, blk, LANES),
                            lambda b, hp, st, nb=nb: (b, 0, jnp.maximum(st * nb - 1, 0), hp))
        q, k, v = qkv[g], qkv[3 + g], qkv[6 + g]
        in_specs += [cur, cur, halo, cur, halo]
        args += [q, k, k, v, v]
    pos = pl.BlockSpec((None, DIL_POS, LANES), lambda b, hp, st: (b, st, hp))
    in_specs.append(pos)
    args.append(sgate)
    return pl.pallas_call(
        _dil_kernel,
        grid=(B, n_pairs, S // DIL_POS),
        in_specs=in_specs,
        out_specs=pos,
        out_shape=jax.ShapeDtypeStruct((B, S, DIL_GROUP_W), BF16),
        scratch_shapes=[pltpu.VMEM((3, DIL_POS, LANES), F32)] * 3,
        compiler_params=_cparams(("arbitrary", "arbitrary", "arbitrary")),
        name="dil_attn",
    )(*args)


def _even_mixers(x, g_norm, w_in, b_f, g_q, g_k):
    B, S, D = x.shape
    n_f = N_DENSE_HEADS
    cut0, cut1 = 3 * DENSE_W, 3 * DENSE_W + n_f
    wa, wb = w_in[:, :cut0].astype(BF16), w_in[:, cut1:].astype(BF16)
    wf = jnp.zeros((16, D), F32).at[:n_f].set(w_in[:, cut0:cut1].T).astype(BF16)
    bf = jnp.zeros((8, LANES), F32).at[:n_f].set(jnp.broadcast_to(b_f[:, None], (n_f, LANES)))
    gq = (g_q * (DENSE_HEAD_DIM ** -0.5 * LOG2E)).reshape(1, DENSE_HEAD_DIM)
    gk = g_k.reshape(1, DENSE_HEAD_DIM)
    fq, fk, fv, sq, sk, sv, sgate, cum = _even_proj(x, g_norm.reshape(1, D), wa, wb, wf, bf, gq, gk)

    t = FOX_KEYS
    cend = cum[:, :n_f, t - 1::t].reshape(B * n_f, S // t)
    qk_bound = math.sqrt(DENSE_HEAD_DIM) * jnp.max(jnp.abs(g_q)) * jnp.max(jnp.abs(g_k)) * 1.02
    thr = ((EXP_ZERO + 1.0 + qk_bound) * LOG2E).reshape(1).astype(F32)
    mixed_f = _fox_attn(cend, thr, fq, fk, fv, cum, sgate)

    idx = jnp.arange(SB_BLOCK)
    upper = (idx[:, None] > idx[None, :]).astype(BF16)
    mixed_s = _sb_attn(sq, sk, sv, upper, sgate)
    return mixed_f, mixed_s


def _odd_layer(x, mixed_f, mixed_s, even_w_out, g_norm, w_in, g_q, g_k, w_out):
    B, S, D = x.shape
    wo_f, wo_s = even_w_out[:DENSE_W].astype(BF16), even_w_out[DENSE_W:].astype(BF16)
    w = w_in.astype(BF16)
    gq = jnp.tile(g_q * (DIL_HEAD_DIM ** -0.5 * LOG2E), N_DIL_HEADS).reshape(1, DIL_GROUP_W)
    gk = jnp.tile(g_k, N_DIL_HEADS).reshape(1, DIL_GROUP_W)
    outs = _odd_proj(x, mixed_f, mixed_s, wo_f, wo_s, g_norm.reshape(1, D), w, gq, gk)
    n_all = len(DILATIONS) * N_DIL_HEADS
    slopes = jnp.asarray([LOG2E * 2.0 ** (-8.0 * (i + 1) / n_all) for i in range(n_all)], F32)
    mixed = _dil_attn(slopes, outs[1:10], outs[10])
    return _odd_out(outs[0], mixed, w_out.astype(BF16))


def kernel(x, even_norm, even_w_in, even_b_f, even_q_gain, even_k_gain, even_w_out,
           odd_norm, odd_w_in, odd_q_gain, odd_k_gain, odd_w_out):
    assert x.shape[1] % DIL_POS == 0 and x.shape[2] == D_MODEL
    mixed_f, mixed_s = _even_mixers(x, even_norm[0], even_w_in[0], even_b_f[0], even_q_gain[0],
                                    even_k_gain[0])
    return _odd_layer(x, mixed_f, mixed_s, even_w_out[0], odd_norm[0], odd_w_in[0], odd_q_gain[0],
                      odd_k_gain[0], odd_w_out[0])
```

```python
import math

import jax
import jax.numpy as jnp
from jax import lax
from jax.experimental import pallas as pl
from jax.experimental.pallas import tpu as pltpu

F32 = jnp.float32
BF16 = jnp.bfloat16

D_MODEL = 1024
DENSE_HEAD_DIM = 128
N_DENSE_HEADS = 4
DENSE_W = N_DENSE_HEADS * DENSE_HEAD_DIM
DIL_HEAD_DIM = 64
N_DIL_HEADS = 8
DIL_GROUP_W = N_DIL_HEADS * DIL_HEAD_DIM
DILATIONS = (1, 4, 16)
DIL_SPAN = 128
DIL_BLK = 128
RMS_EPS = 1e-6

LANES = 128
SUBLANES = 8
BF16_SUBLANES = 16
PROJ_ROWS = 1024
PROJ_SUB_ROWS = 512
ODD_PROJ_ROWS = 512
ATT_ROWS = 512
FOX_KEYS = 512
SB_BLOCK = 256
DIL_POS = DILATIONS[-1] * DIL_BLK
VMEM_LIMIT = 56 * 1024 * 1024

EXP_ZERO = 104.0
LOG2E = math.log2(math.e)

_NT = (((1,), (1,)), ((), ()))


def _cparams(sem):
    return pltpu.CompilerParams(dimension_semantics=sem, vmem_limit_bytes=VMEM_LIMIT)


def _resident_spec(shape):
    return pl.BlockSpec(shape, lambda *_: (0,) * len(shape), pipeline_mode=pl.Buffered(1))


def _log_sigmoid(z):
    return jnp.minimum(z, 0.0) - jnp.log(1.0 + jnp.exp(-jnp.abs(z)))


def _log2_sigmoid_pair(z2):
    soft = jnp.log2(1.0 + jnp.exp2(-jnp.abs(z2)))
    log_beta = jnp.minimum(z2, 0.0) - soft
    return log_beta, log_beta - z2


def _silu(x):
    return x * (1.0 / (1.0 + jnp.exp(-x)))


def _row_rms(x, g):
    ms = jnp.mean(x * x, axis=-1, keepdims=True)
    return x * lax.rsqrt(ms + RMS_EPS) * g


def _even_proj_kernel(x_ref, gn_ref, wa_ref, wb_ref, wf_ref, bf_ref, gq_ref, gk_ref,
                      fq_ref, fk_ref, fv_ref, sq_ref, sk_ref, sv_ref, sg_ref, cum_ref, carry_sc):
    i = pl.program_id(1)
    rows = x_ref.shape[0]
    sub = PROJ_SUB_ROWS
    lane = lax.broadcasted_iota(jnp.int32, (SUBLANES, sub), 1)

    @pl.when(i == 0)
    def _():
        carry_sc[...] = jnp.zeros_like(carry_sc)

    carry = carry_sc[:, 0:1]
    for r0 in range(0, rows, sub):
        rs = slice(r0, r0 + sub)
        h = _row_rms(x_ref[rs, :], gn_ref[...]).astype(BF16)

        fl = lax.dot_general(wf_ref[...], h, _NT, preferred_element_type=F32)[:SUBLANES] + bf_ref[:, 0:1]
        lf = _log_sigmoid(fl)
        shift = 1
        while shift < sub:
            lf = lf + jnp.where(lane >= shift, pltpu.roll(lf, shift, axis=1), 0.0)
            shift *= 2
        cum = lf + carry
        cum_ref[:, rs] = cum * LOG2E
        carry = cum[:, sub - 1:sub]

        def proj(w_ref, chunk, h=h):
            return jnp.dot(h, w_ref[:, chunk * DENSE_W:(chunk + 1) * DENSE_W], preferred_element_type=F32)

        def head_norm(acc, g_ref, out_ref, rs=rs):
            for hh in range(N_DENSE_HEADS):
                sl = slice(hh * DENSE_HEAD_DIM, (hh + 1) * DENSE_HEAD_DIM)
                out_ref[rs, sl] = _row_rms(acc[:, sl], g_ref[...]).astype(BF16)

        head_norm(proj(wa_ref, 0), gq_ref, fq_ref)
        head_norm(proj(wa_ref, 1), gk_ref, fk_ref)
        for c in range(2):
            sl = slice(c * DENSE_W, (c + 1) * DENSE_W)
            sg_ref[rs, sl] = _silu(proj(wb_ref, 3 + c)).astype(BF16)
        sq_ref[rs, :] = (proj(wb_ref, 0) * (DENSE_HEAD_DIM ** -0.5 * LOG2E)).astype(BF16)
        fv_ref[rs, :] = proj(wa_ref, 2).astype(BF16)
        sk_ref[rs, :] = proj(wb_ref, 1).astype(BF16)
        sv_ref[rs, :] = proj(wb_ref, 2).astype(BF16)
    carry_sc[...] = jnp.broadcast_to(carry, carry_sc.shape)


def _even_proj(x, g_norm, wa, wb, wf, b_f, g_q, g_k):
    B, S, D = x.shape
    tm = PROJ_ROWS
    row = lambda width: pl.BlockSpec((None, tm, width), lambda b, i: (b, i, 0))
    const = _resident_spec
    act = lambda width: jax.ShapeDtypeStruct((B, S, width), BF16)
    return pl.pallas_call(
        _even_proj_kernel,
        grid=(B, S // tm),
        in_specs=[row(D), const((1, D)), const(wa.shape), const(wb.shape), const(wf.shape),
                  const(b_f.shape), const(g_q.shape), const(g_k.shape)],
        out_specs=[row(DENSE_W)] * 6 + [row(2 * DENSE_W),
                                        pl.BlockSpec((None, SUBLANES, tm), lambda b, i: (b, 0, i))],
        out_shape=[act(DENSE_W)] * 6 + [act(2 * DENSE_W), jax.ShapeDtypeStruct((B, SUBLANES, S), F32)],
        scratch_shapes=[pltpu.VMEM((SUBLANES, LANES), F32)],
        compiler_params=_cparams(("arbitrary", "arbitrary")),
        name="even_proj",
    )(x, g_norm, wa, wb, wf, b_f, g_q, g_k)


def _fox_kernel(cend_ref, thr_ref, q_ref, k_ref, v_ref, c_ref, sg_ref, o_ref, m_sc, acc_sc, s_sc):
    h, i = pl.program_id(0), pl.program_id(1)
    nbatch, tq = q_ref.shape[0], q_ref.shape[1]
    tk = FOX_KEYS
    nsub = tq // tk
    batches = range(nbatch)
    c_base = [jnp.where(i > 0, cend_ref[b * N_DENSE_HEADS + h, jnp.maximum(i * nsub - 1, 0)], 0.0)
              for b in batches]

    def logits(b, start, width, rows):
        keys = pl.ds(pl.multiple_of(start, width), width)
        s = lax.dot_general(q_ref[b, rows, :], k_ref[b, keys, :], _NT, preferred_element_type=F32)
        return s + (c_base[b] - c_ref[b, pl.ds(h, 1), keys])

    def update(b, s, start, width, rows, first=False):
        keys = pl.ds(pl.multiple_of(start, width), width)
        m_cur = jnp.broadcast_to(jnp.max(s, axis=-1, keepdims=True), (s.shape[0], LANES))
        m_new = m_cur if first else jnp.maximum(m_sc[b, rows, :], m_cur)
        p = jnp.exp2(s - jnp.concatenate([m_new] * (width // LANES), axis=1))
        v_ones = jnp.concatenate([v_ref[b, keys, :], jnp.ones((width, LANES), BF16)], axis=1)
        pv = jnp.dot(p.astype(BF16), v_ones, preferred_element_type=F32)
        if first:
            acc_sc[b, rows, :] = pv
        else:
            alpha = jnp.exp2(m_sc[b, rows, :] - m_new)
            acc_sc[b, rows, :] = jnp.concatenate([alpha, alpha], axis=1) * acc_sc[b, rows, :] + pv
        m_sc[b, rows, :] = m_new

    for dd in range(nsub):
        rows = slice(dd * tk, tq)
        for b in batches:
            s = logits(b, i * tq + dd * tk, tk, rows)
            r_io = lax.broadcasted_iota(jnp.int32, s.shape, 0)
            c_io = lax.broadcasted_iota(jnp.int32, s.shape, 1)
            update(b, jnp.where(c_io <= r_io, s, -jnp.inf), i * tq + dd * tk, tk, rows, first=dd == 0)

    reach = thr_ref[0] - jnp.min(m_sc[...])

    def wanted(kt):
        ktc = jnp.maximum(kt, 0)
        far = c_base[0] - cend_ref[h, ktc]
        for b in batches[1:]:
            far = jnp.maximum(far, c_base[b] - cend_ref[b * N_DENSE_HEADS + h, ktc])
        return jnp.logical_and(kt >= 0, far >= -reach)

    everything = slice(0, tq)
    kt0 = i * nsub - 1
    for b in batches:
        s_sc[b] = logits(b, jnp.maximum(kt0, 0) * tk, tk, everything)

    def body(c):
        kt, _ = c
        for b in batches:
            s = s_sc[b]
            s_sc[b] = logits(b, jnp.maximum(kt - 1, 0) * tk, tk, everything)
            update(b, s, kt * tk, tk, everything)
        return kt - 1, wanted(kt - 1)

    lax.while_loop(lambda c: c[1], body, (kt0, wanted(kt0)))
    for b in batches:
        o_ref[b] = (acc_sc[b, :, :LANES] * (1.0 / acc_sc[b, :, LANES:])
                    * sg_ref[b].astype(F32)).astype(BF16)


def _fox_attn(cend, thr, fq, fk, fv, cum, sgate):
    B, S, _ = fq.shape
    tq = ATT_ROWS
    tile = pl.BlockSpec((B, tq, DENSE_HEAD_DIM), lambda h, i: (0, i, h))
    full = pl.BlockSpec((B, S, DENSE_HEAD_DIM), lambda h, i: (0, 0, h))
    smem = pl.BlockSpec(memory_space=pltpu.SMEM)
    return pl.pallas_call(
        _fox_kernel,
        grid=(N_DENSE_HEADS, S // tq),
        in_specs=[smem, smem, tile, full, full,
                  pl.BlockSpec((B, SUBLANES, S), lambda h, i: (0, 0, 0)), tile],
        out_specs=tile,
        out_shape=jax.ShapeDtypeStruct((B, S, DENSE_W), BF16),
        scratch_shapes=[pltpu.VMEM((B, tq, LANES), F32), pltpu.VMEM((B, tq, 2 * LANES), F32),
                        pltpu.VMEM((B, tq, FOX_KEYS), F32)],
        compiler_params=_cparams(("arbitrary", "arbitrary")),
        name="fox_attn",
    )(cend, thr, fq, fk, fv, cum, sgate)


def _sb_kernel(q_ref, k_ref, v_ref, u_ref, sg_ref, o_ref, off_sc, acc_sc):
    i = pl.program_id(1)
    nbatch, tq = q_ref.shape[0], q_ref.shape[1]
    r = SB_BLOCK
    nblk = tq // r
    chains = [(b, rb) for b in range(nbatch) for rb in range(nblk)]
    off_sc[...] = jnp.zeros_like(off_sc)
    acc_sc[...] = jnp.zeros_like(acc_sc)
    r_io = lax.broadcasted_iota(jnp.int32, (r, r), 0)
    c_io = lax.broadcasted_iota(jnp.int32, (r, r), 1)
    strict = c_io < r_io

    def walk(j, diagonal):
        rows = [slice(rb * r, (rb + 1) * r) for _, rb in chains]
        kts = [i * nblk + rb - j for _, rb in chains]
        keys = [pl.ds(pl.multiple_of(jnp.maximum(kt, 0) * r, r), r) for kt in kts]
        zs = [lax.dot_general(q_ref[b, rows[c], :], k_ref[b, keys[c], :], _NT, preferred_element_type=F32)
              for c, (b, _) in enumerate(chains)]
        pairs, offs, laters = [], [], []
        for c, (b, _) in enumerate(chains):
            log_beta, log_om = _log2_sigmoid_pair(zs[c])
            off = off_sc[b, rows[c], :]
            if diagonal:
                log_om = jnp.where(strict, log_om, 0.0)
            else:
                off = jnp.where(kts[c] < 0, -1e30, off)
            pairs.append((log_beta, log_om))
            offs.append(off)
            laters.append(jnp.dot(log_om.astype(BF16), u_ref[...], preferred_element_type=F32)
                          + jnp.concatenate([off] * (r // LANES), axis=1))
        for c, (b, _) in enumerate(chains):
            log_beta, log_om = pairs[c]
            w = jnp.exp2(log_beta + laters[c])
            if diagonal:
                w = jnp.where(strict, w, 0.0)
            acc_sc[b, rows[c], :] += jnp.dot(w.astype(BF16), v_ref[b, keys[c], :],
                                             preferred_element_type=F32)
            off_sc[b, rows[c], :] = offs[c] + jnp.sum(log_om, axis=-1, keepdims=True)

    walk(0, True)

    def body(c):
        j, _ = c
        walk(j, False)
        alive = jnp.max(off_sc[...]) >= -(EXP_ZERO + 8.0) * LOG2E
        keys_left = i * nblk + nblk - 1 - (j + 1) >= 0
        return j + 1, jnp.logical_and(keys_left, alive)

    lax.while_loop(lambda c: c[1], body, (1, True))
    for b in range(nbatch):
        o_ref[b] = (acc_sc[b] * sg_ref[b].astype(F32)).astype(BF16)


def _sb_attn(sq, sk, sv, upper, sgate):
    B, S, _ = sq.shape
    tq, tk = ATT_ROWS, SB_BLOCK
    tile = pl.BlockSpec((B, tq, DENSE_HEAD_DIM), lambda h, i: (0, i, h))
    gate_tile = pl.BlockSpec((B, tq, DENSE_HEAD_DIM), lambda h, i: (0, i, N_DENSE_HEADS + h))
    full = pl.BlockSpec((B, S, DENSE_HEAD_DIM), lambda h, i: (0, 0, h))
    return pl.pallas_call(
        _sb_kernel,
        grid=(N_DENSE_HEADS, S // tq),
        in_specs=[tile, full, full, _resident_spec((tk, tk)), gate_tile],
        out_specs=tile,
        out_shape=jax.ShapeDtypeStruct((B, S, DENSE_W), BF16),
        scratch_shapes=[pltpu.VMEM((B, tq, LANES), F32), pltpu.VMEM((B, tq, DENSE_HEAD_DIM), F32)],
        compiler_params=_cparams(("arbitrary", "arbitrary")),
        name="sb_attn",
    )(sq, sk, sv, upper, sgate)


def _odd_out_kernel(x_ref, m_ref, w_ref, o_ref):
    o_ref[...] = x_ref[...] + jnp.dot(m_ref[...], w_ref[...], preferred_element_type=F32)


def _odd_out(x, mixed, wo):
    B, S, D = x.shape
    tm = PROJ_ROWS
    row = lambda width: pl.BlockSpec((None, tm, width), lambda b, i: (b, i, 0))
    return pl.pallas_call(
        _odd_out_kernel,
        grid=(B, S // tm),
        in_specs=[row(D), row(DIL_GROUP_W), _resident_spec(wo.shape)],
        out_specs=row(D),
        out_shape=jax.ShapeDtypeStruct((B, S, D), F32),
        compiler_params=_cparams(("arbitrary", "arbitrary")),
        name="odd_out",
    )(x, mixed, wo)


def _odd_proj_kernel(x_ref, mf_ref, ms_ref, wof_ref, wos_ref, gn_ref, w_ref, gq_ref, gk_ref, res_ref, *rest):
    out_refs, sg_ref, h_sc = rest[:9], rest[9], rest[10]
    rows = x_ref.shape[0]
    n_slab = D_MODEL // LANES
    res = (x_ref[...]
           + jnp.dot(mf_ref[...], wof_ref[...], preferred_element_type=F32)
           + jnp.dot(ms_ref[...], wos_ref[...], preferred_element_type=F32))
    res_ref[...] = res
    hf = _row_rms(res, gn_ref[...])
    h = hf.astype(BF16)
    for cb in range(n_slab):
        h_sc[cb] = hf[:, cb * LANES:(cb + 1) * LANES]

    def by_residue(dil):
        parts = [jnp.concatenate([h_sc[cb, pl.ds(r, rows // dil, stride=dil), :] for cb in range(n_slab)],
                                 axis=1) for r in range(dil)]
        return jnp.concatenate(parts, axis=0).astype(BF16)

    lhs = [h] + [by_residue(dil) for dil in DILATIONS[1:]]

    lane_lo = lax.broadcasted_iota(jnp.int32, (rows, LANES), 1) < DIL_HEAD_DIM

    def head_norm(acc, g_ref):
        cols = []
        for cb in range(DIL_GROUP_W // LANES):
            x = acc[:, cb * LANES:(cb + 1) * LANES]
            x2 = x * x
            lo = jnp.sum(jnp.where(lane_lo, x2, 0.0), axis=-1, keepdims=True)
            hi = jnp.sum(jnp.where(lane_lo, 0.0, x2), axis=-1, keepdims=True)
            ms = jnp.where(lane_lo, lo, hi) * (1.0 / DIL_HEAD_DIM)
            cols.append(x * lax.rsqrt(ms + RMS_EPS))
        return jnp.concatenate(cols, axis=1) * g_ref[...]

    gate = jnp.dot(h, w_ref[:, 9 * DIL_GROUP_W:10 * DIL_GROUP_W], preferred_element_type=F32)
    sg_ref[...] = _silu(gate).astype(BF16)
    for c in sorted(range(9), key=lambda c: (c % 3, c)):
        kind, g = divmod(c, 3)
        acc = jnp.dot(lhs[g], w_ref[:, c * DIL_GROUP_W:(c + 1) * DIL_GROUP_W],
                      preferred_element_type=F32)
        if kind == 0:
            acc = head_norm(acc, gq_ref)
        elif kind == 1:
            acc = head_norm(acc, gk_ref)
        dil = DILATIONS[g]
        per = rows // dil
        for r in range(dil):
            out_refs[c][r] = acc[r * per:(r + 1) * per, :].astype(BF16)


def _odd_proj(x, mixed_f, mixed_s, wo_f, wo_s, g_norm, w, g_q, g_k):
    B, S, D = x.shape
    tm = ODD_PROJ_ROWS
    const = _resident_spec
    row = lambda width: pl.BlockSpec((None, tm, width), lambda b, i: (b, i, 0))
    out_specs, out_shape = [row(D)], [jax.ShapeDtypeStruct((B, S, D), F32)]
    for _ in range(3):
        for dil in DILATIONS:
            out_specs.append(pl.BlockSpec((None, dil, tm // dil, DIL_GROUP_W), lambda b, i: (b, 0, i, 0)))
            out_shape.append(jax.ShapeDtypeStruct((B, dil, S // dil, DIL_GROUP_W), BF16))
    out_specs.append(pl.BlockSpec((None, tm, DIL_GROUP_W), lambda b, i: (b, i, 0)))
    out_shape.append(jax.ShapeDtypeStruct((B, S, DIL_GROUP_W), BF16))
    return pl.pallas_call(
        _odd_proj_kernel,
        grid=(B, S // tm),
        in_specs=[row(D), row(DENSE_W), row(DENSE_W), const(wo_f.shape), const(wo_s.shape),
                  const((1, D)), const(w.shape), const(g_q.shape), const(g_k.shape)],
        out_specs=out_specs,
        out_shape=out_shape,
        scratch_shapes=[pltpu.VMEM((D // LANES, tm, LANES), F32)],
        compiler_params=_cparams(("arbitrary", "arbitrary")),
        name="odd_proj",
    )(x, mixed_f, mixed_s, wo_f, wo_s, g_norm, w, g_q, g_k)


def _dil_kernel(slope_ref, *refs):
    ins, (sg_ref, o_ref), (o_sc, m_sc, l_sc) = refs[:15], refs[15:17], refs[17:]
    hp, st = pl.program_id(1), pl.program_id(2)
    blk = DIL_BLK

    a_io = lax.broadcasted_iota(jnp.int32, (blk, 2 * blk), 0)
    c_io = lax.broadcasted_iota(jnp.int32, (blk, 2 * blk), 1)
    dist = a_io - c_io + blk
    band = jnp.logical_and(dist >= 0, dist <= DIL_SPAN)
    neg = jnp.where(band, 0.0, -jnp.inf)
    neg_first = jnp.where(jnp.logical_and(band, c_io >= blk), 0.0, -jnp.inf)
    dist_f = dist.astype(F32)
    lane_lo = lax.broadcasted_iota(jnp.int32, (blk, LANES), 1) < DIL_HEAD_DIM

    for g, dil in enumerate(DILATIONS):
        q_ref, kc_ref, kh_ref, vc_ref, vh_ref = ins[5 * g:5 * g + 5]
        nb = DIL_POS // dil // blk
        alibi = [dist_f * (slope_ref[g * N_DIL_HEADS + 2 * hp + hd] * float(dil)) for hd in range(2)]
        bias = jnp.concatenate([neg - alibi[0], neg - alibi[1]], axis=0)
        bias_first = jnp.concatenate([neg_first - alibi[0], neg_first - alibi[1]], axis=0)
        ones = jnp.ones((2 * blk, LANES), BF16)

        for r in range(dil):
            for n in range(nb):
                q2 = q_ref[r, n * blk:(n + 1) * blk, :]
                if n == 0:
                    k2 = jnp.concatenate([kh_ref[r], kc_ref[r, 0:blk, :]], axis=0)
                    v2 = jnp.concatenate([vh_ref[r], vc_ref[r, 0:blk, :]], axis=0)
                    b2 = jnp.where(st == 0, bias_first, bias)
                else:
                    k2 = kc_ref[r, (n - 1) * blk:(n + 1) * blk, :]
                    v2 = vc_ref[r, (n - 1) * blk:(n + 1) * blk, :]
                    b2 = bias
                zero = jnp.zeros_like(q2)
                q_st = jnp.concatenate([jnp.where(lane_lo, q2, zero), jnp.where(lane_lo, zero, q2)], axis=0)
                s = lax.dot_general(q_st, k2, _NT, preferred_element_type=F32) + b2
                m_st = jnp.max(s, axis=-1, keepdims=True)
                p = jnp.exp2(s - m_st)
                ol = jnp.dot(p.astype(BF16), jnp.concatenate([v2, ones], axis=1), preferred_element_type=F32)
                o = jnp.where(lane_lo, ol[:blk, :LANES], ol[blk:, :LANES])
                m = jnp.where(lane_lo, m_st[:blk], m_st[blk:])
                l = jnp.where(lane_lo, ol[:blk, LANES:], ol[blk:, LANES:])
                if dil == 1:
                    dst = pl.ds(n * blk, blk)
                else:
                    dst = pl.ds(n * blk * dil + r, blk, stride=dil)
                o_sc[g, dst, :] = o
                m_sc[g, dst, :] = m
                l_sc[g, dst, :] = l

    chunk = 256

    def merge(ci, carry):
        rows = pl.ds(pl.multiple_of(ci * chunk, chunk), chunk)
        ms = [m_sc[g, rows, :] for g in range(3)]
        m_all = jnp.maximum(jnp.maximum(ms[0], ms[1]), ms[2])
        num = jnp.zeros((chunk, LANES), F32)
        den = jnp.zeros((chunk, LANES), F32)
        for g in range(3):
            e = jnp.exp2(ms[g] - m_all)
            num = num + e * o_sc[g, rows, :]
            den = den + e * l_sc[g, rows, :]
        o_ref[rows, :] = (num / den * sg_ref[rows, :].astype(F32)).astype(BF16)
        return carry

    lax.fori_loop(0, DIL_POS // chunk, merge, 0)


def _dil_attn(slopes, qkv, sgate):
    B, S, _ = sgate.shape
    n_pairs = N_DIL_HEADS * DIL_HEAD_DIM // LANES
    blk = DIL_BLK
    in_specs = [pl.BlockSpec(memory_space=pltpu.SMEM)]
    args = [slopes]
    for g, dil in enumerate(DILATIONS):
        length = DIL_POS // dil
        nb = length // blk
        cur = pl.BlockSpec((None, dil, length, LANES), lambda b, hp, st: (b, 0, st, hp))
        halo = pl.BlockSpec((None, dil, blk, LANES),
                            lambda b, hp, st, nb=nb: (b, 0, jnp.maximum(st * nb - 1, 0), hp))
        q, k, v = qkv[g], qkv[3 + g], qkv[6 + g]
        in_specs += [cur, cur, halo, cur, halo]
        args += [q, k, k, v, v]
    pos = pl.BlockSpec((None, DIL_POS, LANES), lambda b, hp, st: (b, st, hp))
    in_specs.append(pos)
    args.append(sgate)
    return pl.pallas_call(
        _dil_kernel,
        grid=(B, n_pairs, S // DIL_POS),
        in_specs=in_specs,
        out_specs=pos,
        out_shape=jax.ShapeDtypeStruct((B, S, DIL_GROUP_W), BF16),
        scratch_shapes=[pltpu.VMEM((3, DIL_POS, LANES), F32)] * 3,
        compiler_params=_cparams(("arbitrary", "arbitrary", "arbitrary")),
        name="dil_attn",
    )(*args)


def _even_mixers(x, g_norm, w_in, b_f, g_q, g_k):
    B, S, D = x.shape
    n_f = N_DENSE_HEADS
    cut0, cut1 = 3 * DENSE_W, 3 * DENSE_W + n_f
    wa, wb = w_in[:, :cut0].astype(BF16), w_in[:, cut1:].astype(BF16)
    wf = jnp.zeros((BF16_SUBLANES, D), F32).at[:n_f].set(w_in[:, cut0:cut1].T).astype(BF16)
    bf = jnp.zeros((SUBLANES, LANES), F32).at[:n_f].set(jnp.broadcast_to(b_f[:, None], (n_f, LANES)))
    gq = (g_q * (DENSE_HEAD_DIM ** -0.5 * LOG2E)).reshape(1, DENSE_HEAD_DIM)
    gk = g_k.reshape(1, DENSE_HEAD_DIM)
    fq, fk, fv, sq, sk, sv, sgate, cum = _even_proj(x, g_norm.reshape(1, D), wa, wb, wf, bf, gq, gk)

    t = FOX_KEYS
    cend = cum[:, :n_f, t - 1::t].reshape(B * n_f, S // t)
    qk_bound = math.sqrt(DENSE_HEAD_DIM) * jnp.max(jnp.abs(g_q)) * jnp.max(jnp.abs(g_k)) * 1.02
    thr = ((EXP_ZERO + 1.0 + qk_bound) * LOG2E).reshape(1).astype(F32)
    mixed_f = _fox_attn(cend, thr, fq, fk, fv, cum, sgate)

    idx = jnp.arange(SB_BLOCK)
    upper = (idx[:, None] > idx[None, :]).astype(BF16)
    mixed_s = _sb_attn(sq, sk, sv, upper, sgate)
    return mixed_f, mixed_s


def _odd_layer(x, mixed_f, mixed_s, even_w_out, g_norm, w_in, g_q, g_k, w_out):
    B, S, D = x.shape
    wo_f, wo_s = even_w_out[:DENSE_W].astype(BF16), even_w_out[DENSE_W:].astype(BF16)
    w = w_in.astype(BF16)
    gq = jnp.tile(g_q * (DIL_HEAD_DIM ** -0.5 * LOG2E), N_DIL_HEADS).reshape(1, DIL_GROUP_W)
    gk = jnp.tile(g_k, N_DIL_HEADS).reshape(1, DIL_GROUP_W)
    outs = _odd_proj(x, mixed_f, mixed_s, wo_f, wo_s, g_norm.reshape(1, D), w, gq, gk)
    n_all = len(DILATIONS) * N_DIL_HEADS
    slopes = jnp.asarray([LOG2E * 2.0 ** (-8.0 * (i + 1) / n_all) for i in range(n_all)], F32)
    mixed = _dil_attn(slopes, outs[1:10], outs[10])
    return _odd_out(outs[0], mixed, w_out.astype(BF16))


def kernel(x, even_norm, even_w_in, even_b_f, even_q_gain, even_k_gain, even_w_out,
           odd_norm, odd_w_in, odd_q_gain, odd_k_gain, odd_w_out):
    assert x.shape[1] % DIL_POS == 0 and x.shape[2] == D_MODEL
    mixed_f, mixed_s = _even_mixers(x, even_norm[0], even_w_in[0], even_b_f[0], even_q_gain[0],
                                    even_k_gain[0])
    return _odd_layer(x, mixed_f, mixed_s, even_w_out[0], odd_norm[0], odd_w_in[0], odd_q_gain[0],
                      odd_k_gain[0], odd_w_out[0])
```

```python
import math

import jax
import jax.numpy as jnp
from jax import lax
from jax.experimental import pallas as pl
from jax.experimental.pallas import tpu as pltpu

F32 = jnp.float32
BF16 = jnp.bfloat16

D_MODEL = 1024
DENSE_HEAD_DIM = 128
N_DENSE_HEADS = 4
DENSE_W = N_DENSE_HEADS * DENSE_HEAD_DIM
DIL_HEAD_DIM = 64
N_DIL_HEADS = 8
DIL_GROUP_W = N_DIL_HEADS * DIL_HEAD_DIM
DILATIONS = (1, 4, 16)
DIL_SPAN = 128
DIL_BLK = 128
RMS_EPS = 1e-6

LANES = 128
SUBLANES = 8
BF16_SUBLANES = 16
PROJ_ROWS = 1024
PROJ_SUB_ROWS = 512
ODD_PROJ_ROWS = 512
ATT_ROWS = 512
FOX_KEYS = 512
SB_BLOCK = 256
DIL_POS = DILATIONS[-1] * DIL_BLK
VMEM_LIMIT = 56 * 1024 * 1024

EXP_ZERO = 104.0
LOG2E = math.log2(math.e)

_NT = (((1,), (1,)), ((), ()))


def _cparams(sem):
    return pltpu.CompilerParams(dimension_semantics=sem, vmem_limit_bytes=VMEM_LIMIT)


def _resident_spec(shape):
    return pl.BlockSpec(shape, lambda *_: (0,) * len(shape), pipeline_mode=pl.Buffered(1))


def _log_sigmoid(z):
    return jnp.minimum(z, 0.0) - jnp.log(1.0 + jnp.exp(-jnp.abs(z)))


def _log2_sigmoid_pair(z2):
    soft = jnp.log2(1.0 + jnp.exp2(-jnp.abs(z2)))
    log_beta = jnp.minimum(z2, 0.0) - soft
    return log_beta, log_beta - z2


def _silu(x):
    return x * (1.0 / (1.0 + jnp.exp(-x)))


def _row_rms(x, g):
    ms = jnp.mean(x * x, axis=-1, keepdims=True)
    return x * lax.rsqrt(ms + RMS_EPS) * g


def _even_proj_kernel(x_ref, gn_ref, wa_ref, wb_ref, wf_ref, bf_ref, gq_ref, gk_ref,
                      fq_ref, fk_ref, fv_ref, sq_ref, sk_ref, sv_ref, sg_ref, cum_ref, carry_sc):
    i = pl.program_id(1)
    rows = x_ref.shape[0]
    sub = PROJ_SUB_ROWS
    lane = lax.broadcasted_iota(jnp.int32, (SUBLANES, sub), 1)

    @pl.when(i == 0)
    def _():
        carry_sc[...] = jnp.zeros_like(carry_sc)

    carry = carry_sc[:, 0:1]
    for r0 in range(0, rows, sub):
        rs = slice(r0, r0 + sub)
        h = _row_rms(x_ref[rs, :], gn_ref[...]).astype(BF16)

        fl = lax.dot_general(wf_ref[...], h, _NT, preferred_element_type=F32)[:SUBLANES] + bf_ref[:, 0:1]
        lf = _log_sigmoid(fl)
        shift = 1
        while shift < sub:
            lf = lf + jnp.where(lane >= shift, pltpu.roll(lf, shift, axis=1), 0.0)
            shift *= 2
        cum = lf + carry
        cum_ref[:, rs] = cum * LOG2E
        carry = cum[:, sub - 1:sub]

        def proj(w_ref, chunk, h=h):
            return jnp.dot(h, w_ref[:, chunk * DENSE_W:(chunk + 1) * DENSE_W], preferred_element_type=F32)

        def head_norm(acc, g_ref, out_ref, rs=rs):
            for hh in range(N_DENSE_HEADS):
                sl = slice(hh * DENSE_HEAD_DIM, (hh + 1) * DENSE_HEAD_DIM)
                out_ref[rs, sl] = _row_rms(acc[:, sl], g_ref[...]).astype(BF16)

        head_norm(proj(wa_ref, 0), gq_ref, fq_ref)
        head_norm(proj(wa_ref, 1), gk_ref, fk_ref)
        for c in range(2):
            sl = slice(c * DENSE_W, (c + 1) * DENSE_W)
            sg_ref[rs, sl] = _silu(proj(wb_ref, 3 + c)).astype(BF16)
        sq_ref[rs, :] = (proj(wb_ref, 0) * (DENSE_HEAD_DIM ** -0.5 * LOG2E)).astype(BF16)
        fv_ref[rs, :] = proj(wa_ref, 2).astype(BF16)
        sk_ref[rs, :] = proj(wb_ref, 1).astype(BF16)
        sv_ref[rs, :] = proj(wb_ref, 2).astype(BF16)
    carry_sc[...] = jnp.broadcast_to(carry, carry_sc.shape)


def _even_proj(x, g_norm, wa, wb, wf, b_f, g_q, g_k):
    B, S, D = x.shape
    tm = PROJ_ROWS
    row = lambda width: pl.BlockSpec((None, tm, width), lambda b, i: (b, i, 0))
    const = _resident_spec
    act = lambda width: jax.ShapeDtypeStruct((B, S, width), BF16)
    return pl.pallas_call(
        _even_proj_kernel,
        grid=(B, S // tm),
        in_specs=[row(D), const((1, D)), const(wa.shape), const(wb.shape), const(wf.shape),
                  const(b_f.shape), const(g_q.shape), const(g_k.shape)],
        out_specs=[row(DENSE_W)] * 6 + [row(2 * DENSE_W),
                                        pl.BlockSpec((None, SUBLANES, tm), lambda b, i: (b, 0, i))],
        out_shape=[act(DENSE_W)] * 6 + [act(2 * DENSE_W), jax.ShapeDtypeStruct((B, SUBLANES, S), F32)],
        scratch_shapes=[pltpu.VMEM((SUBLANES, LANES), F32)],
        compiler_params=_cparams(("arbitrary", "arbitrary")),
        name="even_proj",
    )(x, g_norm, wa, wb, wf, b_f, g_q, g_k)


def _fox_kernel(cend_ref, thr_ref, q_ref, k_ref, v_ref, c_ref, sg_ref, o_ref, m_sc, acc_sc, s_sc):
    h, i = pl.program_id(0), pl.program_id(1)
    nbatch, tq = q_ref.shape[0], q_ref.shape[1]
    tk = FOX_KEYS
    nsub = tq // tk
    batches = range(nbatch)
    c_base = [jnp.where(i > 0, cend_ref[b * N_DENSE_HEADS + h, jnp.maximum(i * nsub - 1, 0)], 0.0)
              for b in batches]

    def logits(b, start, width, rows):
        keys = pl.ds(pl.multiple_of(start, width), width)
        s = lax.dot_general(q_ref[b, rows, :], k_ref[b, keys, :], _NT, preferred_element_type=F32)
        return s + (c_base[b] - c_ref[b, pl.ds(h, 1), keys])

    def update(b, s, start, width, rows, first=False):
        keys = pl.ds(pl.multiple_of(start, width), width)
        m_cur = jnp.broadcast_to(jnp.max(s, axis=-1, keepdims=True), (s.shape[0], LANES))
        m_new = m_cur if first else jnp.maximum(m_sc[b, rows, :], m_cur)
        p = jnp.exp2(s - jnp.concatenate([m_new] * (width // LANES), axis=1))
        v_ones = jnp.concatenate([v_ref[b, keys, :], jnp.ones((width, LANES), BF16)], axis=1)
        pv = jnp.dot(p.astype(BF16), v_ones, preferred_element_type=F32)
        if first:
            acc_sc[b, rows, :] = pv
        else:
            alpha = jnp.exp2(m_sc[b, rows, :] - m_new)
            acc_sc[b, rows, :] = jnp.concatenate([alpha, alpha], axis=1) * acc_sc[b, rows, :] + pv
        m_sc[b, rows, :] = m_new

    for dd in range(nsub):
        rows = slice(dd * tk, tq)
        for b in batches:
            s = logits(b, i * tq + dd * tk, tk, rows)
            r_io = lax.broadcasted_iota(jnp.int32, s.shape, 0)
            c_io = lax.broadcasted_iota(jnp.int32, s.shape, 1)
            update(b, jnp.where(c_io <= r_io, s, -jnp.inf), i * tq + dd * tk, tk, rows, first=dd == 0)

    reach = thr_ref[0] - jnp.min(m_sc[...])

    def wanted(kt):
        ktc = jnp.maximum(kt, 0)
        far = c_base[0] - cend_ref[h, ktc]
        for b in batches[1:]:
            far = jnp.maximum(far, c_base[b] - cend_ref[b * N_DENSE_HEADS + h, ktc])
        return jnp.logical_and(kt >= 0, far >= -reach)

    everything = slice(0, tq)
    kt0 = i * nsub - 1
    for b in batches:
        s_sc[b] = logits(b, jnp.maximum(kt0, 0) * tk, tk, everything)

    def body(c):
        kt, _ = c
        for b in batches:
            s = s_sc[b]
            s_sc[b] = logits(b, jnp.maximum(kt - 1, 0) * tk, tk, everything)
            update(b, s, kt * tk, tk, everything)
        return kt - 1, wanted(kt - 1)

    lax.while_loop(lambda c: c[1], body, (kt0, wanted(kt0)))
    for b in batches:
        o_ref[b] = (acc_sc[b, :, :LANES] * (1.0 / acc_sc[b, :, LANES:])
                    * sg_ref[b].astype(F32)).astype(BF16)


def _fox_attn(cend, thr, fq, fk, fv, cum, sgate):
    B, S, _ = fq.shape
    tq = ATT_ROWS
    tile = pl.BlockSpec((B, tq, DENSE_HEAD_DIM), lambda h, i: (0, i, h))
    full = pl.BlockSpec((B, S, DENSE_HEAD_DIM), lambda h, i: (0, 0, h))
    smem = pl.BlockSpec(memory_space=pltpu.SMEM)
    return pl.pallas_call(
        _fox_kernel,
        grid=(N_DENSE_HEADS, S // tq),
        in_specs=[smem, smem, tile, full, full,
                  pl.BlockSpec((B, SUBLANES, S), lambda h, i: (0, 0, 0)), tile],
        out_specs=tile,
        out_shape=jax.ShapeDtypeStruct((B, S, DENSE_W), BF16),
        scratch_shapes=[pltpu.VMEM((B, tq, LANES), F32), pltpu.VMEM((B, tq, 2 * LANES), F32),
                        pltpu.VMEM((B, tq, FOX_KEYS), F32)],
        compiler_params=_cparams(("arbitrary", "arbitrary")),
        name="fox_attn",
    )(cend, thr, fq, fk, fv, cum, sgate)


def _sb_kernel(q_ref, k_ref, v_ref, u_ref, sg_ref, o_ref, off_sc, acc_sc):
    i = pl.program_id(1)
    nbatch, tq = q_ref.shape[0], q_ref.shape[1]
    r = SB_BLOCK
    nblk = tq // r
    chains = [(b, rb) for b in range(nbatch) for rb in range(nblk)]
    off_sc[...] = jnp.zeros_like(off_sc)
    acc_sc[...] = jnp.zeros_like(acc_sc)
    r_io = lax.broadcasted_iota(jnp.int32, (r, r), 0)
    c_io = lax.broadcasted_iota(jnp.int32, (r, r), 1)
    strict = c_io < r_io

    def walk(j, diagonal):
        rows = [slice(rb * r, (rb + 1) * r) for _, rb in chains]
        kts = [i * nblk + rb - j for _, rb in chains]
        keys = [pl.ds(pl.multiple_of(jnp.maximum(kt, 0) * r, r), r) for kt in kts]
        zs = [lax.dot_general(q_ref[b, rows[c], :], k_ref[b, keys[c], :], _NT, preferred_element_type=F32)
              for c, (b, _) in enumerate(chains)]
        pairs, offs, laters = [], [], []
        for c, (b, _) in enumerate(chains):
            log_beta, log_om = _log2_sigmoid_pair(zs[c])
            off = off_sc[b, rows[c], :]
            if diagonal:
                log_om = jnp.where(strict, log_om, 0.0)
            else:
                off = jnp.where(kts[c] < 0, -1e30, off)
            pairs.append((log_beta, log_om))
            offs.append(off)
            laters.append(jnp.dot(log_om.astype(BF16), u_ref[...], preferred_element_type=F32)
                          + jnp.concatenate([off] * (r // LANES), axis=1))
        for c, (b, _) in enumerate(chains):
            log_beta, log_om = pairs[c]
            w = jnp.exp2(log_beta + laters[c])
            if diagonal:
                w = jnp.where(strict, w, 0.0)
            acc_sc[b, rows[c], :] += jnp.dot(w.astype(BF16), v_ref[b, keys[c], :],
                                             preferred_element_type=F32)
            off_sc[b, rows[c], :] = offs[c] + jnp.sum(log_om, axis=-1, keepdims=True)

    walk(0, True)

    def body(c):
        j, _ = c
        walk(j, False)
        alive = jnp.max(off_sc[...]) >= -(EXP_ZERO + 8.0) * LOG2E
        keys_left = i * nblk + nblk - 1 - (j + 1) >= 0
        return j + 1, jnp.logical_and(keys_left, alive)

    lax.while_loop(lambda c: c[1], body, (1, True))
    for b in range(nbatch):
        o_ref[b] = (acc_sc[b] * sg_ref[b].astype(F32)).astype(BF16)


def _sb_attn(sq, sk, sv, upper, sgate):
    B, S, _ = sq.shape
    tq, tk = ATT_ROWS, SB_BLOCK
    tile = pl.BlockSpec((B, tq, DENSE_HEAD_DIM), lambda h, i: (0, i, h))
    gate_tile = pl.BlockSpec((B, tq, DENSE_HEAD_DIM), lambda h, i: (0, i, N_DENSE_HEADS + h))
    full = pl.BlockSpec((B, S, DENSE_HEAD_DIM), lambda h, i: (0, 0, h))
    return pl.pallas_call(
        _sb_kernel,
        grid=(N_DENSE_HEADS, S // tq),
        in_specs=[tile, full, full, _resident_spec((tk, tk)), gate_tile],
        out_specs=tile,
        out_shape=jax.ShapeDtypeStruct((B, S, DENSE_W), BF16),
        scratch_shapes=[pltpu.VMEM((B, tq, LANES), F32), pltpu.VMEM((B, tq, DENSE_HEAD_DIM), F32)],
        compiler_params=_cparams(("arbitrary", "arbitrary")),
        name="sb_attn",
    )(sq, sk, sv, upper, sgate)


def _odd_out_kernel(x_ref, m_ref, w_ref, o_ref):
    o_ref[...] = x_ref[...] + jnp.dot(m_ref[...], w_ref[...], preferred_element_type=F32)


def _odd_out(x, mixed, wo):
    B, S, D = x.shape
    tm = PROJ_ROWS
    row = lambda width: pl.BlockSpec((None, tm, width), lambda b, i: (b, i, 0))
    return pl.pallas_call(
        _odd_out_kernel,
        grid=(B, S // tm),
        in_specs=[row(D), row(DIL_GROUP_W), _resident_spec(wo.shape)],
        out_specs=row(D),
        out_shape=jax.ShapeDtypeStruct((B, S, D), F32),
        compiler_params=_cparams(("arbitrary", "arbitrary")),
        name="odd_out",
    )(x, mixed, wo)


def _odd_proj_kernel(x_ref, mf_ref, ms_ref, wof_ref, wos_ref, gn_ref, w_ref, gq_ref, gk_ref, res_ref, *rest):
    out_refs, sg_ref, h_sc = rest[:9], rest[9], rest[10]
    rows = x_ref.shape[0]
    n_slab = D_MODEL // LANES
    res = (x_ref[...]
           + jnp.dot(mf_ref[...], wof_ref[...], preferred_element_type=F32)
           + jnp.dot(ms_ref[...], wos_ref[...], preferred_element_type=F32))
    res_ref[...] = res
    hf = _row_rms(res, gn_ref[...])
    h = hf.astype(BF16)
    for cb in range(n_slab):
        h_sc[cb] = hf[:, cb * LANES:(cb + 1) * LANES]

    def by_residue(dil):
        parts = [jnp.concatenate([h_sc[cb, pl.ds(r, rows // dil, stride=dil), :] for cb in range(n_slab)],
                                 axis=1) for r in range(dil)]
        return jnp.concatenate(parts, axis=0).astype(BF16)

    lhs = [h] + [by_residue(dil) for dil in DILATIONS[1:]]

    lane_lo = lax.broadcasted_iota(jnp.int32, (rows, LANES), 1) < DIL_HEAD_DIM

    def head_norm(acc, g_ref):
        cols = []
        for cb in range(DIL_GROUP_W // LANES):
            x = acc[:, cb * LANES:(cb + 1) * LANES]
            x2 = x * x
            lo = jnp.sum(jnp.where(lane_lo, x2, 0.0), axis=-1, keepdims=True)
            hi = jnp.sum(jnp.where(lane_lo, 0.0, x2), axis=-1, keepdims=True)
            ms = jnp.where(lane_lo, lo, hi) * (1.0 / DIL_HEAD_DIM)
            cols.append(x * lax.rsqrt(ms + RMS_EPS))
        return jnp.concatenate(cols, axis=1) * g_ref[...]

    gate = jnp.dot(h, w_ref[:, 9 * DIL_GROUP_W:10 * DIL_GROUP_W], preferred_element_type=F32)
    sg_ref[...] = _silu(gate).astype(BF16)
    for c in sorted(range(9), key=lambda c: (c % 3, c)):
        kind, g = divmod(c, 3)
        acc = jnp.dot(lhs[g], w_ref[:, c * DIL_GROUP_W:(c + 1) * DIL_GROUP_W],
                      preferred_element_type=F32)
        if kind == 0:
            acc = head_norm(acc, gq_ref)
        elif kind == 1:
            acc = head_norm(acc, gk_ref)
        dil = DILATIONS[g]
        per = rows // dil
        for r in range(dil):
            out_refs[c][r] = acc[r * per:(r + 1) * per, :].astype(BF16)


def _odd_proj(x, mixed_f, mixed_s, wo_f, wo_s, g_norm, w, g_q, g_k):
    B, S, D = x.shape
    tm = ODD_PROJ_ROWS
    const = _resident_spec
    row = lambda width: pl.BlockSpec((None, tm, width), lambda b, i: (b, i, 0))
    out_specs, out_shape = [row(D)], [jax.ShapeDtypeStruct((B, S, D), F32)]
    for _ in range(3):
        for dil in DILATIONS:
            out_specs.append(pl.BlockSpec((None, dil, tm // dil, DIL_GROUP_W), lambda b, i: (b, 0, i, 0)))
            out_shape.append(jax.ShapeDtypeStruct((B, dil, S // dil, DIL_GROUP_W), BF16))
    out_specs.append(pl.BlockSpec((None, tm, DIL_GROUP_W), lambda b, i: (b, i, 0)))
    out_shape.append(jax.ShapeDtypeStruct((B, S, DIL_GROUP_W), BF16))
    return pl.pallas_call(
        _odd_proj_kernel,
        grid=(B, S // tm),
        in_specs=[row(D), row(DENSE_W), row(DENSE_W), const(wo_f.shape), const(wo_s.shape),
                  const((1, D)), const(w.shape), const(g_q.shape), const(g_k.shape)],
        out_specs=out_specs,
        out_shape=out_shape,
        scratch_shapes=[pltpu.VMEM((D // LANES, tm, LANES), F32)],
        compiler_params=_cparams(("arbitrary", "arbitrary")),
        name="odd_proj",
    )(x, mixed_f, mixed_s, wo_f, wo_s, g_norm, w, g_q, g_k)


def _dil_kernel(slope_ref, *refs):
    ins, (sg_ref, o_ref), (o_sc, m_sc, l_sc) = refs[:15], refs[15:17], refs[17:]
    hp, st = pl.program_id(1), pl.program_id(2)
    blk = DIL_BLK
    quarter = DIL_POS // 4

    a_io = lax.broadcasted_iota(jnp.int32, (blk, 2 * blk), 0)
    c_io = lax.broadcasted_iota(jnp.int32, (blk, 2 * blk), 1)
    dist = a_io - c_io + blk
    band = jnp.logical_and(dist >= 0, dist <= DIL_SPAN)
    neg = jnp.where(band, 0.0, -jnp.inf)
    neg_first = jnp.where(jnp.logical_and(band, c_io >= blk), 0.0, -jnp.inf)
    dist_f = dist.astype(F32)
    lane_lo = lax.broadcasted_iota(jnp.int32, (blk, LANES), 1) < DIL_HEAD_DIM

    for g, dil in enumerate(DILATIONS):
        q_ref, kc_ref, kh_ref, vc_ref, vh_ref = ins[5 * g:5 * g + 5]
        nb = DIL_POS // dil // blk
        alibi = [dist_f * (slope_ref[g * N_DIL_HEADS + 2 * hp + hd] * float(dil)) for hd in range(2)]
        bias = jnp.concatenate([neg - alibi[0], neg - alibi[1]], axis=0)
        bias_first = jnp.concatenate([neg_first - alibi[0], neg_first - alibi[1]], axis=0)
        ones = jnp.ones((2 * blk, LANES), BF16)

        for r in range(dil):
            for n in range(nb):
                q2 = q_ref[r, n * blk:(n + 1) * blk, :]
                if n == 0:
                    k2 = jnp.concatenate([kh_ref[r], kc_ref[r, 0:blk, :]], axis=0)
                    v2 = jnp.concatenate([vh_ref[r], vc_ref[r, 0:blk, :]], axis=0)
                    b2 = jnp.where(st == 0, bias_first, bias)
                else:
                    k2 = kc_ref[r, (n - 1) * blk:(n + 1) * blk, :]
                    v2 = vc_ref[r, (n - 1) * blk:(n + 1) * blk, :]
                    b2 = bias
                zero = jnp.zeros_like(q2)
                q_st = jnp.concatenate([jnp.where(lane_lo, q2, zero), jnp.where(lane_lo, zero, q2)], axis=0)
                s = lax.dot_general(q_st, k2, _NT, preferred_element_type=F32) + b2
                m_st = jnp.max(s, axis=-1, keepdims=True)
                p = jnp.exp2(s - m_st)
                ol = jnp.dot(p.astype(BF16), jnp.concatenate([v2, ones], axis=1), preferred_element_type=F32)
                o = jnp.where(lane_lo, ol[:blk, :LANES], ol[blk:, :LANES])
                m = jnp.where(lane_lo, m_st[:blk], m_st[blk:])
                l = jnp.where(lane_lo, ol[:blk, LANES:], ol[blk:, LANES:])
                pos = (n * blk) * dil + r
                if dil == 1:
                    dst = pl.ds(pos, blk)
                else:
                    dst = pl.ds((pos % 4) * quarter + pos // 4, blk, stride=dil // 4)
                o_sc[g, dst, :] = o
                m_sc[g, dst, :] = m
                l_sc[g, dst, :] = l

    def merge(ci, carry):
        res, idx = ci // (quarter // blk), (ci % (quarter // blk)) * blk
        rows = ([pl.ds(4 * idx + res, blk, stride=4)]
                + [pl.ds(pl.multiple_of(res * quarter + idx, blk), blk)] * 2)
        ms = [m_sc[g, rows[g], :] for g in range(3)]
        m_all = jnp.maximum(jnp.maximum(ms[0], ms[1]), ms[2])
        num = jnp.zeros((blk, LANES), F32)
        den = jnp.zeros((blk, LANES), F32)
        for g in range(3):
            e = jnp.exp2(ms[g] - m_all)
            num = num + e * o_sc[g, rows[g], :]
            den = den + e * l_sc[g, rows[g], :]
        o_sc[0, rows[0], :] = num / den
        return carry

    lax.fori_loop(0, DIL_POS // blk, merge, 0)

    chunk = 256

    def finish(ci, carry):
        rows = pl.ds(pl.multiple_of(ci * chunk, chunk), chunk)
        o_ref[rows, :] = (o_sc[0, rows, :] * sg_ref[rows, :].astype(F32)).astype(BF16)
        return carry

    lax.fori_loop(0, DIL_POS // chunk, finish, 0)


def _dil_attn(slopes, qkv, sgate):
    B, S, _ = sgate.shape
    n_pairs = N_DIL_HEADS * DIL_HEAD_DIM // LANES
    blk = DIL_BLK
    in_specs = [pl.BlockSpec(memory_space=pltpu.SMEM)]
    args = [slopes]
    for g, dil in enumerate(DILATIONS):
        length = DIL_POS // dil
        nb = length // blk
        cur = pl.BlockSpec((None, dil, length, LANES), lambda b, hp, st: (b, 0, st, hp))
        halo = pl.BlockSpec((None, dil, blk, LANES),
                            lambda b, hp, st, nb=nb: (b, 0, jnp.maximum(st * nb - 1, 0), hp))
        q, k, v = qkv[g], qkv[3 + g], qkv[6 + g]
        in_specs += [cur, cur, halo, cur, halo]
        args += [q, k, k, v, v]
    pos = pl.BlockSpec((None, DIL_POS, LANES), lambda b, hp, st: (b, st, hp))
    in_specs.append(pos)
    args.append(sgate)
    return pl.pallas_call(
        _dil_kernel,
        grid=(B, n_pairs, S // DIL_POS),
        in_specs=in_specs,
        out_specs=pos,
        out_shape=jax.ShapeDtypeStruct((B, S, DIL_GROUP_W), BF16),
        scratch_shapes=[pltpu.VMEM((3, DIL_POS, LANES), F32)] * 3,
        compiler_params=_cparams(("arbitrary", "arbitrary", "arbitrary")),
        name="dil_attn",
    )(*args)


def _even_mixers(x, g_norm, w_in, b_f, g_q, g_k):
    B, S, D = x.shape
    n_f = N_DENSE_HEADS
    cut0, cut1 = 3 * DENSE_W, 3 * DENSE_W + n_f
    wa, wb = w_in[:, :cut0].astype(BF16), w_in[:, cut1:].astype(BF16)
    wf = jnp.zeros((BF16_SUBLANES, D), F32).at[:n_f].set(w_in[:, cut0:cut1].T).astype(BF16)
    bf = jnp.zeros((SUBLANES, LANES), F32).at[:n_f].set(jnp.broadcast_to(b_f[:, None], (n_f, LANES)))
    gq = (g_q * (DENSE_HEAD_DIM ** -0.5 * LOG2E)).reshape(1, DENSE_HEAD_DIM)
    gk = g_k.reshape(1, DENSE_HEAD_DIM)
    fq, fk, fv, sq, sk, sv, sgate, cum = _even_proj(x, g_norm.reshape(1, D), wa, wb, wf, bf, gq, gk)

    t = FOX_KEYS
    cend = cum[:, :n_f, t - 1::t].reshape(B * n_f, S // t)
    qk_bound = math.sqrt(DENSE_HEAD_DIM) * jnp.max(jnp.abs(g_q)) * jnp.max(jnp.abs(g_k)) * 1.02
    thr = ((EXP_ZERO + 1.0 + qk_bound) * LOG2E).reshape(1).astype(F32)
    mixed_f = _fox_attn(cend, thr, fq, fk, fv, cum, sgate)

    idx = jnp.arange(SB_BLOCK)
    upper = (idx[:, None] > idx[None, :]).astype(BF16)
    mixed_s = _sb_attn(sq, sk, sv, upper, sgate)
    return mixed_f, mixed_s


def _odd_layer(x, mixed_f, mixed_s, even_w_out, g_norm, w_in, g_q, g_k, w_out):
    B, S, D = x.shape
    wo_f, wo_s = even_w_out[:DENSE_W].astype(BF16), even_w_out[DENSE_W:].astype(BF16)
    w = w_in.astype(BF16)
    gq = jnp.tile(g_q * (DIL_HEAD_DIM ** -0.5 * LOG2E), N_DIL_HEADS).reshape(1, DIL_GROUP_W)
    gk = jnp.tile(g_k, N_DIL_HEADS).reshape(1, DIL_GROUP_W)
    outs = _odd_proj(x, mixed_f, mixed_s, wo_f, wo_s, g_norm.reshape(1, D), w, gq, gk)
    n_all = len(DILATIONS) * N_DIL_HEADS
    slopes = jnp.asarray([LOG2E * 2.0 ** (-8.0 * (i + 1) / n_all) for i in range(n_all)], F32)
    mixed = _dil_attn(slopes, outs[1:10], outs[10])
    return _odd_out(outs[0], mixed, w_out.astype(BF16))


def kernel(x, even_norm, even_w_in, even_b_f, even_q_gain, even_k_gain, even_w_out,
           odd_norm, odd_w_in, odd_q_gain, odd_k_gain, odd_w_out):
    assert x.shape[1] % DIL_POS == 0 and x.shape[2] == D_MODEL
    mixed_f, mixed_s = _even_mixers(x, even_norm[0], even_w_in[0], even_b_f[0], even_q_gain[0],
                                    even_k_gain[0])
    return _odd_layer(x, mixed_f, mixed_s, even_w_out[0], odd_norm[0], odd_w_in[0], odd_q_gain[0],
                      odd_k_gain[0], odd_w_out[0])
```

```python
import math

import jax
import jax.numpy as jnp
from jax import lax
from jax.experimental import pallas as pl
from jax.experimental.pallas import tpu as pltpu

F32 = jnp.float32
BF16 = jnp.bfloat16

D_MODEL = 1024
DENSE_HEAD_DIM = 128
N_DENSE_HEADS = 4
DENSE_W = N_DENSE_HEADS * DENSE_HEAD_DIM
DIL_HEAD_DIM = 64
N_DIL_HEADS = 8
DIL_GROUP_W = N_DIL_HEADS * DIL_HEAD_DIM
DILATIONS = (1, 4, 16)
DIL_SPAN = 128
DIL_BLK = 128
RMS_EPS = 1e-6

LANES = 128
SUBLANES = 8
BF16_SUBLANES = 16
PROJ_ROWS = 1024
ODD_OUT_ROWS = 2048
PROJ_SUB_ROWS = 512
ODD_PROJ_ROWS = 512
ATT_ROWS = 512
FOX_KEYS = 512
SB_BLOCK = 256
DIL_POS = DILATIONS[-1] * DIL_BLK
VMEM_LIMIT = 56 * 1024 * 1024

EXP_ZERO = 104.0
LOG2E = math.log2(math.e)

_NT = (((1,), (1,)), ((), ()))


def _cparams(sem):
    return pltpu.CompilerParams(dimension_semantics=sem, vmem_limit_bytes=VMEM_LIMIT)


def _resident_spec(shape):
    return pl.BlockSpec(shape, lambda *_: (0,) * len(shape), pipeline_mode=pl.Buffered(1))


def _log_sigmoid(z):
    return jnp.minimum(z, 0.0) - jnp.log(1.0 + jnp.exp(-jnp.abs(z)))


def _log2_sigmoid_pair(z2):
    soft = jnp.log2(1.0 + jnp.exp2(-jnp.abs(z2)))
    log_beta = jnp.minimum(z2, 0.0) - soft
    return log_beta, log_beta - z2


def _silu(x):
    return x * (1.0 / (1.0 + jnp.exp(-x)))


def _row_rms(x, g):
    ms = jnp.mean(x * x, axis=-1, keepdims=True)
    return x * lax.rsqrt(ms + RMS_EPS) * g


def _even_proj_kernel(x_ref, gn_ref, wa_ref, wb_ref, wf_ref, bf_ref, gq_ref, gk_ref,
                      fq_ref, fk_ref, fv_ref, sq_ref, sk_ref, sv_ref, sg_ref, cum_ref, carry_sc):
    i = pl.program_id(1)
    rows = x_ref.shape[0]
    sub = PROJ_SUB_ROWS
    lane = lax.broadcasted_iota(jnp.int32, (SUBLANES, sub), 1)

    @pl.when(i == 0)
    def _():
        carry_sc[...] = jnp.zeros_like(carry_sc)

    carry = carry_sc[:, 0:1]
    for r0 in range(0, rows, sub):
        rs = slice(r0, r0 + sub)
        h = _row_rms(x_ref[rs, :], gn_ref[...]).astype(BF16)

        fl = lax.dot_general(wf_ref[...], h, _NT, preferred_element_type=F32)[:SUBLANES] + bf_ref[:, 0:1]
        lf = _log_sigmoid(fl)
        shift = 1
        while shift < sub:
            lf = lf + jnp.where(lane >= shift, pltpu.roll(lf, shift, axis=1), 0.0)
            shift *= 2
        cum = lf + carry
        cum_ref[:, rs] = cum * LOG2E
        carry = cum[:, sub - 1:sub]

        def proj(w_ref, chunk, h=h):
            return jnp.dot(h, w_ref[:, chunk * DENSE_W:(chunk + 1) * DENSE_W], preferred_element_type=F32)

        def head_norm(acc, g_ref, out_ref, rs=rs):
            for hh in range(N_DENSE_HEADS):
                sl = slice(hh * DENSE_HEAD_DIM, (hh + 1) * DENSE_HEAD_DIM)
                out_ref[rs, sl] = _row_rms(acc[:, sl], g_ref[...]).astype(BF16)

        head_norm(proj(wa_ref, 0), gq_ref, fq_ref)
        head_norm(proj(wa_ref, 1), gk_ref, fk_ref)
        for c in range(2):
            sl = slice(c * DENSE_W, (c + 1) * DENSE_W)
            sg_ref[rs, sl] = _silu(proj(wb_ref, 3 + c)).astype(BF16)
        sq_ref[rs, :] = (proj(wb_ref, 0) * (DENSE_HEAD_DIM ** -0.5 * LOG2E)).astype(BF16)
        fv_ref[rs, :] = proj(wa_ref, 2).astype(BF16)
        sk_ref[rs, :] = proj(wb_ref, 1).astype(BF16)
        sv_ref[rs, :] = proj(wb_ref, 2).astype(BF16)
    carry_sc[...] = jnp.broadcast_to(carry, carry_sc.shape)


def _even_proj(x, g_norm, wa, wb, wf, b_f, g_q, g_k):
    B, S, D = x.shape
    tm = PROJ_ROWS
    row = lambda width: pl.BlockSpec((None, tm, width), lambda b, i: (b, i, 0))
    const = _resident_spec
    act = lambda width: jax.ShapeDtypeStruct((B, S, width), BF16)
    return pl.pallas_call(
        _even_proj_kernel,
        grid=(B, S // tm),
        in_specs=[row(D), const((1, D)), const(wa.shape), const(wb.shape), const(wf.shape),
                  const(b_f.shape), const(g_q.shape), const(g_k.shape)],
        out_specs=[row(DENSE_W)] * 6 + [row(2 * DENSE_W),
                                        pl.BlockSpec((None, SUBLANES, tm), lambda b, i: (b, 0, i))],
        out_shape=[act(DENSE_W)] * 6 + [act(2 * DENSE_W), jax.ShapeDtypeStruct((B, SUBLANES, S), F32)],
        scratch_shapes=[pltpu.VMEM((SUBLANES, LANES), F32)],
        compiler_params=_cparams(("arbitrary", "arbitrary")),
        name="even_proj",
    )(x, g_norm, wa, wb, wf, b_f, g_q, g_k)


def _fox_kernel(cend_ref, thr_ref, q_ref, k_ref, v_ref, c_ref, sg_ref, o_ref, m_sc, acc_sc, s_sc):
    h, i = pl.program_id(0), pl.program_id(1)
    nbatch, tq = q_ref.shape[0], q_ref.shape[1]
    tk = FOX_KEYS
    nsub = tq // tk
    batches = range(nbatch)
    c_base = [jnp.where(i > 0, cend_ref[b * N_DENSE_HEADS + h, jnp.maximum(i * nsub - 1, 0)], 0.0)
              for b in batches]

    def logits(b, start, width, rows):
        keys = pl.ds(pl.multiple_of(start, width), width)
        s = lax.dot_general(q_ref[b, rows, :], k_ref[b, keys, :], _NT, preferred_element_type=F32)
        return s + (c_base[b] - c_ref[b, pl.ds(h, 1), keys])

    def update(b, s, start, width, rows, first=False):
        keys = pl.ds(pl.multiple_of(start, width), width)
        m_cur = jnp.broadcast_to(jnp.max(s, axis=-1, keepdims=True), (s.shape[0], LANES))
        m_new = m_cur if first else jnp.maximum(m_sc[b, rows, :], m_cur)
        p = jnp.exp2(s - jnp.concatenate([m_new] * (width // LANES), axis=1))
        v_ones = jnp.concatenate([v_ref[b, keys, :], jnp.ones((width, LANES), BF16)], axis=1)
        pv = jnp.dot(p.astype(BF16), v_ones, preferred_element_type=F32)
        if first:
            acc_sc[b, rows, :] = pv
        else:
            alpha = jnp.exp2(m_sc[b, rows, :] - m_new)
            acc_sc[b, rows, :] = jnp.concatenate([alpha, alpha], axis=1) * acc_sc[b, rows, :] + pv
        m_sc[b, rows, :] = m_new

    for dd in range(nsub):
        rows = slice(dd * tk, tq)
        for b in batches:
            s = logits(b, i * tq + dd * tk, tk, rows)
            r_io = lax.broadcasted_iota(jnp.int32, s.shape, 0)
            c_io = lax.broadcasted_iota(jnp.int32, s.shape, 1)
            update(b, jnp.where(c_io <= r_io, s, -jnp.inf), i * tq + dd * tk, tk, rows, first=dd == 0)

    reach = thr_ref[0] - jnp.min(m_sc[...])

    def wanted(kt):
        ktc = jnp.maximum(kt, 0)
        far = c_base[0] - cend_ref[h, ktc]
        for b in batches[1:]:
            far = jnp.maximum(far, c_base[b] - cend_ref[b * N_DENSE_HEADS + h, ktc])
        return jnp.logical_and(kt >= 0, far >= -reach)

    everything = slice(0, tq)
    kt0 = i * nsub - 1
    for b in batches:
        s_sc[b] = logits(b, jnp.maximum(kt0, 0) * tk, tk, everything)

    def body(c):
        kt, _ = c
        for b in batches:
            s = s_sc[b]
            s_sc[b] = logits(b, jnp.maximum(kt - 1, 0) * tk, tk, everything)
            update(b, s, kt * tk, tk, everything)
        return kt - 1, wanted(kt - 1)

    lax.while_loop(lambda c: c[1], body, (kt0, wanted(kt0)))
    for b in batches:
        o_ref[b] = (acc_sc[b, :, :LANES] * (1.0 / acc_sc[b, :, LANES:])
                    * sg_ref[b].astype(F32)).astype(BF16)


def _fox_attn(cend, thr, fq, fk, fv, cum, sgate):
    B, S, _ = fq.shape
    tq = ATT_ROWS
    tile = pl.BlockSpec((B, tq, DENSE_HEAD_DIM), lambda h, i: (0, i, h))
    full = pl.BlockSpec((B, S, DENSE_HEAD_DIM), lambda h, i: (0, 0, h))
    smem = pl.BlockSpec(memory_space=pltpu.SMEM)
    return pl.pallas_call(
        _fox_kernel,
        grid=(N_DENSE_HEADS, S // tq),
        in_specs=[smem, smem, tile, full, full,
                  pl.BlockSpec((B, SUBLANES, S), lambda h, i: (0, 0, 0)), tile],
        out_specs=tile,
        out_shape=jax.ShapeDtypeStruct((B, S, DENSE_W), BF16),
        scratch_shapes=[pltpu.VMEM((B, tq, LANES), F32), pltpu.VMEM((B, tq, 2 * LANES), F32),
                        pltpu.VMEM((B, tq, FOX_KEYS), F32)],
        compiler_params=_cparams(("arbitrary", "arbitrary")),
        name="fox_attn",
    )(cend, thr, fq, fk, fv, cum, sgate)


def _sb_kernel(q_ref, k_ref, v_ref, u_ref, sg_ref, o_ref, off_sc, acc_sc):
    i = pl.program_id(1)
    nbatch, tq = q_ref.shape[0], q_ref.shape[1]
    r = SB_BLOCK
    nblk = tq // r
    chains = [(b, rb) for b in range(nbatch) for rb in range(nblk)]
    off_sc[...] = jnp.zeros_like(off_sc)
    acc_sc[...] = jnp.zeros_like(acc_sc)
    r_io = lax.broadcasted_iota(jnp.int32, (r, r), 0)
    c_io = lax.broadcasted_iota(jnp.int32, (r, r), 1)
    strict = c_io < r_io

    def walk(j, diagonal):
        rows = [slice(rb * r, (rb + 1) * r) for _, rb in chains]
        kts = [i * nblk + rb - j for _, rb in chains]
        keys = [pl.ds(pl.multiple_of(jnp.maximum(kt, 0) * r, r), r) for kt in kts]
        zs = [lax.dot_general(q_ref[b, rows[c], :], k_ref[b, keys[c], :], _NT, preferred_element_type=F32)
              for c, (b, _) in enumerate(chains)]
        pairs, offs, laters = [], [], []
        for c, (b, _) in enumerate(chains):
            log_beta, log_om = _log2_sigmoid_pair(zs[c])
            off = off_sc[b, rows[c], :]
            if diagonal:
                log_om = jnp.where(strict, log_om, 0.0)
            else:
                off = jnp.where(kts[c] < 0, -1e30, off)
            pairs.append((log_beta, log_om))
            offs.append(off)
            laters.append(jnp.dot(log_om.astype(BF16), u_ref[...], preferred_element_type=F32)
                          + jnp.concatenate([off] * (r // LANES), axis=1))
        for c, (b, _) in enumerate(chains):
            log_beta, log_om = pairs[c]
            w = jnp.exp2(log_beta + laters[c])
            if diagonal:
                w = jnp.where(strict, w, 0.0)
            acc_sc[b, rows[c], :] += jnp.dot(w.astype(BF16), v_ref[b, keys[c], :],
                                             preferred_element_type=F32)
            off_sc[b, rows[c], :] = offs[c] + jnp.sum(log_om, axis=-1, keepdims=True)

    walk(0, True)

    def body(c):
        j, _ = c
        walk(j, False)
        alive = jnp.max(off_sc[...]) >= -(EXP_ZERO + 8.0) * LOG2E
        keys_left = i * nblk + nblk - 1 - (j + 1) >= 0
        return j + 1, jnp.logical_and(keys_left, alive)

    lax.while_loop(lambda c: c[1], body, (1, True))
    for b in range(nbatch):
        o_ref[b] = (acc_sc[b] * sg_ref[b].astype(F32)).astype(BF16)


def _sb_attn(sq, sk, sv, upper, sgate):
    B, S, _ = sq.shape
    tq, tk = ATT_ROWS, SB_BLOCK
    tile = pl.BlockSpec((B, tq, DENSE_HEAD_DIM), lambda h, i: (0, i, h))
    gate_tile = pl.BlockSpec((B, tq, DENSE_HEAD_DIM), lambda h, i: (0, i, N_DENSE_HEADS + h))
    full = pl.BlockSpec((B, S, DENSE_HEAD_DIM), lambda h, i: (0, 0, h))
    return pl.pallas_call(
        _sb_kernel,
        grid=(N_DENSE_HEADS, S // tq),
        in_specs=[tile, full, full, _resident_spec((tk, tk)), gate_tile],
        out_specs=tile,
        out_shape=jax.ShapeDtypeStruct((B, S, DENSE_W), BF16),
        scratch_shapes=[pltpu.VMEM((B, tq, LANES), F32), pltpu.VMEM((B, tq, DENSE_HEAD_DIM), F32)],
        compiler_params=_cparams(("arbitrary", "arbitrary")),
        name="sb_attn",
    )(sq, sk, sv, upper, sgate)


def _odd_out_kernel(x_ref, m_ref, w_ref, o_ref):
    o_ref[...] = x_ref[...] + jnp.dot(m_ref[...], w_ref[...], preferred_element_type=F32)


def _odd_out(x, mixed, wo):
    B, S, D = x.shape
    tm = ODD_OUT_ROWS
    row = lambda width: pl.BlockSpec((None, tm, width), lambda b, i: (b, i, 0))
    return pl.pallas_call(
        _odd_out_kernel,
        grid=(B, S // tm),
        in_specs=[row(D), row(DIL_GROUP_W), _resident_spec(wo.shape)],
        out_specs=row(D),
        out_shape=jax.ShapeDtypeStruct((B, S, D), F32),
        compiler_params=_cparams(("arbitrary", "arbitrary")),
        name="odd_out",
    )(x, mixed, wo)


def _odd_proj_kernel(x_ref, mf_ref, ms_ref, wof_ref, wos_ref, gn_ref, w_ref, gq_ref, gk_ref, res_ref, *rest):
    out_refs, sg_ref, h_sc, h4_sc = rest[:9], rest[9], rest[10], rest[11]
    rows = x_ref.shape[0]
    n_slab = D_MODEL // LANES
    res = (x_ref[...]
           + jnp.dot(mf_ref[...], wof_ref[...], preferred_element_type=F32)
           + jnp.dot(ms_ref[...], wos_ref[...], preferred_element_type=F32))
    res_ref[...] = res
    hf = _row_rms(res, gn_ref[...])
    h = hf.astype(BF16)
    for cb in range(n_slab):
        h_sc[cb] = hf[:, cb * LANES:(cb + 1) * LANES]

    per4 = rows // 4
    for r4 in range(4):
        for cb in range(n_slab):
            h4_sc[cb, r4 * per4:(r4 + 1) * per4, :] = h_sc[cb, pl.ds(r4, per4, stride=4), :]
    lhs4 = jnp.concatenate([h4_sc[cb] for cb in range(n_slab)], axis=1).astype(BF16)
    lhs16 = jnp.concatenate(
        [jnp.concatenate([h4_sc[cb, pl.ds((r % 4) * per4 + r // 4, per4 // 4, stride=4), :]
                          for cb in range(n_slab)], axis=1) for r in range(16)], axis=0).astype(BF16)
    assert DILATIONS == (1, 4, 16)
    lhs = [h, lhs4, lhs16]

    lane_lo = lax.broadcasted_iota(jnp.int32, (rows, LANES), 1) < DIL_HEAD_DIM

    def head_norm(acc, g_ref):
        cols = []
        for cb in range(DIL_GROUP_W // LANES):
            x = acc[:, cb * LANES:(cb + 1) * LANES]
            x2 = x * x
            lo = jnp.sum(jnp.where(lane_lo, x2, 0.0), axis=-1, keepdims=True)
            hi = jnp.sum(jnp.where(lane_lo, 0.0, x2), axis=-1, keepdims=True)
            ms = jnp.where(lane_lo, lo, hi) * (1.0 / DIL_HEAD_DIM)
            cols.append(x * lax.rsqrt(ms + RMS_EPS))
        return jnp.concatenate(cols, axis=1) * g_ref[...]

    gate = jnp.dot(h, w_ref[:, 9 * DIL_GROUP_W:10 * DIL_GROUP_W], preferred_element_type=F32)
    sg_ref[...] = _silu(gate).astype(BF16)
    for c in sorted(range(9), key=lambda c: (c % 3, c)):
        kind, g = divmod(c, 3)
        acc = jnp.dot(lhs[g], w_ref[:, c * DIL_GROUP_W:(c + 1) * DIL_GROUP_W],
                      preferred_element_type=F32)
        if kind == 0:
            acc = head_norm(acc, gq_ref)
        elif kind == 1:
            acc = head_norm(acc, gk_ref)
        dil = DILATIONS[g]
        per = rows // dil
        for r in range(dil):
            out_refs[c][r] = acc[r * per:(r + 1) * per, :].astype(BF16)


def _odd_proj(x, mixed_f, mixed_s, wo_f, wo_s, g_norm, w, g_q, g_k):
    B, S, D = x.shape
    tm = ODD_PROJ_ROWS
    const = _resident_spec
    row = lambda width: pl.BlockSpec((None, tm, width), lambda b, i: (b, i, 0))
    out_specs, out_shape = [row(D)], [jax.ShapeDtypeStruct((B, S, D), F32)]
    for _ in range(3):
        for dil in DILATIONS:
            out_specs.append(pl.BlockSpec((None, dil, tm // dil, DIL_GROUP_W), lambda b, i: (b, 0, i, 0)))
            out_shape.append(jax.ShapeDtypeStruct((B, dil, S // dil, DIL_GROUP_W), BF16))
    out_specs.append(pl.BlockSpec((None, tm, DIL_GROUP_W), lambda b, i: (b, i, 0)))
    out_shape.append(jax.ShapeDtypeStruct((B, S, DIL_GROUP_W), BF16))
    return pl.pallas_call(
        _odd_proj_kernel,
        grid=(B, S // tm),
        in_specs=[row(D), row(DENSE_W), row(DENSE_W), const(wo_f.shape), const(wo_s.shape),
                  const((1, D)), const(w.shape), const(g_q.shape), const(g_k.shape)],
        out_specs=out_specs,
        out_shape=out_shape,
        scratch_shapes=[pltpu.VMEM((D // LANES, tm, LANES), F32)] * 2,
        compiler_params=_cparams(("arbitrary", "arbitrary")),
        name="odd_proj",
    )(x, mixed_f, mixed_s, wo_f, wo_s, g_norm, w, g_q, g_k)


def _dil_kernel(slope_ref, *refs):
    ins, (sg_ref, o_ref), (o_sc, m_sc, l_sc) = refs[:15], refs[15:17], refs[17:]
    hp, st = pl.program_id(1), pl.program_id(2)
    blk = DIL_BLK
    quarter = DIL_POS // 4

    a_io = lax.broadcasted_iota(jnp.int32, (blk, 2 * blk), 0)
    c_io = lax.broadcasted_iota(jnp.int32, (blk, 2 * blk), 1)
    dist = a_io - c_io + blk
    band = jnp.logical_and(dist >= 0, dist <= DIL_SPAN)
    neg = jnp.where(band, 0.0, -jnp.inf)
    neg_first = jnp.where(jnp.logical_and(band, c_io >= blk), 0.0, -jnp.inf)
    dist_f = dist.astype(F32)
    lane_lo = lax.broadcasted_iota(jnp.int32, (blk, LANES), 1) < DIL_HEAD_DIM

    for g, dil in enumerate(DILATIONS):
        q_ref, kc_ref, kh_ref, vc_ref, vh_ref = ins[5 * g:5 * g + 5]
        nb = DIL_POS // dil // blk
        alibi = [dist_f * (slope_ref[g * N_DIL_HEADS + 2 * hp + hd] * float(dil)) for hd in range(2)]
        bias = jnp.concatenate([neg - alibi[0], neg - alibi[1]], axis=0)
        bias_first = jnp.concatenate([neg_first - alibi[0], neg_first - alibi[1]], axis=0)
        ones = jnp.ones((2 * blk, LANES), BF16)

        for r in range(dil):
            for n in range(nb):
                q2 = q_ref[r, n * blk:(n + 1) * blk, :]
                if n == 0:
                    k2 = jnp.concatenate([kh_ref[r], kc_ref[r, 0:blk, :]], axis=0)
                    v2 = jnp.concatenate([vh_ref[r], vc_ref[r, 0:blk, :]], axis=0)
                    b2 = jnp.where(st == 0, bias_first, bias)
                else:
                    k2 = kc_ref[r, (n - 1) * blk:(n + 1) * blk, :]
                    v2 = vc_ref[r, (n - 1) * blk:(n + 1) * blk, :]
                    b2 = bias
                zero = jnp.zeros_like(q2)
                q_st = jnp.concatenate([jnp.where(lane_lo, q2, zero), jnp.where(lane_lo, zero, q2)], axis=0)
                s = lax.dot_general(q_st, k2, _NT, preferred_element_type=F32) + b2
                m_st = jnp.max(s, axis=-1, keepdims=True)
                p = jnp.exp2(s - m_st)
                ol = jnp.dot(p.astype(BF16), jnp.concatenate([v2, ones], axis=1), preferred_element_type=F32)
                o = jnp.where(lane_lo, ol[:blk, :LANES], ol[blk:, :LANES])
                m = jnp.where(lane_lo, m_st[:blk], m_st[blk:])
                l = jnp.where(lane_lo, ol[:blk, LANES:], ol[blk:, LANES:])
                pos = (n * blk) * dil + r
                if dil == 1:
                    dst = pl.ds(pos, blk)
                else:
                    dst = pl.ds((pos % 4) * quarter + pos // 4, blk, stride=dil // 4)
                o_sc[g, dst, :] = o
                m_sc[g, dst, :] = m
                l_sc[g, dst, :] = l

    def merge(ci, carry):
        res, idx = ci // (quarter // blk), (ci % (quarter // blk)) * blk
        rows = ([pl.ds(4 * idx + res, blk, stride=4)]
                + [pl.ds(pl.multiple_of(res * quarter + idx, blk), blk)] * 2)
        ms = [m_sc[g, rows[g], :] for g in range(3)]
        m_all = jnp.maximum(jnp.maximum(ms[0], ms[1]), ms[2])
        num = jnp.zeros((blk, LANES), F32)
        den = jnp.zeros((blk, LANES), F32)
        for g in range(3):
            e = jnp.exp2(ms[g] - m_all)
            num = num + e * o_sc[g, rows[g], :]
            den = den + e * l_sc[g, rows[g], :]
        o_sc[0, rows[0], :] = num / den
        return carry

    lax.fori_loop(0, DIL_POS // blk, merge, 0, unroll=2)

    chunk = 256

    def finish(ci, carry):
        rows = pl.ds(pl.multiple_of(ci * chunk, chunk), chunk)
        o_ref[rows, :] = (o_sc[0, rows, :] * sg_ref[rows, :].astype(F32)).astype(BF16)
        return carry

    lax.fori_loop(0, DIL_POS // chunk, finish, 0)


def _dil_attn(slopes, qkv, sgate):
    B, S, _ = sgate.shape
    n_pairs = N_DIL_HEADS * DIL_HEAD_DIM // LANES
    blk = DIL_BLK
    in_specs = [pl.BlockSpec(memory_space=pltpu.SMEM)]
    args = [slopes]
    for g, dil in enumerate(DILATIONS):
        length = DIL_POS // dil
        nb = length // blk
        cur = pl.BlockSpec((None, dil, length, LANES), lambda b, hp, st: (b, 0, st, hp))
        halo = pl.BlockSpec((None, dil, blk, LANES),
                            lambda b, hp, st, nb=nb: (b, 0, jnp.maximum(st * nb - 1, 0), hp))
        q, k, v = qkv[g], qkv[3 + g], qkv[6 + g]
        in_specs += [cur, cur, halo, cur, halo]
        args += [q, k, k, v, v]
    pos = pl.BlockSpec((None, DIL_POS, LANES), lambda b, hp, st: (b, st, hp))
    in_specs.append(pos)
    args.append(sgate)
    return pl.pallas_call(
        _dil_kernel,
        grid=(B, n_pairs, S // DIL_POS),
        in_specs=in_specs,
        out_specs=pos,
        out_shape=jax.ShapeDtypeStruct((B, S, DIL_GROUP_W), BF16),
        scratch_shapes=[pltpu.VMEM((3, DIL_POS, LANES), F32)] * 3,
        compiler_params=_cparams(("arbitrary", "arbitrary", "arbitrary")),
        name="dil_attn",
    )(*args)


def _even_mixers(x, g_norm, w_in, b_f, g_q, g_k):
    B, S, D = x.shape
    n_f = N_DENSE_HEADS
    cut0, cut1 = 3 * DENSE_W, 3 * DENSE_W + n_f
    wa, wb = w_in[:, :cut0].astype(BF16), w_in[:, cut1:].astype(BF16)
    wf = jnp.zeros((BF16_SUBLANES, D), F32).at[:n_f].set(w_in[:, cut0:cut1].T).astype(BF16)
    bf = jnp.zeros((SUBLANES, LANES), F32).at[:n_f].set(jnp.broadcast_to(b_f[:, None], (n_f, LANES)))
    gq = (g_q * (DENSE_HEAD_DIM ** -0.5 * LOG2E)).reshape(1, DENSE_HEAD_DIM)
    gk = g_k.reshape(1, DENSE_HEAD_DIM)
    fq, fk, fv, sq, sk, sv, sgate, cum = _even_proj(x, g_norm.reshape(1, D), wa, wb, wf, bf, gq, gk)

    t = FOX_KEYS
    cend = cum[:, :n_f, t - 1::t].reshape(B * n_f, S // t)
    qk_bound = math.sqrt(DENSE_HEAD_DIM) * jnp.max(jnp.abs(g_q)) * jnp.max(jnp.abs(g_k)) * 1.02
    thr = ((EXP_ZERO + 1.0 + qk_bound) * LOG2E).reshape(1).astype(F32)
    mixed_f = _fox_attn(cend, thr, fq, fk, fv, cum, sgate)

    idx = jnp.arange(SB_BLOCK)
    upper = (idx[:, None] > idx[None, :]).astype(BF16)
    mixed_s = _sb_attn(sq, sk, sv, upper, sgate)
    return mixed_f, mixed_s


def _odd_layer(x, mixed_f, mixed_s, even_w_out, g_norm, w_in, g_q, g_k, w_out):
    B, S, D = x.shape
    wo_f, wo_s = even_w_out[:DENSE_W].astype(BF16), even_w_out[DENSE_W:].astype(BF16)
    w = w_in.astype(BF16)
    gq = jnp.tile(g_q * (DIL_HEAD_DIM ** -0.5 * LOG2E), N_DIL_HEADS).reshape(1, DIL_GROUP_W)
    gk = jnp.tile(g_k, N_DIL_HEADS).reshape(1, DIL_GROUP_W)
    outs = _odd_proj(x, mixed_f, mixed_s, wo_f, wo_s, g_norm.reshape(1, D), w, gq, gk)
    n_all = len(DILATIONS) * N_DIL_HEADS
    slopes = jnp.asarray([LOG2E * 2.0 ** (-8.0 * (i + 1) / n_all) for i in range(n_all)], F32)
    mixed = _dil_attn(slopes, outs[1:10], outs[10])
    return _odd_out(outs[0], mixed, w_out.astype(BF16))


def kernel(x, even_norm, even_w_in, even_b_f, even_q_gain, even_k_gain, even_w_out,
           odd_norm, odd_w_in, odd_q_gain, odd_k_gain, odd_w_out):
    assert x.shape[1] % DIL_POS == 0 and x.shape[2] == D_MODEL
    mixed_f, mixed_s = _even_mixers(x, even_norm[0], even_w_in[0], even_b_f[0], even_q_gain[0],
                                    even_k_gain[0])
    return _odd_layer(x, mixed_f, mixed_s, even_w_out[0], odd_norm[0], odd_w_in[0], odd_q_gain[0],
                      odd_k_gain[0], odd_w_out[0])
```

```python
import math

import jax
import jax.numpy as jnp
from jax import lax
from jax.experimental import pallas as pl
from jax.experimental.pallas import tpu as pltpu

F32 = jnp.float32
BF16 = jnp.bfloat16

D_MODEL = 1024
DENSE_HEAD_DIM = 128
N_DENSE_HEADS = 4
DENSE_W = N_DENSE_HEADS * DENSE_HEAD_DIM
DIL_HEAD_DIM = 64
N_DIL_HEADS = 8
DIL_GROUP_W = N_DIL_HEADS * DIL_HEAD_DIM
DILATIONS = (1, 4, 16)
DIL_SPAN = 128
DIL_BLK = 128
RMS_EPS = 1e-6

LANES = 128
SUBLANES = 8
BF16_SUBLANES = 16
PROJ_ROWS = 1024
ODD_OUT_ROWS = 2048
PROJ_SUB_ROWS = 512
ODD_PROJ_ROWS = 512
ATT_ROWS = 512
FOX_KEYS = 512
SB_ROWS = 1024
SB_BLOCK = 256
DIL_POS = DILATIONS[-1] * DIL_BLK
VMEM_LIMIT = 56 * 1024 * 1024

EXP_ZERO = 104.0
LOG2E = math.log2(math.e)

_NT = (((1,), (1,)), ((), ()))


def _cparams(sem):
    return pltpu.CompilerParams(dimension_semantics=sem, vmem_limit_bytes=VMEM_LIMIT)


def _resident_spec(shape):
    return pl.BlockSpec(shape, lambda *_: (0,) * len(shape), pipeline_mode=pl.Buffered(1))


def _log_sigmoid(z):
    return jnp.minimum(z, 0.0) - jnp.log(1.0 + jnp.exp(-jnp.abs(z)))


def _log2_sigmoid_pair(z2):
    soft = jnp.log2(1.0 + jnp.exp2(-jnp.abs(z2)))
    log_beta = jnp.minimum(z2, 0.0) - soft
    return log_beta, log_beta - z2


def _silu(x):
    return x * (1.0 / (1.0 + jnp.exp(-x)))


def _row_rms(x, g):
    ms = jnp.mean(x * x, axis=-1, keepdims=True)
    return x * lax.rsqrt(ms + RMS_EPS) * g


def _even_proj_kernel(x_ref, gn_ref, wa_ref, wb_ref, wf_ref, bf_ref, gq_ref, gk_ref,
                      fq_ref, fk_ref, fv_ref, sq_ref, sk_ref, sv_ref, sg_ref, cum_ref, carry_sc):
    i = pl.program_id(1)
    rows = x_ref.shape[0]
    sub = PROJ_SUB_ROWS
    lane = lax.broadcasted_iota(jnp.int32, (SUBLANES, sub), 1)

    @pl.when(i == 0)
    def _():
        carry_sc[...] = jnp.zeros_like(carry_sc)

    carry = carry_sc[:, 0:1]
    for r0 in range(0, rows, sub):
        rs = slice(r0, r0 + sub)
        h = _row_rms(x_ref[rs, :], gn_ref[...]).astype(BF16)

        fl = lax.dot_general(wf_ref[...], h, _NT, preferred_element_type=F32)[:SUBLANES] + bf_ref[:, 0:1]
        lf = _log_sigmoid(fl)
        shift = 1
        while shift < sub:
            lf = lf + jnp.where(lane >= shift, pltpu.roll(lf, shift, axis=1), 0.0)
            shift *= 2
        cum = lf + carry
        cum_ref[:, rs] = cum * LOG2E
        carry = cum[:, sub - 1:sub]

        def proj(w_ref, chunk, h=h):
            return jnp.dot(h, w_ref[:, chunk * DENSE_W:(chunk + 1) * DENSE_W], preferred_element_type=F32)

        def head_norm(acc, g_ref, out_ref, rs=rs):
            for hh in range(N_DENSE_HEADS):
                sl = slice(hh * DENSE_HEAD_DIM, (hh + 1) * DENSE_HEAD_DIM)
                out_ref[rs, sl] = _row_rms(acc[:, sl], g_ref[...]).astype(BF16)

        head_norm(proj(wa_ref, 0), gq_ref, fq_ref)
        head_norm(proj(wa_ref, 1), gk_ref, fk_ref)
        for c in range(2):
            sl = slice(c * DENSE_W, (c + 1) * DENSE_W)
            sg_ref[rs, sl] = _silu(proj(wb_ref, 3 + c)).astype(BF16)
        sq_ref[rs, :] = (proj(wb_ref, 0) * (DENSE_HEAD_DIM ** -0.5 * LOG2E)).astype(BF16)
        fv_ref[rs, :] = proj(wa_ref, 2).astype(BF16)
        sk_ref[rs, :] = proj(wb_ref, 1).astype(BF16)
        sv_ref[rs, :] = proj(wb_ref, 2).astype(BF16)
    carry_sc[...] = jnp.broadcast_to(carry, carry_sc.shape)


def _even_proj(x, g_norm, wa, wb, wf, b_f, g_q, g_k):
    B, S, D = x.shape
    tm = PROJ_ROWS
    row = lambda width: pl.BlockSpec((None, tm, width), lambda b, i: (b, i, 0))
    const = _resident_spec
    act = lambda width: jax.ShapeDtypeStruct((B, S, width), BF16)
    return pl.pallas_call(
        _even_proj_kernel,
        grid=(B, S // tm),
        in_specs=[row(D), const((1, D)), const(wa.shape), const(wb.shape), const(wf.shape),
                  const(b_f.shape), const(g_q.shape), const(g_k.shape)],
        out_specs=[row(DENSE_W)] * 6 + [row(2 * DENSE_W),
                                        pl.BlockSpec((None, SUBLANES, tm), lambda b, i: (b, 0, i))],
        out_shape=[act(DENSE_W)] * 6 + [act(2 * DENSE_W), jax.ShapeDtypeStruct((B, SUBLANES, S), F32)],
        scratch_shapes=[pltpu.VMEM((SUBLANES, LANES), F32)],
        compiler_params=_cparams(("arbitrary", "arbitrary")),
        name="even_proj",
    )(x, g_norm, wa, wb, wf, b_f, g_q, g_k)


def _fox_kernel(cend_ref, thr_ref, q_ref, k_ref, v_ref, c_ref, sg_ref, o_ref, m_sc, acc_sc, s_sc):
    h, i = pl.program_id(0), pl.program_id(1)
    nbatch, tq = q_ref.shape[0], q_ref.shape[1]
    tk = FOX_KEYS
    nsub = tq // tk
    batches = range(nbatch)
    c_base = [jnp.where(i > 0, cend_ref[b * N_DENSE_HEADS + h, jnp.maximum(i * nsub - 1, 0)], 0.0)
              for b in batches]

    def logits(b, start, width, rows):
        keys = pl.ds(pl.multiple_of(start, width), width)
        s = lax.dot_general(q_ref[b, rows, :], k_ref[b, keys, :], _NT, preferred_element_type=F32)
        return s + (c_base[b] - c_ref[b, pl.ds(h, 1), keys])

    def update(b, s, start, width, rows, first=False):
        keys = pl.ds(pl.multiple_of(start, width), width)
        m_cur = jnp.broadcast_to(jnp.max(s, axis=-1, keepdims=True), (s.shape[0], LANES))
        m_new = m_cur if first else jnp.maximum(m_sc[b, rows, :], m_cur)
        p = jnp.exp2(s - jnp.concatenate([m_new] * (width // LANES), axis=1))
        v_ones = jnp.concatenate([v_ref[b, keys, :], jnp.ones((width, LANES), BF16)], axis=1)
        pv = jnp.dot(p.astype(BF16), v_ones, preferred_element_type=F32)
        if first:
            acc_sc[b, rows, :] = pv
        else:
            alpha = jnp.exp2(m_sc[b, rows, :] - m_new)
            acc_sc[b, rows, :] = jnp.concatenate([alpha, alpha], axis=1) * acc_sc[b, rows, :] + pv
        m_sc[b, rows, :] = m_new

    for dd in range(nsub):
        rows = slice(dd * tk, tq)
        for b in batches:
            s = logits(b, i * tq + dd * tk, tk, rows)
            r_io = lax.broadcasted_iota(jnp.int32, s.shape, 0)
            c_io = lax.broadcasted_iota(jnp.int32, s.shape, 1)
            update(b, jnp.where(c_io <= r_io, s, -jnp.inf), i * tq + dd * tk, tk, rows, first=dd == 0)

    reach = thr_ref[0] - jnp.min(m_sc[...])

    def wanted(kt):
        ktc = jnp.maximum(kt, 0)
        far = c_base[0] - cend_ref[h, ktc]
        for b in batches[1:]:
            far = jnp.maximum(far, c_base[b] - cend_ref[b * N_DENSE_HEADS + h, ktc])
        return jnp.logical_and(kt >= 0, far >= -reach)

    everything = slice(0, tq)
    kt0 = i * nsub - 1
    for b in batches:
        s_sc[b] = logits(b, jnp.maximum(kt0, 0) * tk, tk, everything)

    def body(c):
        kt, _ = c
        for b in batches:
            s = s_sc[b]
            s_sc[b] = logits(b, jnp.maximum(kt - 1, 0) * tk, tk, everything)
            update(b, s, kt * tk, tk, everything)
        return kt - 1, wanted(kt - 1)

    lax.while_loop(lambda c: c[1], body, (kt0, wanted(kt0)))
    for b in batches:
        o_ref[b] = (acc_sc[b, :, :LANES] * (1.0 / acc_sc[b, :, LANES:])
                    * sg_ref[b].astype(F32)).astype(BF16)


def _fox_attn(cend, thr, fq, fk, fv, cum, sgate):
    B, S, _ = fq.shape
    tq = ATT_ROWS
    tile = pl.BlockSpec((B, tq, DENSE_HEAD_DIM), lambda h, i: (0, i, h))
    full = pl.BlockSpec((B, S, DENSE_HEAD_DIM), lambda h, i: (0, 0, h))
    smem = pl.BlockSpec(memory_space=pltpu.SMEM)
    return pl.pallas_call(
        _fox_kernel,
        grid=(N_DENSE_HEADS, S // tq),
        in_specs=[smem, smem, tile, full, full,
                  pl.BlockSpec((B, SUBLANES, S), lambda h, i: (0, 0, 0)), tile],
        out_specs=tile,
        out_shape=jax.ShapeDtypeStruct((B, S, DENSE_W), BF16),
        scratch_shapes=[pltpu.VMEM((B, tq, LANES), F32), pltpu.VMEM((B, tq, 2 * LANES), F32),
                        pltpu.VMEM((B, tq, FOX_KEYS), F32)],
        compiler_params=_cparams(("arbitrary", "arbitrary")),
        name="fox_attn",
    )(cend, thr, fq, fk, fv, cum, sgate)


def _sb_kernel(q_ref, k_ref, v_ref, u_ref, sg_ref, o_ref, off_sc, acc_sc):
    i = pl.program_id(1)
    nbatch, tq = q_ref.shape[0], q_ref.shape[1]
    r = SB_BLOCK
    nblk = tq // r
    chains = [(b, rb) for b in range(nbatch) for rb in range(nblk)]
    off_sc[...] = jnp.zeros_like(off_sc)
    acc_sc[...] = jnp.zeros_like(acc_sc)
    r_io = lax.broadcasted_iota(jnp.int32, (r, r), 0)
    c_io = lax.broadcasted_iota(jnp.int32, (r, r), 1)
    strict = c_io < r_io

    def walk(j, diagonal):
        rows = [slice(rb * r, (rb + 1) * r) for _, rb in chains]
        kts = [i * nblk + rb - j for _, rb in chains]
        keys = [pl.ds(pl.multiple_of(jnp.maximum(kt, 0) * r, r), r) for kt in kts]
        zs = [lax.dot_general(q_ref[b, rows[c], :], k_ref[b, keys[c], :], _NT, preferred_element_type=F32)
              for c, (b, _) in enumerate(chains)]
        pairs, offs, laters = [], [], []
        for c, (b, _) in enumerate(chains):
            log_beta, log_om = _log2_sigmoid_pair(zs[c])
            off = off_sc[b, rows[c], :]
            if diagonal:
                log_om = jnp.where(strict, log_om, 0.0)
            else:
                off = jnp.where(kts[c] < 0, -1e30, off)
            pairs.append((log_beta, log_om))
            offs.append(off)
            laters.append(jnp.dot(log_om.astype(BF16), u_ref[...], preferred_element_type=F32)
                          + jnp.concatenate([off] * (r // LANES), axis=1))
        for c, (b, _) in enumerate(chains):
            log_beta, log_om = pairs[c]
            w = jnp.exp2(log_beta + laters[c])
            if diagonal:
                w = jnp.where(strict, w, 0.0)
            acc_sc[b, rows[c], :] += jnp.dot(w.astype(BF16), v_ref[b, keys[c], :],
                                             preferred_element_type=F32)
            off_sc[b, rows[c], :] = offs[c] + jnp.sum(log_om, axis=-1, keepdims=True)

    walk(0, True)

    def body(c):
        j, _ = c
        walk(j, False)
        alive = jnp.max(off_sc[...]) >= -(EXP_ZERO + 8.0) * LOG2E
        keys_left = i * nblk + nblk - 1 - (j + 1) >= 0
        return j + 1, jnp.logical_and(keys_left, alive)

    lax.while_loop(lambda c: c[1], body, (1, True))
    for b in range(nbatch):
        o_ref[b] = (acc_sc[b] * sg_ref[b].astype(F32)).astype(BF16)


def _sb_attn(sq, sk, sv, upper, sgate):
    B, S, _ = sq.shape
    tq, tk = SB_ROWS, SB_BLOCK
    tile = pl.BlockSpec((B, tq, DENSE_HEAD_DIM), lambda h, i: (0, i, h))
    gate_tile = pl.BlockSpec((B, tq, DENSE_HEAD_DIM), lambda h, i: (0, i, N_DENSE_HEADS + h))
    full = pl.BlockSpec((B, S, DENSE_HEAD_DIM), lambda h, i: (0, 0, h))
    return pl.pallas_call(
        _sb_kernel,
        grid=(N_DENSE_HEADS, S // tq),
        in_specs=[tile, full, full, _resident_spec((tk, tk)), gate_tile],
        out_specs=tile,
        out_shape=jax.ShapeDtypeStruct((B, S, DENSE_W), BF16),
        scratch_shapes=[pltpu.VMEM((B, tq, LANES), F32), pltpu.VMEM((B, tq, DENSE_HEAD_DIM), F32)],
        compiler_params=_cparams(("arbitrary", "arbitrary")),
        name="sb_attn",
    )(sq, sk, sv, upper, sgate)


def _odd_out_kernel(x_ref, m_ref, w_ref, o_ref):
    o_ref[...] = x_ref[...] + jnp.dot(m_ref[...], w_ref[...], preferred_element_type=F32)


def _odd_out(x, mixed, wo):
    B, S, D = x.shape
    tm = ODD_OUT_ROWS
    row = lambda width: pl.BlockSpec((None, tm, width), lambda b, i: (b, i, 0))
    return pl.pallas_call(
        _odd_out_kernel,
        grid=(B, S // tm),
        in_specs=[row(D), row(DIL_GROUP_W), _resident_spec(wo.shape)],
        out_specs=row(D),
        out_shape=jax.ShapeDtypeStruct((B, S, D), F32),
        compiler_params=_cparams(("arbitrary", "arbitrary")),
        name="odd_out",
    )(x, mixed, wo)


def _odd_proj_kernel(x_ref, mf_ref, ms_ref, wof_ref, wos_ref, gn_ref, w_ref, gq_ref, gk_ref, res_ref, *rest):
    out_refs, sg_ref, h_sc, h4_sc = rest[:9], rest[9], rest[10], rest[11]
    rows = x_ref.shape[0]
    n_slab = D_MODEL // LANES
    res = (x_ref[...]
           + jnp.dot(mf_ref[...], wof_ref[...], preferred_element_type=F32)
           + jnp.dot(ms_ref[...], wos_ref[...], preferred_element_type=F32))
    res_ref[...] = res
    hf = _row_rms(res, gn_ref[...])
    h = hf.astype(BF16)
    for cb in range(n_slab):
        h_sc[cb] = hf[:, cb * LANES:(cb + 1) * LANES]

    per4 = rows // 4
    for r4 in range(4):
        for cb in range(n_slab):
            h4_sc[cb, r4 * per4:(r4 + 1) * per4, :] = h_sc[cb, pl.ds(r4, per4, stride=4), :]
    lhs4 = jnp.concatenate([h4_sc[cb] for cb in range(n_slab)], axis=1).astype(BF16)
    lhs16 = jnp.concatenate(
        [jnp.concatenate([h4_sc[cb, pl.ds((r % 4) * per4 + r // 4, per4 // 4, stride=4), :]
                          for cb in range(n_slab)], axis=1) for r in range(16)], axis=0).astype(BF16)
    assert DILATIONS == (1, 4, 16)
    lhs = [h, lhs4, lhs16]

    lane_lo = lax.broadcasted_iota(jnp.int32, (rows, LANES), 1) < DIL_HEAD_DIM

    def head_norm(acc, g_ref):
        cols = []
        for cb in range(DIL_GROUP_W // LANES):
            x = acc[:, cb * LANES:(cb + 1) * LANES]
            x2 = x * x
            lo = jnp.sum(jnp.where(lane_lo, x2, 0.0), axis=-1, keepdims=True)
            hi = jnp.sum(jnp.where(lane_lo, 0.0, x2), axis=-1, keepdims=True)
            ms = jnp.where(lane_lo, lo, hi) * (1.0 / DIL_HEAD_DIM)
            cols.append(x * lax.rsqrt(ms + RMS_EPS))
        return jnp.concatenate(cols, axis=1) * g_ref[...]

    gate = jnp.dot(h, w_ref[:, 9 * DIL_GROUP_W:10 * DIL_GROUP_W], preferred_element_type=F32)
    sg_ref[...] = _silu(gate).astype(BF16)
    for c in sorted(range(9), key=lambda c: (c % 3, c)):
        kind, g = divmod(c, 3)
        acc = jnp.dot(lhs[g], w_ref[:, c * DIL_GROUP_W:(c + 1) * DIL_GROUP_W],
                      preferred_element_type=F32)
        if kind == 0:
            acc = head_norm(acc, gq_ref)
        elif kind == 1:
            acc = head_norm(acc, gk_ref)
        dil = DILATIONS[g]
        per = rows // dil
        for r in range(dil):
            out_refs[c][r] = acc[r * per:(r + 1) * per, :].astype(BF16)


def _odd_proj(x, mixed_f, mixed_s, wo_f, wo_s, g_norm, w, g_q, g_k):
    B, S, D = x.shape
    tm = ODD_PROJ_ROWS
    const = _resident_spec
    row = lambda width: pl.BlockSpec((None, tm, width), lambda b, i: (b, i, 0))
    out_specs, out_shape = [row(D)], [jax.ShapeDtypeStruct((B, S, D), F32)]
    for _ in range(3):
        for dil in DILATIONS:
            out_specs.append(pl.BlockSpec((None, dil, tm // dil, DIL_GROUP_W), lambda b, i: (b, 0, i, 0)))
            out_shape.append(jax.ShapeDtypeStruct((B, dil, S // dil, DIL_GROUP_W), BF16))
    out_specs.append(pl.BlockSpec((None, tm, DIL_GROUP_W), lambda b, i: (b, i, 0)))
    out_shape.append(jax.ShapeDtypeStruct((B, S, DIL_GROUP_W), BF16))
    return pl.pallas_call(
        _odd_proj_kernel,
        grid=(B, S // tm),
        in_specs=[row(D), row(DENSE_W), row(DENSE_W), const(wo_f.shape), const(wo_s.shape),
                  const((1, D)), const(w.shape), const(g_q.shape), const(g_k.shape)],
        out_specs=out_specs,
        out_shape=out_shape,
        scratch_shapes=[pltpu.VMEM((D // LANES, tm, LANES), F32)] * 2,
        compiler_params=_cparams(("arbitrary", "arbitrary")),
        name="odd_proj",
    )(x, mixed_f, mixed_s, wo_f, wo_s, g_norm, w, g_q, g_k)


def _dil_kernel(slope_ref, *refs):
    ins, (sg_ref, o_ref), (o_sc, m_sc, l_sc) = refs[:15], refs[15:17], refs[17:]
    hp, st = pl.program_id(1), pl.program_id(2)
    blk = DIL_BLK
    quarter = DIL_POS // 4

    a_io = lax.broadcasted_iota(jnp.int32, (blk, 2 * blk), 0)
    c_io = lax.broadcasted_iota(jnp.int32, (blk, 2 * blk), 1)
    dist = a_io - c_io + blk
    band = jnp.logical_and(dist >= 0, dist <= DIL_SPAN)
    neg = jnp.where(band, 0.0, -jnp.inf)
    neg_first = jnp.where(jnp.logical_and(band, c_io >= blk), 0.0, -jnp.inf)
    dist_f = dist.astype(F32)
    lane_lo = lax.broadcasted_iota(jnp.int32, (blk, LANES), 1) < DIL_HEAD_DIM

    for g, dil in enumerate(DILATIONS):
        q_ref, kc_ref, kh_ref, vc_ref, vh_ref = ins[5 * g:5 * g + 5]
        nb = DIL_POS // dil // blk
        alibi = [dist_f * (slope_ref[g * N_DIL_HEADS + 2 * hp + hd] * float(dil)) for hd in range(2)]
        bias = jnp.concatenate([neg - alibi[0], neg - alibi[1]], axis=0)
        bias_first = jnp.concatenate([neg_first - alibi[0], neg_first - alibi[1]], axis=0)
        ones = jnp.ones((2 * blk, LANES), BF16)

        for r in range(dil):
            for n in range(nb):
                q2 = q_ref[r, n * blk:(n + 1) * blk, :]
                if n == 0:
                    k2 = jnp.concatenate([kh_ref[r], kc_ref[r, 0:blk, :]], axis=0)
                    v2 = jnp.concatenate([vh_ref[r], vc_ref[r, 0:blk, :]], axis=0)
                    b2 = jnp.where(st == 0, bias_first, bias)
                else:
                    k2 = kc_ref[r, (n - 1) * blk:(n + 1) * blk, :]
                    v2 = vc_ref[r, (n - 1) * blk:(n + 1) * blk, :]
                    b2 = bias
                zero = jnp.zeros_like(q2)
                q_st = jnp.concatenate([jnp.where(lane_lo, q2, zero), jnp.where(lane_lo, zero, q2)], axis=0)
                s = lax.dot_general(q_st, k2, _NT, preferred_element_type=F32) + b2
                m_st = jnp.max(s, axis=-1, keepdims=True)
                p = jnp.exp2(s - m_st)
                ol = jnp.dot(p.astype(BF16), jnp.concatenate([v2, ones], axis=1), preferred_element_type=F32)
                o = jnp.where(lane_lo, ol[:blk, :LANES], ol[blk:, :LANES])
                m = jnp.where(lane_lo, m_st[:blk], m_st[blk:])
                l = jnp.where(lane_lo, ol[:blk, LANES:], ol[blk:, LANES:])
                pos = (n * blk) * dil + r
                if dil == 1:
                    dst = pl.ds(pos, blk)
                else:
                    dst = pl.ds((pos % 4) * quarter + pos // 4, blk, stride=dil // 4)
                o_sc[g, dst, :] = o
                m_sc[g, dst, :] = m
                l_sc[g, dst, :] = l

    def merge(ci, carry):
        res, idx = ci // (quarter // blk), (ci % (quarter // blk)) * blk
        rows = ([pl.ds(4 * idx + res, blk, stride=4)]
                + [pl.ds(pl.multiple_of(res * quarter + idx, blk), blk)] * 2)
        ms = [m_sc[g, rows[g], :] for g in range(3)]
        m_all = jnp.maximum(jnp.maximum(ms[0], ms[1]), ms[2])
        num = jnp.zeros((blk, LANES), F32)
        den = jnp.zeros((blk, LANES), F32)
        for g in range(3):
            e = jnp.exp2(ms[g] - m_all)
            num = num + e * o_sc[g, rows[g], :]
            den = den + e * l_sc[g, rows[g], :]
        o_sc[0, rows[0], :] = num / den
        return carry

    lax.fori_loop(0, DIL_POS // blk, merge, 0, unroll=2)

    chunk = 256

    def finish(ci, carry):
        rows = pl.ds(pl.multiple_of(ci * chunk, chunk), chunk)
        o_ref[rows, :] = (o_sc[0, rows, :] * sg_ref[rows, :].astype(F32)).astype(BF16)
        return carry

    lax.fori_loop(0, DIL_POS // chunk, finish, 0)


def _dil_attn(slopes, qkv, sgate):
    B, S, _ = sgate.shape
    n_pairs = N_DIL_HEADS * DIL_HEAD_DIM // LANES
    blk = DIL_BLK
    in_specs = [pl.BlockSpec(memory_space=pltpu.SMEM)]
    args = [slopes]
    for g, dil in enumerate(DILATIONS):
        length = DIL_POS // dil
        nb = length // blk
        cur = pl.BlockSpec((None, dil, length, LANES), lambda b, hp, st: (b, 0, st, hp))
        halo = pl.BlockSpec((None, dil, blk, LANES),
                            lambda b, hp, st, nb=nb: (b, 0, jnp.maximum(st * nb - 1, 0), hp))
        q, k, v = qkv[g], qkv[3 + g], qkv[6 + g]
        in_specs += [cur, cur, halo, cur, halo]
        args += [q, k, k, v, v]
    pos = pl.BlockSpec((None, DIL_POS, LANES), lambda b, hp, st: (b, st, hp))
    in_specs.append(pos)
    args.append(sgate)
    return pl.pallas_call(
        _dil_kernel,
        grid=(B, n_pairs, S // DIL_POS),
        in_specs=in_specs,
        out_specs=pos,
        out_shape=jax.ShapeDtypeStruct((B, S, DIL_GROUP_W), BF16),
        scratch_shapes=[pltpu.VMEM((3, DIL_POS, LANES), F32)] * 3,
        compiler_params=_cparams(("arbitrary", "arbitrary", "arbitrary")),
        name="dil_attn",
    )(*args)


def _even_mixers(x, g_norm, w_in, b_f, g_q, g_k):
    B, S, D = x.shape
    n_f = N_DENSE_HEADS
    cut0, cut1 = 3 * DENSE_W, 3 * DENSE_W + n_f
    wa, wb = w_in[:, :cut0].astype(BF16), w_in[:, cut1:].astype(BF16)
    wf = jnp.zeros((BF16_SUBLANES, D), F32).at[:n_f].set(w_in[:, cut0:cut1].T).astype(BF16)
    bf = jnp.zeros((SUBLANES, LANES), F32).at[:n_f].set(jnp.broadcast_to(b_f[:, None], (n_f, LANES)))
    gq = (g_q * (DENSE_HEAD_DIM ** -0.5 * LOG2E)).reshape(1, DENSE_HEAD_DIM)
    gk = g_k.reshape(1, DENSE_HEAD_DIM)
    fq, fk, fv, sq, sk, sv, sgate, cum = _even_proj(x, g_norm.reshape(1, D), wa, wb, wf, bf, gq, gk)

    t = FOX_KEYS
    cend = cum[:, :n_f, t - 1::t].reshape(B * n_f, S // t)
    qk_bound = math.sqrt(DENSE_HEAD_DIM) * jnp.max(jnp.abs(g_q)) * jnp.max(jnp.abs(g_k)) * 1.02
    thr = ((EXP_ZERO + 1.0 + qk_bound) * LOG2E).reshape(1).astype(F32)
    mixed_f = _fox_attn(cend, thr, fq, fk, fv, cum, sgate)

    idx = jnp.arange(SB_BLOCK)
    upper = (idx[:, None] > idx[None, :]).astype(BF16)
    mixed_s = _sb_attn(sq, sk, sv, upper, sgate)
    return mixed_f, mixed_s


def _odd_layer(x, mixed_f, mixed_s, even_w_out, g_norm, w_in, g_q, g_k, w_out):
    B, S, D = x.shape
    wo_f, wo_s = even_w_out[:DENSE_W].astype(BF16), even_w_out[DENSE_W:].astype(BF16)
    w = w_in.astype(BF16)
    gq = jnp.tile(g_q * (DIL_HEAD_DIM ** -0.5 * LOG2E), N_DIL_HEADS).reshape(1, DIL_GROUP_W)
    gk = jnp.tile(g_k, N_DIL_HEADS).reshape(1, DIL_GROUP_W)
    outs = _odd_proj(x, mixed_f, mixed_s, wo_f, wo_s, g_norm.reshape(1, D), w, gq, gk)
    n_all = len(DILATIONS) * N_DIL_HEADS
    slopes = jnp.asarray([LOG2E * 2.0 ** (-8.0 * (i + 1) / n_all) for i in range(n_all)], F32)
    mixed = _dil_attn(slopes, outs[1:10], outs[10])
    return _odd_out(outs[0], mixed, w_out.astype(BF16))


def kernel(x, even_norm, even_w_in, even_b_f, even_q_gain, even_k_gain, even_w_out,
           odd_norm, odd_w_in, odd_q_gain, odd_k_gain, odd_w_out):
    assert x.shape[1] % DIL_POS == 0 and x.shape[2] == D_MODEL
    mixed_f, mixed_s = _even_mixers(x, even_norm[0], even_w_in[0], even_b_f[0], even_q_gain[0],
                                    even_k_gain[0])
    return _odd_layer(x, mixed_f, mixed_s, even_w_out[0], odd_norm[0], odd_w_in[0], odd_q_gain[0],
                      odd_k_gain[0], odd_w_out[0])
```

```python
import math

import jax
import jax.numpy as jnp
from jax import lax
from jax.experimental import pallas as pl
from jax.experimental.pallas import tpu as pltpu

F32 = jnp.float32
BF16 = jnp.bfloat16

D_MODEL = 1024
DENSE_HEAD_DIM = 128
N_DENSE_HEADS = 4
DENSE_W = N_DENSE_HEADS * DENSE_HEAD_DIM
DIL_HEAD_DIM = 64
N_DIL_HEADS = 8
DIL_GROUP_W = N_DIL_HEADS * DIL_HEAD_DIM
DILATIONS = (1, 4, 16)
DIL_SPAN = 128
DIL_BLK = 128
RMS_EPS = 1e-6

LANES = 128
SUBLANES = 8
BF16_SUBLANES = 16
PROJ_ROWS = 1024
ODD_OUT_ROWS = 2048
PROJ_SUB_ROWS = 256
ODD_PROJ_ROWS = 512
ATT_ROWS = 512
FOX_KEYS = 512
SB_ROWS = 1024
SB_BLOCK = 256
DIL_POS = DILATIONS[-1] * DIL_BLK
VMEM_LIMIT = 56 * 1024 * 1024

EXP_ZERO = 104.0
LOG2E = math.log2(math.e)

_NT = (((1,), (1,)), ((), ()))


def _cparams(sem):
    return pltpu.CompilerParams(dimension_semantics=sem, vmem_limit_bytes=VMEM_LIMIT)


def _resident_spec(shape):
    return pl.BlockSpec(shape, lambda *_: (0,) * len(shape), pipeline_mode=pl.Buffered(1))


def _log_sigmoid(z):
    return jnp.minimum(z, 0.0) - jnp.log(1.0 + jnp.exp(-jnp.abs(z)))


def _log2_sigmoid_pair(z2):
    soft = jnp.log2(1.0 + jnp.exp2(-jnp.abs(z2)))
    log_beta = jnp.minimum(z2, 0.0) - soft
    return log_beta, log_beta - z2


def _silu(x):
    return x * (1.0 / (1.0 + jnp.exp(-x)))


def _row_rms(x, g):
    ms = jnp.mean(x * x, axis=-1, keepdims=True)
    return x * lax.rsqrt(ms + RMS_EPS) * g


def _even_proj_kernel(x_ref, gn_ref, wa_ref, wb_ref, wf_ref, bf_ref, gq_ref, gk_ref,
                      fq_ref, fk_ref, fv_ref, sq_ref, sk_ref, sv_ref, sg_ref, cum_ref, carry_sc):
    i = pl.program_id(1)
    rows = x_ref.shape[0]
    sub = PROJ_SUB_ROWS
    lane = lax.broadcasted_iota(jnp.int32, (SUBLANES, sub), 1)

    @pl.when(i == 0)
    def _():
        carry_sc[...] = jnp.zeros_like(carry_sc)

    carry = carry_sc[:, 0:1]
    for r0 in range(0, rows, sub):
        rs = slice(r0, r0 + sub)
        h = _row_rms(x_ref[rs, :], gn_ref[...]).astype(BF16)

        fl = lax.dot_general(wf_ref[...], h, _NT, preferred_element_type=F32)[:SUBLANES] + bf_ref[:, 0:1]
        lf = _log_sigmoid(fl)
        shift = 1
        while shift < sub:
            lf = lf + jnp.where(lane >= shift, pltpu.roll(lf, shift, axis=1), 0.0)
            shift *= 2
        cum = lf + carry
        cum_ref[:, rs] = cum * LOG2E
        carry = cum[:, sub - 1:sub]

        def proj(w_ref, chunk, h=h):
            return jnp.dot(h, w_ref[:, chunk * DENSE_W:(chunk + 1) * DENSE_W], preferred_element_type=F32)

        def head_norm(acc, g_ref, out_ref, rs=rs):
            for hh in range(N_DENSE_HEADS):
                sl = slice(hh * DENSE_HEAD_DIM, (hh + 1) * DENSE_HEAD_DIM)
                out_ref[rs, sl] = _row_rms(acc[:, sl], g_ref[...]).astype(BF16)

        head_norm(proj(wa_ref, 0), gq_ref, fq_ref)
        head_norm(proj(wa_ref, 1), gk_ref, fk_ref)
        for c in range(2):
            sl = slice(c * DENSE_W, (c + 1) * DENSE_W)
            sg_ref[rs, sl] = _silu(proj(wb_ref, 3 + c)).astype(BF16)
        sq_ref[rs, :] = (proj(wb_ref, 0) * (DENSE_HEAD_DIM ** -0.5 * LOG2E)).astype(BF16)
        fv_ref[rs, :] = proj(wa_ref, 2).astype(BF16)
        sk_ref[rs, :] = proj(wb_ref, 1).astype(BF16)
        sv_ref[rs, :] = proj(wb_ref, 2).astype(BF16)
    carry_sc[...] = jnp.broadcast_to(carry, carry_sc.shape)


def _even_proj(x, g_norm, wa, wb, wf, b_f, g_q, g_k):
    B, S, D = x.shape
    tm = PROJ_ROWS
    row = lambda width: pl.BlockSpec((None, tm, width), lambda b, i: (b, i, 0))
    const = _resident_spec
    act = lambda width: jax.ShapeDtypeStruct((B, S, width), BF16)
    return pl.pallas_call(
        _even_proj_kernel,
        grid=(B, S // tm),
        in_specs=[row(D), const((1, D)), const(wa.shape), const(wb.shape), const(wf.shape),
                  const(b_f.shape), const(g_q.shape), const(g_k.shape)],
        out_specs=[row(DENSE_W)] * 6 + [row(2 * DENSE_W),
                                        pl.BlockSpec((None, SUBLANES, tm), lambda b, i: (b, 0, i))],
        out_shape=[act(DENSE_W)] * 6 + [act(2 * DENSE_W), jax.ShapeDtypeStruct((B, SUBLANES, S), F32)],
        scratch_shapes=[pltpu.VMEM((SUBLANES, LANES), F32)],
        compiler_params=_cparams(("arbitrary", "arbitrary")),
        name="even_proj",
    )(x, g_norm, wa, wb, wf, b_f, g_q, g_k)


def _fox_kernel(cend_ref, thr_ref, q_ref, k_ref, v_ref, c_ref, sg_ref, o_ref, m_sc, acc_sc, s_sc):
    h, i = pl.program_id(0), pl.program_id(1)
    nbatch, tq = q_ref.shape[0], q_ref.shape[1]
    tk = FOX_KEYS
    nsub = tq // tk
    batches = range(nbatch)
    c_base = [jnp.where(i > 0, cend_ref[b * N_DENSE_HEADS + h, jnp.maximum(i * nsub - 1, 0)], 0.0)
              for b in batches]

    def logits(b, start, width, rows):
        keys = pl.ds(pl.multiple_of(start, width), width)
        s = lax.dot_general(q_ref[b, rows, :], k_ref[b, keys, :], _NT, preferred_element_type=F32)
        return s + (c_base[b] - c_ref[b, pl.ds(h, 1), keys])

    def update(b, s, start, width, rows, first=False):
        keys = pl.ds(pl.multiple_of(start, width), width)
        m_cur = jnp.broadcast_to(jnp.max(s, axis=-1, keepdims=True), (s.shape[0], LANES))
        m_new = m_cur if first else jnp.maximum(m_sc[b, rows, :], m_cur)
        p = jnp.exp2(s - jnp.concatenate([m_new] * (width // LANES), axis=1))
        v_ones = jnp.concatenate([v_ref[b, keys, :], jnp.ones((width, LANES), BF16)], axis=1)
        pv = jnp.dot(p.astype(BF16), v_ones, preferred_element_type=F32)
        if first:
            acc_sc[b, rows, :] = pv
        else:
            alpha = jnp.exp2(m_sc[b, rows, :] - m_new)
            acc_sc[b, rows, :] = jnp.concatenate([alpha, alpha], axis=1) * acc_sc[b, rows, :] + pv
        m_sc[b, rows, :] = m_new

    for dd in range(nsub):
        rows = slice(dd * tk, tq)
        for b in batches:
            s = logits(b, i * tq + dd * tk, tk, rows)
            r_io = lax.broadcasted_iota(jnp.int32, s.shape, 0)
            c_io = lax.broadcasted_iota(jnp.int32, s.shape, 1)
            update(b, jnp.where(c_io <= r_io, s, -jnp.inf), i * tq + dd * tk, tk, rows, first=dd == 0)

    reach = thr_ref[0] - jnp.min(m_sc[...])

    def wanted(kt):
        ktc = jnp.maximum(kt, 0)
        far = c_base[0] - cend_ref[h, ktc]
        for b in batches[1:]:
            far = jnp.maximum(far, c_base[b] - cend_ref[b * N_DENSE_HEADS + h, ktc])
        return jnp.logical_and(kt >= 0, far >= -reach)

    everything = slice(0, tq)
    kt0 = i * nsub - 1
    for b in batches:
        s_sc[b] = logits(b, jnp.maximum(kt0, 0) * tk, tk, everything)

    def body(c):
        kt, _ = c
        for b in batches:
            s = s_sc[b]
            s_sc[b] = logits(b, jnp.maximum(kt - 1, 0) * tk, tk, everything)
            update(b, s, kt * tk, tk, everything)
        return kt - 1, wanted(kt - 1)

    lax.while_loop(lambda c: c[1], body, (kt0, wanted(kt0)))
    for b in batches:
        o_ref[b] = (acc_sc[b, :, :LANES] * (1.0 / acc_sc[b, :, LANES:])
                    * sg_ref[b].astype(F32)).astype(BF16)


def _fox_attn(cend, thr, fq, fk, fv, cum, sgate):
    B, S, _ = fq.shape
    tq = ATT_ROWS
    tile = pl.BlockSpec((B, tq, DENSE_HEAD_DIM), lambda h, i: (0, i, h))
    full = pl.BlockSpec((B, S, DENSE_HEAD_DIM), lambda h, i: (0, 0, h))
    smem = pl.BlockSpec(memory_space=pltpu.SMEM)
    return pl.pallas_call(
        _fox_kernel,
        grid=(N_DENSE_HEADS, S // tq),
        in_specs=[smem, smem, tile, full, full,
                  pl.BlockSpec((B, SUBLANES, S), lambda h, i: (0, 0, 0)), tile],
        out_specs=tile,
        out_shape=jax.ShapeDtypeStruct((B, S, DENSE_W), BF16),
        scratch_shapes=[pltpu.VMEM((B, tq, LANES), F32), pltpu.VMEM((B, tq, 2 * LANES), F32),
                        pltpu.VMEM((B, tq, FOX_KEYS), F32)],
        compiler_params=_cparams(("arbitrary", "arbitrary")),
        name="fox_attn",
    )(cend, thr, fq, fk, fv, cum, sgate)


def _sb_kernel(q_ref, k_ref, v_ref, u_ref, sg_ref, o_ref, off_sc, acc_sc):
    i = pl.program_id(1)
    nbatch, tq = q_ref.shape[0], q_ref.shape[1]
    r = SB_BLOCK
    nblk = tq // r
    chains = [(b, rb) for b in range(nbatch) for rb in range(nblk)]
    off_sc[...] = jnp.zeros_like(off_sc)
    acc_sc[...] = jnp.zeros_like(acc_sc)
    r_io = lax.broadcasted_iota(jnp.int32, (r, r), 0)
    c_io = lax.broadcasted_iota(jnp.int32, (r, r), 1)
    strict = c_io < r_io

    def walk(j, diagonal):
        rows = [slice(rb * r, (rb + 1) * r) for _, rb in chains]
        kts = [i * nblk + rb - j for _, rb in chains]
        keys = [pl.ds(pl.multiple_of(jnp.maximum(kt, 0) * r, r), r) for kt in kts]
        zs = [lax.dot_general(q_ref[b, rows[c], :], k_ref[b, keys[c], :], _NT, preferred_element_type=F32)
              for c, (b, _) in enumerate(chains)]
        pairs, offs, laters = [], [], []
        for c, (b, _) in enumerate(chains):
            log_beta, log_om = _log2_sigmoid_pair(zs[c])
            off = off_sc[b, rows[c], :]
            if diagonal:
                log_om = jnp.where(strict, log_om, 0.0)
            else:
                off = jnp.where(kts[c] < 0, -1e30, off)
            pairs.append((log_beta, log_om))
            offs.append(off)
            laters.append(jnp.dot(log_om.astype(BF16), u_ref[...], preferred_element_type=F32)
                          + jnp.concatenate([off] * (r // LANES), axis=1))
        for c, (b, _) in enumerate(chains):
            log_beta, log_om = pairs[c]
            w = jnp.exp2(log_beta + laters[c])
            if diagonal:
                w = jnp.where(strict, w, 0.0)
            acc_sc[b, rows[c], :] += jnp.dot(w.astype(BF16), v_ref[b, keys[c], :],
                                             preferred_element_type=F32)
            off_sc[b, rows[c], :] = offs[c] + jnp.sum(log_om, axis=-1, keepdims=True)

    walk(0, True)

    def body(c):
        j, _ = c
        walk(j, False)
        alive = jnp.max(off_sc[...]) >= -(EXP_ZERO + 8.0) * LOG2E
        keys_left = i * nblk + nblk - 1 - (j + 1) >= 0
        return j + 1, jnp.logical_and(keys_left, alive)

    lax.while_loop(lambda c: c[1], body, (1, True))
    for b in range(nbatch):
        o_ref[b] = (acc_sc[b] * sg_ref[b].astype(F32)).astype(BF16)


def _sb_attn(sq, sk, sv, upper, sgate):
    B, S, _ = sq.shape
    tq, tk = SB_ROWS, SB_BLOCK
    tile = pl.BlockSpec((B, tq, DENSE_HEAD_DIM), lambda h, i: (0, i, h))
    gate_tile = pl.BlockSpec((B, tq, DENSE_HEAD_DIM), lambda h, i: (0, i, N_DENSE_HEADS + h))
    full = pl.BlockSpec((B, S, DENSE_HEAD_DIM), lambda h, i: (0, 0, h))
    return pl.pallas_call(
        _sb_kernel,
        grid=(N_DENSE_HEADS, S // tq),
        in_specs=[tile, full, full, _resident_spec((tk, tk)), gate_tile],
        out_specs=tile,
        out_shape=jax.ShapeDtypeStruct((B, S, DENSE_W), BF16),
        scratch_shapes=[pltpu.VMEM((B, tq, LANES), F32), pltpu.VMEM((B, tq, DENSE_HEAD_DIM), F32)],
        compiler_params=_cparams(("arbitrary", "arbitrary")),
        name="sb_attn",
    )(sq, sk, sv, upper, sgate)


def _odd_out_kernel(x_ref, m_ref, w_ref, o_ref):
    o_ref[...] = x_ref[...] + jnp.dot(m_ref[...], w_ref[...], preferred_element_type=F32)


def _odd_out(x, mixed, wo):
    B, S, D = x.shape
    tm = ODD_OUT_ROWS
    row = lambda width: pl.BlockSpec((None, tm, width), lambda b, i: (b, i, 0))
    return pl.pallas_call(
        _odd_out_kernel,
        grid=(B, S // tm),
        in_specs=[row(D), row(DIL_GROUP_W), _resident_spec(wo.shape)],
        out_specs=row(D),
        out_shape=jax.ShapeDtypeStruct((B, S, D), F32),
        compiler_params=_cparams(("arbitrary", "arbitrary")),
        name="odd_out",
    )(x, mixed, wo)


def _odd_proj_kernel(x_ref, mf_ref, ms_ref, wof_ref, wos_ref, gn_ref, w_ref, gq_ref, gk_ref, res_ref, *rest):
    out_refs, sg_ref, h_sc, h4_sc = rest[:9], rest[9], rest[10], rest[11]
    rows = x_ref.shape[0]
    n_slab = D_MODEL // LANES
    res = (x_ref[...]
           + jnp.dot(mf_ref[...], wof_ref[...], preferred_element_type=F32)
           + jnp.dot(ms_ref[...], wos_ref[...], preferred_element_type=F32))
    res_ref[...] = res
    hf = _row_rms(res, gn_ref[...])
    h = hf.astype(BF16)
    for cb in range(n_slab):
        h_sc[cb] = hf[:, cb * LANES:(cb + 1) * LANES]

    per4 = rows // 4
    for r4 in range(4):
        for cb in range(n_slab):
            h4_sc[cb, r4 * per4:(r4 + 1) * per4, :] = h_sc[cb, pl.ds(r4, per4, stride=4), :]
    lhs4 = jnp.concatenate([h4_sc[cb] for cb in range(n_slab)], axis=1).astype(BF16)
    lhs16 = jnp.concatenate(
        [jnp.concatenate([h4_sc[cb, pl.ds((r % 4) * per4 + r // 4, per4 // 4, stride=4), :]
                          for cb in range(n_slab)], axis=1) for r in range(16)], axis=0).astype(BF16)
    assert DILATIONS == (1, 4, 16)
    lhs = [h, lhs4, lhs16]

    lane_lo = lax.broadcasted_iota(jnp.int32, (rows, LANES), 1) < DIL_HEAD_DIM

    def head_norm(acc, g_ref):
        cols = []
        for cb in range(DIL_GROUP_W // LANES):
            x = acc[:, cb * LANES:(cb + 1) * LANES]
            x2 = x * x
            lo = jnp.sum(jnp.where(lane_lo, x2, 0.0), axis=-1, keepdims=True)
            hi = jnp.sum(jnp.where(lane_lo, 0.0, x2), axis=-1, keepdims=True)
            ms = jnp.where(lane_lo, lo, hi) * (1.0 / DIL_HEAD_DIM)
            cols.append(x * lax.rsqrt(ms + RMS_EPS))
        return jnp.concatenate(cols, axis=1) * g_ref[...]

    gate = jnp.dot(h, w_ref[:, 9 * DIL_GROUP_W:10 * DIL_GROUP_W], preferred_element_type=F32)
    sg_ref[...] = _silu(gate).astype(BF16)
    for c in sorted(range(9), key=lambda c: (c % 3, c)):
        kind, g = divmod(c, 3)
        acc = jnp.dot(lhs[g], w_ref[:, c * DIL_GROUP_W:(c + 1) * DIL_GROUP_W],
                      preferred_element_type=F32)
        if kind == 0:
            acc = head_norm(acc, gq_ref)
        elif kind == 1:
            acc = head_norm(acc, gk_ref)
        dil = DILATIONS[g]
        per = rows // dil
        for r in range(dil):
            out_refs[c][r] = acc[r * per:(r + 1) * per, :].astype(BF16)


def _odd_proj(x, mixed_f, mixed_s, wo_f, wo_s, g_norm, w, g_q, g_k):
    B, S, D = x.shape
    tm = ODD_PROJ_ROWS
    const = _resident_spec
    row = lambda width: pl.BlockSpec((None, tm, width), lambda b, i: (b, i, 0))
    out_specs, out_shape = [row(D)], [jax.ShapeDtypeStruct((B, S, D), F32)]
    for _ in range(3):
        for dil in DILATIONS:
            out_specs.append(pl.BlockSpec((None, dil, tm // dil, DIL_GROUP_W), lambda b, i: (b, 0, i, 0)))
            out_shape.append(jax.ShapeDtypeStruct((B, dil, S // dil, DIL_GROUP_W), BF16))
    out_specs.append(pl.BlockSpec((None, tm, DIL_GROUP_W), lambda b, i: (b, i, 0)))
    out_shape.append(jax.ShapeDtypeStruct((B, S, DIL_GROUP_W), BF16))
    return pl.pallas_call(
        _odd_proj_kernel,
        grid=(B, S // tm),
        in_specs=[row(D), row(DENSE_W), row(DENSE_W), const(wo_f.shape), const(wo_s.shape),
                  const((1, D)), const(w.shape), const(g_q.shape), const(g_k.shape)],
        out_specs=out_specs,
        out_shape=out_shape,
        scratch_shapes=[pltpu.VMEM((D // LANES, tm, LANES), F32)] * 2,
        compiler_params=_cparams(("arbitrary", "arbitrary")),
        name="odd_proj",
    )(x, mixed_f, mixed_s, wo_f, wo_s, g_norm, w, g_q, g_k)


def _dil_kernel(slope_ref, *refs):
    ins, (sg_ref, o_ref), (o_sc, m_sc, l_sc) = refs[:15], refs[15:17], refs[17:]
    hp, st = pl.program_id(1), pl.program_id(2)
    blk = DIL_BLK
    quarter = DIL_POS // 4

    a_io = lax.broadcasted_iota(jnp.int32, (blk, 2 * blk), 0)
    c_io = lax.broadcasted_iota(jnp.int32, (blk, 2 * blk), 1)
    dist = a_io - c_io + blk
    band = jnp.logical_and(dist >= 0, dist <= DIL_SPAN)
    neg = jnp.where(band, 0.0, -jnp.inf)
    neg_first = jnp.where(jnp.logical_and(band, c_io >= blk), 0.0, -jnp.inf)
    dist_f = dist.astype(F32)
    lane_lo = lax.broadcasted_iota(jnp.int32, (blk, LANES), 1) < DIL_HEAD_DIM

    for g, dil in enumerate(DILATIONS):
        q_ref, kc_ref, kh_ref, vc_ref, vh_ref = ins[5 * g:5 * g + 5]
        nb = DIL_POS // dil // blk
        alibi = [dist_f * (slope_ref[g * N_DIL_HEADS + 2 * hp + hd] * float(dil)) for hd in range(2)]
        bias = jnp.concatenate([neg - alibi[0], neg - alibi[1]], axis=0)
        bias_first = jnp.concatenate([neg_first - alibi[0], neg_first - alibi[1]], axis=0)
        ones = jnp.ones((2 * blk, LANES), BF16)

        for r in range(dil):
            for n in range(nb):
                q2 = q_ref[r, n * blk:(n + 1) * blk, :]
                if n == 0:
                    k2 = jnp.concatenate([kh_ref[r], kc_ref[r, 0:blk, :]], axis=0)
                    v2 = jnp.concatenate([vh_ref[r], vc_ref[r, 0:blk, :]], axis=0)
                    b2 = jnp.where(st == 0, bias_first, bias)
                else:
                    k2 = kc_ref[r, (n - 1) * blk:(n + 1) * blk, :]
                    v2 = vc_ref[r, (n - 1) * blk:(n + 1) * blk, :]
                    b2 = bias
                zero = jnp.zeros_like(q2)
                q_st = jnp.concatenate([jnp.where(lane_lo, q2, zero), jnp.where(lane_lo, zero, q2)], axis=0)
                s = lax.dot_general(q_st, k2, _NT, preferred_element_type=F32) + b2
                m_st = jnp.max(s, axis=-1, keepdims=True)
                p = jnp.exp2(s - m_st)
                ol = jnp.dot(p.astype(BF16), jnp.concatenate([v2, ones], axis=1), preferred_element_type=F32)
                o = jnp.where(lane_lo, ol[:blk, :LANES], ol[blk:, :LANES])
                m = jnp.where(lane_lo, m_st[:blk], m_st[blk:])
                l = jnp.where(lane_lo, ol[:blk, LANES:], ol[blk:, LANES:])
                pos = (n * blk) * dil + r
                if dil == 1:
                    dst = pl.ds(pos, blk)
                else:
                    dst = pl.ds((pos % 4) * quarter + pos // 4, blk, stride=dil // 4)
                o_sc[g, dst, :] = o
                m_sc[g, dst, :] = m
                l_sc[g, dst, :] = l

    def merge(ci, carry):
        res, idx = ci // (quarter // blk), (ci % (quarter // blk)) * blk
        rows = ([pl.ds(4 * idx + res, blk, stride=4)]
                + [pl.ds(pl.multiple_of(res * quarter + idx, blk), blk)] * 2)
        ms = [m_sc[g, rows[g], :] for g in range(3)]
        m_all = jnp.maximum(jnp.maximum(ms[0], ms[1]), ms[2])
        num = jnp.zeros((blk, LANES), F32)
        den = jnp.zeros((blk, LANES), F32)
        for g in range(3):
            e = jnp.exp2(ms[g] - m_all)
            num = num + e * o_sc[g, rows[g], :]
            den = den + e * l_sc[g, rows[g], :]
        o_sc[0, rows[0], :] = num / den
        return carry

    lax.fori_loop(0, DIL_POS // blk, merge, 0, unroll=2)

    chunk = 256

    def finish(ci, carry):
        rows = pl.ds(pl.multiple_of(ci * chunk, chunk), chunk)
        o_ref[rows, :] = (o_sc[0, rows, :] * sg_ref[rows, :].astype(F32)).astype(BF16)
        return carry

    lax.fori_loop(0, DIL_POS // chunk, finish, 0)


def _dil_attn(slopes, qkv, sgate):
    B, S, _ = sgate.shape
    n_pairs = N_DIL_HEADS * DIL_HEAD_DIM // LANES
    blk = DIL_BLK
    in_specs = [pl.BlockSpec(memory_space=pltpu.SMEM)]
    args = [slopes]
    for g, dil in enumerate(DILATIONS):
        length = DIL_POS // dil
        nb = length // blk
        cur = pl.BlockSpec((None, dil, length, LANES), lambda b, hp, st: (b, 0, st, hp))
        halo = pl.BlockSpec((None, dil, blk, LANES),
                            lambda b, hp, st, nb=nb: (b, 0, jnp.maximum(st * nb - 1, 0), hp))
        q, k, v = qkv[g], qkv[3 + g], qkv[6 + g]
        in_specs += [cur, cur, halo, cur, halo]
        args += [q, k, k, v, v]
    pos = pl.BlockSpec((None, DIL_POS, LANES), lambda b, hp, st: (b, st, hp))
    in_specs.append(pos)
    args.append(sgate)
    return pl.pallas_call(
        _dil_kernel,
        grid=(B, n_pairs, S // DIL_POS),
        in_specs=in_specs,
        out_specs=pos,
        out_shape=jax.ShapeDtypeStruct((B, S, DIL_GROUP_W), BF16),
        scratch_shapes=[pltpu.VMEM((3, DIL_POS, LANES), F32)] * 3,
        compiler_params=_cparams(("arbitrary", "arbitrary", "arbitrary")),
        name="dil_attn",
    )(*args)


def _even_mixers(x, g_norm, w_in, b_f, g_q, g_k):
    B, S, D = x.shape
    n_f = N_DENSE_HEADS
    cut0, cut1 = 3 * DENSE_W, 3 * DENSE_W + n_f
    wa, wb = w_in[:, :cut0].astype(BF16), w_in[:, cut1:].astype(BF16)
    wf = jnp.zeros((BF16_SUBLANES, D), F32).at[:n_f].set(w_in[:, cut0:cut1].T).astype(BF16)
    bf = jnp.zeros((SUBLANES, LANES), F32).at[:n_f].set(jnp.broadcast_to(b_f[:, None], (n_f, LANES)))
    gq = (g_q * (DENSE_HEAD_DIM ** -0.5 * LOG2E)).reshape(1, DENSE_HEAD_DIM)
    gk = g_k.reshape(1, DENSE_HEAD_DIM)
    fq, fk, fv, sq, sk, sv, sgate, cum = _even_proj(x, g_norm.reshape(1, D), wa, wb, wf, bf, gq, gk)

    t = FOX_KEYS
    cend = cum[:, :n_f, t - 1::t].reshape(B * n_f, S // t)
    qk_bound = math.sqrt(DENSE_HEAD_DIM) * jnp.max(jnp.abs(g_q)) * jnp.max(jnp.abs(g_k)) * 1.02
    thr = ((EXP_ZERO + 1.0 + qk_bound) * LOG2E).reshape(1).astype(F32)
    mixed_f = _fox_attn(cend, thr, fq, fk, fv, cum, sgate)

    idx = jnp.arange(SB_BLOCK)
    upper = (idx[:, None] > idx[None, :]).astype(BF16)
    mixed_s = _sb_attn(sq, sk, sv, upper, sgate)
    return mixed_f, mixed_s


def _odd_layer(x, mixed_f, mixed_s, even_w_out, g_norm, w_in, g_q, g_k, w_out):
    B, S, D = x.shape
    wo_f, wo_s = even_w_out[:DENSE_W].astype(BF16), even_w_out[DENSE_W:].astype(BF16)
    w = w_in.astype(BF16)
    gq = jnp.tile(g_q * (DIL_HEAD_DIM ** -0.5 * LOG2E), N_DIL_HEADS).reshape(1, DIL_GROUP_W)
    gk = jnp.tile(g_k, N_DIL_HEADS).reshape(1, DIL_GROUP_W)
    outs = _odd_proj(x, mixed_f, mixed_s, wo_f, wo_s, g_norm.reshape(1, D), w, gq, gk)
    n_all = len(DILATIONS) * N_DIL_HEADS
    slopes = jnp.asarray([LOG2E * 2.0 ** (-8.0 * (i + 1) / n_all) for i in range(n_all)], F32)
    mixed = _dil_attn(slopes, outs[1:10], outs[10])
    return _odd_out(outs[0], mixed, w_out.astype(BF16))


def kernel(x, even_norm, even_w_in, even_b_f, even_q_gain, even_k_gain, even_w_out,
           odd_norm, odd_w_in, odd_q_gain, odd_k_gain, odd_w_out):
    assert x.shape[1] % DIL_POS == 0 and x.shape[2] == D_MODEL
    mixed_f, mixed_s = _even_mixers(x, even_norm[0], even_w_in[0], even_b_f[0], even_q_gain[0],
                                    even_k_gain[0])
    return _odd_layer(x, mixed_f, mixed_s, even_w_out[0], odd_norm[0], odd_w_in[0], odd_q_gain[0],
                      odd_k_gain[0], odd_w_out[0])
```
